```python
import math
import jax, jax.numpy as jnp
from jax import lax
import numpy as np

D_MODEL = 1024
BATCH = 4
SEQ = 8192
DEPTH = 1

N_MEM = 256
EPS = 1e-6
ATTN_HEADS = 8
ATTN_KV_HEADS = 2
ATTN_HEAD_DIM = 64
WINDOW = 128
ATTN_BLOCK = 128
REL_BUCKETS = 32
REL_MAX_DIST = 128
DN_HEADS = 8
DN_KEY_DIM = 64
DN_VALUE_DIM = 64
DN_CONV = 5
DN_CHUNK = 64
MEM_HEADS = 4
MEM_HEAD_DIM = 128
D_FF = 2816
FFN_CONV = 3
N_BRANCH = 3

ATTN_Q = ATTN_HEADS * ATTN_HEAD_DIM
ATTN_KV = ATTN_KV_HEADS * ATTN_HEAD_DIM
DN_QK = DN_HEADS * DN_KEY_DIM
DN_V = DN_HEADS * DN_VALUE_DIM
MEM_Q = MEM_HEADS * MEM_HEAD_DIM
IN_SPLITS = (ATTN_Q, ATTN_KV, ATTN_KV, DN_QK, DN_QK, DN_V, DN_V, 2 * DN_HEADS, 2 * DN_HEADS, MEM_Q, N_BRANCH * D_MODEL)
D_IN = sum(IN_SPLITS)

kernel_name = "hybrid_gated_swa_deltanet_memory_encoder"


def _split(t, sizes):
    idx = np.cumsum(sizes)[:-1].tolist()
    return jnp.split(t, idx, axis=-1)


def rmsnorm(x, g):
    x32 = x.astype(jnp.float32)
    y = x32 * lax.rsqrt(jnp.mean(x32 * x32, axis=-1, keepdims=True) + EPS)
    return (y * g.astype(jnp.float32)).astype(x.dtype)


def l2norm(x):
    x32 = x.astype(jnp.float32)
    return (x32 * lax.rsqrt(jnp.sum(x32 * x32, axis=-1, keepdims=True) + EPS)).astype(x.dtype)


def dwconv_centred(x, w):
    K = w.shape[0]
    p = K // 2
    S = x.shape[1]
    xp = jnp.pad(x, ((0, 0), (p, p), (0, 0)))
    y = xp[:, 0:S] * w[0]
    for j in range(1, K):
        y = y + xp[:, j:j + S] * w[j]
    return y


def _t5_buckets(rel):
    nb = REL_BUCKETS // 2
    max_exact = nb // 2
    ret = (rel > 0).astype(np.int32) * nb
    n = np.abs(rel)
    large = max_exact + (np.log(np.maximum(n, 1) / max_exact) / np.log(REL_MAX_DIST / max_exact) * (nb - max_exact)).astype(np.int32)
    large = np.minimum(large, nb - 1)
    return (ret + np.where(n < max_exact, n, large)).astype(np.int32)


def windowed_gqa(q, k, v, sink, rel_table):
    B, S, H, dh = q.shape
    Hkv = k.shape[2]
    G = H // Hkv
    T = ATTN_BLOCK
    nb = S // T
    t_idx = np.arange(T)[:, None]
    j_idx = np.arange(3 * T)[None, :]
    rel = j_idx - T - t_idx
    kpos = np.arange(nb)[:, None, None] * T - T + j_idx[None]
    valid = (np.abs(rel) <= WINDOW)[None] & (kpos >= 0) & (kpos < S)
    bias = jnp.transpose(rel_table[jnp.asarray(_t5_buckets(rel))], (2, 0, 1)).astype(jnp.float32)
    bias = bias.reshape(Hkv, G, T, 3 * T)

    qb = q.reshape(B, nb, T, Hkv, G, dh)

    def band(a):
        ap = jnp.pad(a, ((0, 0), (T, T), (0, 0), (0, 0))).reshape(B, nb + 2, T, Hkv, dh)
        return jnp.concatenate([ap[:, :-2], ap[:, 1:-1], ap[:, 2:]], axis=2)

    kb, vb = band(k), band(v)
    s = jnp.einsum('bnqkgd,bnjkd->bnkgqj', qb, kb).astype(jnp.float32) * (dh ** -0.5) + bias
    s = jnp.where(jnp.asarray(valid)[None, :, None, None], s, -jnp.inf)
    sk = sink.astype(jnp.float32).reshape(1, 1, Hkv, G, 1, 1)
    m = jnp.maximum(s.max(axis=-1, keepdims=True), sk)
    p = jnp.exp(s - m)
    p = p / (p.sum(axis=-1, keepdims=True) + jnp.exp(sk - m))
    o = jnp.einsum('bnkgqj,bnjkd->bnqkgd', p.astype(v.dtype), vb)
    return o.reshape(B, S, H * dh)


def gated_delta_rule(q, k, v, g, beta):
    B, S, H, dk = q.shape
    dv = v.shape[-1]
    C = DN_CHUNK
    n = S // C
    f32 = jnp.float32

    def ch(t):
        return t.astype(f32).reshape(B, n, C, H, -1).transpose(0, 3, 1, 2, 4)

    q, k, v = ch(q), ch(k), ch(v)
    g = g.astype(f32).reshape(B, n, C, H).transpose(0, 3, 1, 2)
    beta = beta.astype(f32).reshape(B, n, C, H).transpose(0, 3, 1, 2)
    gc = jnp.cumsum(g, axis=-1)
    lower = jnp.asarray(np.tril(np.ones((C, C), bool)))
    strict = jnp.asarray(np.tril(np.ones((C, C), bool), -1))
    decay = jnp.exp(jnp.where(lower, gc[..., :, None] - gc[..., None, :], -jnp.inf))
    kb = k * beta[..., None]
    L = jnp.where(strict, jnp.einsum('bhncd,bhnmd->bhncm', kb, k) * decay, 0.0)
    A = L + jnp.eye(C, dtype=f32)
    rhs = jnp.concatenate([v * beta[..., None], kb * jnp.exp(gc)[..., None]], axis=-1)
    sol = lax.linalg.triangular_solve(A, rhs, left_side=True, lower=True, unit_diagonal=True)
    u, w = sol[..., :dv], sol[..., dv:]
    qk = jnp.einsum('bhncd,bhnmd->bhncm', q, k) * decay
    g_last = gc[..., -1]
    q_in = q * jnp.exp(gc)[..., None]
    k_up = k * jnp.exp(g_last[..., None] - gc)[..., None]

    def step(state, inp):
        qi, ki, ui, wi, qki, gl = inp
        v_new = ui - jnp.einsum('bhcd,bhde->bhce', wi, state)
        o = jnp.einsum('bhcd,bhde->bhce', qi, state) + jnp.einsum('bhcm,bhme->bhce', qki, v_new)
        state = state * jnp.exp(gl)[..., None, None] + jnp.einsum('bhcd,bhce->bhde', ki, v_new)
        return state, o

    xs = (jnp.moveaxis(q_in, 2, 0), jnp.moveaxis(k_up, 2, 0), jnp.moveaxis(u, 2, 0),
          jnp.moveaxis(w, 2, 0), jnp.moveaxis(qk, 2, 0), jnp.moveaxis(g_last, 2, 0))
    state0 = jnp.zeros((B, H, dk, dv), f32)
    _, o = lax.scan(step, state0, xs)
    return o.transpose(1, 0, 3, 2, 4).reshape(B, S, H, dv)


def deltanet_branch(dq, dk, dv, dz, db, da, conv_w, a_log, dt_bias, out_g):
    B, S, _ = dq.shape
    f32 = jnp.float32
    qkv = jax.nn.silu(dwconv_centred(jnp.concatenate([dq, dk, dv], axis=-1), conv_w))
    q, k, v = _split(qkv, (DN_QK, DN_QK, DN_V))
    q = l2norm(q.reshape(B, S, DN_HEADS, DN_KEY_DIM)) * (DN_KEY_DIM ** -0.5)
    k = l2norm(k.reshape(B, S, DN_HEADS, DN_KEY_DIM))
    v = v.reshape(B, S, DN_HEADS, DN_VALUE_DIM)
    beta = jax.nn.sigmoid(db.astype(f32)).reshape(B, S, 2, DN_HEADS)
    g = -jnp.exp(a_log.astype(f32)) * jax.nn.softplus(da.astype(f32).reshape(B, S, 2, DN_HEADS) + dt_bias.astype(f32))
    o_fwd = gated_delta_rule(q, k, v, g[:, :, 0], beta[:, :, 0])
    o_bwd = jnp.flip(gated_delta_rule(jnp.flip(q, 1), jnp.flip(k, 1), jnp.flip(v, 1),
                                      jnp.flip(g[:, :, 1], 1), jnp.flip(beta[:, :, 1], 1)), 1)
    z = dz.reshape(B, S, DN_HEADS, DN_VALUE_DIM).astype(f32)
    o = rmsnorm(o_fwd + o_bwd, out_g) * jax.nn.silu(z)
    return o.reshape(B, S, DN_V).astype(dq.dtype)


def memory_cross_attention(mq, mem_n, w_kv, q_g, k_g):
    B, S, _ = mq.shape
    M = mem_n.shape[1]
    k, v = jnp.split(mem_n @ w_kv, 2, axis=-1)
    q = rmsnorm(mq.reshape(B, S, MEM_HEADS, MEM_HEAD_DIM), q_g)
    k = rmsnorm(k.reshape(B, M, MEM_HEADS, MEM_HEAD_DIM), k_g)
    v = v.reshape(B, M, MEM_HEADS, MEM_HEAD_DIM)
    s = jnp.einsum('bshd,bmhd->bhsm', q, k).astype(jnp.float32) * (MEM_HEAD_DIM ** -0.5)
    p = jax.nn.softmax(s, axis=-1).astype(v.dtype)
    return jnp.einsum('bhsm,bmhd->bshd', p, v).reshape(B, S, MEM_Q)


def setup_inputs(seed: int = 0) -> dict:
    key = jax.random.key(seed)
    ks = jax.random.split(key, 26)
    f32 = jnp.float32
    L = DEPTH

    def nrm(k, shape, scale):
        return jax.random.normal(k, shape, f32) * scale

    def gain(k, shape):
        return 1.0 + 0.05 * jax.random.normal(k, shape, f32)

    dt = jnp.exp(jax.random.uniform(ks[9], (L, 2, DN_HEADS), f32, math.log(1e-3), math.log(1e-1)))
    return {
        "x": nrm(ks[0], (BATCH, SEQ, D_MODEL), 1.0),
        "mem": nrm(ks[1], (BATCH, N_MEM, D_MODEL), 1.0),
        "rel_bias_table": nrm(ks[2], (REL_BUCKETS, ATTN_HEADS), 0.5),
        "norm_mix_g": gain(ks[3], (L, D_MODEL)),
        "w_in": nrm(ks[4], (L, D_MODEL, D_IN), D_MODEL ** -0.5),
        "attn_q_norm_g": gain(ks[5], (L, ATTN_HEAD_DIM)),
        "attn_k_norm_g": gain(ks[6], (L, ATTN_HEAD_DIM)),
        "attn_sink": nrm(ks[7], (L, ATTN_HEADS), 1.0),
        "dn_conv_w": nrm(ks[8], (L, DN_CONV, 2 * DN_QK + DN_V), DN_CONV ** -0.5),
        "dn_a_log": jnp.log(jax.random.uniform(ks[10], (L, 2, DN_HEADS), f32, 1.0, 16.0)),
        "dn_dt_bias": dt + jnp.log(-jnp.expm1(-dt)),
        "dn_out_norm_g": gain(ks[11], (L, DN_VALUE_DIM)),
        "mem_norm_g": gain(ks[12], (L, D_MODEL)),
        "mem_w_kv": nrm(ks[13], (L, D_MODEL, 2 * MEM_Q), D_MODEL ** -0.5),
        "mem_q_norm_g": gain(ks[14], (L, MEM_HEAD_DIM)),
        "mem_k_norm_g": gain(ks[15], (L, MEM_HEAD_DIM)),
        "w_br_attn": nrm(ks[16], (L, ATTN_Q, D_MODEL), ATTN_Q ** -0.5),
        "w_br_dn": nrm(ks[17], (L, DN_V, D_MODEL), DN_V ** -0.5),
        "w_br_mem": nrm(ks[18], (L, MEM_Q, D_MODEL), MEM_Q ** -0.5),
        "w_out": nrm(ks[19], (L, D_MODEL, D_MODEL), D_MODEL ** -0.5),
        "norm_ffn_g": gain(ks[20], (L, D_MODEL)),
        "ffn_w_up": nrm(ks[21], (L, D_MODEL, 2 * D_FF), D_MODEL ** -0.5),
        "ffn_conv_w": nrm(ks[22], (L, FFN_CONV, 2 * D_FF), FFN_CONV ** -0.5),
        "ffn_conv_b": nrm(ks[23], (L, 2 * D_FF), 0.01),
        "ffn_w_down": nrm(ks[24], (L, D_FF, D_MODEL), D_FF ** -0.5),
    }


def reference(x, mem, rel_bias_table, norm_mix_g, w_in, attn_q_norm_g, attn_k_norm_g, attn_sink,
              dn_conv_w, dn_a_log, dn_dt_bias, dn_out_norm_g, mem_norm_g, mem_w_kv, mem_q_norm_g,
              mem_k_norm_g, w_br_attn, w_br_dn, w_br_mem, w_out, norm_ffn_g, ffn_w_up, ffn_conv_w,
              ffn_conv_b, ffn_w_down):
    B, S, _ = x.shape
    for l in range(DEPTH):
        h = rmsnorm(x, norm_mix_g[l])
        aq, ak, av, dq, dk, dv, dz, db, da, mq, gates = _split(h @ w_in[l], IN_SPLITS)
        y_attn = windowed_gqa(
            rmsnorm(aq.reshape(B, S, ATTN_HEADS, ATTN_HEAD_DIM), attn_q_norm_g[l]),
            rmsnorm(ak.reshape(B, S, ATTN_KV_HEADS, ATTN_HEAD_DIM), attn_k_norm_g[l]),
            av.reshape(B, S, ATTN_KV_HEADS, ATTN_HEAD_DIM),
            attn_sink[l], rel_bias_table)
        y_dn = deltanet_branch(dq, dk, dv, dz, db, da, dn_conv_w[l], dn_a_log[l], dn_dt_bias[l], dn_out_norm_g[l])
        y_mem = memory_cross_attention(mq, rmsnorm(mem, mem_norm_g[l]), mem_w_kv[l], mem_q_norm_g[l], mem_k_norm_g[l])
        g_attn, g_dn, g_mem = jnp.split(jax.nn.sigmoid(gates), N_BRANCH, axis=-1)
        merged = (g_attn * (y_attn @ w_br_attn[l]) + g_dn * (y_dn @ w_br_dn[l])
                  + g_mem * (y_mem @ w_br_mem[l]))
        x = x + merged @ w_out[l]
        h = rmsnorm(x, norm_ffn_g[l])
        u = dwconv_centred(h @ ffn_w_up[l], ffn_conv_w[l]) + ffn_conv_b[l]
        u_gate, u_val = jnp.split(u, 2, axis=-1)
        x = x + (jax.nn.silu(u_gate) * u_val) @ ffn_w_down[l]
    return x
```

```python
import functools
import math

import numpy as np
import jax
import jax.numpy as jnp
from jax import lax
from jax.experimental import pallas as pl
from jax.experimental.pallas import tpu as pltpu

F32 = jnp.float32
BF16 = jnp.bfloat16

EPS = 1e-6
D_MODEL = 1024
N_MEM = 256
ATTN_HEADS = 8
ATTN_KV_HEADS = 2
ATTN_HEAD_DIM = 64
WINDOW = 128
ATTN_BLOCK = 128
REL_BUCKETS = 32
REL_MAX_DIST = 128
DN_HEADS = 8
DN_KEY_DIM = 64
DN_VALUE_DIM = 64
DN_CONV = 5
DN_CHUNK = 64
MEM_HEADS = 4
MEM_HEAD_DIM = 128
D_FF = 2816
FFN_CONV = 3
N_BRANCH = 3

ATTN_Q = ATTN_HEADS * ATTN_HEAD_DIM
ATTN_KV = ATTN_KV_HEADS * ATTN_HEAD_DIM
DN_QK = DN_HEADS * DN_KEY_DIM
DN_V = DN_HEADS * DN_VALUE_DIM
MEM_Q = MEM_HEADS * MEM_HEAD_DIM
IN_SPLITS = (ATTN_Q, ATTN_KV, ATTN_KV, DN_QK, DN_QK, DN_V, DN_V, 2 * DN_HEADS, 2 * DN_HEADS, MEM_Q,
             N_BRANCH * D_MODEL)

LANES = 128
HALF = 64
HALO = 8
NEG = -1e30
VMEM_LIMIT = 56 * 1024 * 1024

ROW_TILE = 512
FF_CHUNK = 256


def _cparams(sem):
    return pltpu.CompilerParams(dimension_semantics=sem, vmem_limit_bytes=VMEM_LIMIT)


def _dot(a, b):
    return jnp.dot(a, b, preferred_element_type=F32)


def _dot_nt(a, b):
    return lax.dot_general(a, b, (((1,), (1,)), ((), ())), preferred_element_type=F32)


def _dot_tn(a, b):
    return lax.dot_general(a, b, (((0,), (0,)), ((), ())), preferred_element_type=F32)


def _lane_is_low(shape):
    lane = lax.broadcasted_iota(jnp.int32, shape, len(shape) - 1)
    return (lane % LANES) < HALF


def _half_sums(sq):
    low = _lane_is_low(sq.shape)
    s_lo = jnp.sum(jnp.where(low, sq, 0.0), axis=-1, keepdims=True)
    s_hi = jnp.sum(jnp.where(low, 0.0, sq), axis=-1, keepdims=True)
    return jnp.where(low, s_lo, s_hi)


def _silu(x):
    return x * (1.0 / (1.0 + jnp.exp(-x)))


def _sigmoid(x):
    return 1.0 / (1.0 + jnp.exp(-x))


_C_AQ = (0, 512)
_C_AKV = (512, 768)
_C_DQKV = (768, 2304)
_C_DZ = (2304, 2816)
_C_MQ = (2816, 3328)
_C_GATE = (3328, 6400)
_C_BA = (6400, 6528)
_N_IN = 6528


def _inproj_kernel(x_ref, g_ref, w_ref, aq_ref, akv_ref, dqkv_ref, dz_ref, mq_ref, gate_ref, ba_ref):
    x = x_ref[...]
    ms = jnp.mean(x * x, axis=-1, keepdims=True)
    h = ((x * lax.rsqrt(ms + EPS)) * g_ref[...]).astype(BF16)

    def proj(c):
        return _dot(h, w_ref[:, c[0]:c[1]])

    aq_ref[...] = proj(_C_AQ).astype(BF16)
    akv_ref[...] = proj(_C_AKV).astype(BF16)
    dqkv_ref[...] = proj(_C_DQKV).astype(BF16)
    dz_ref[...] = proj(_C_DZ).astype(BF16)
    mq_ref[...] = proj(_C_MQ).astype(BF16)
    gate_ref[...] = _sigmoid(proj(_C_GATE)).astype(BF16)
    ba_ref[...] = proj(_C_BA)


def _inproj(x2, g, w):
    n = x2.shape[0]
    tm = ROW_TILE
    widths = [(512, BF16), (256, BF16), (1536, BF16), (512, BF16), (512, BF16), (3072, BF16), (128, F32)]
    return pl.pallas_call(
        _inproj_kernel,
        grid=(n // tm,),
        in_specs=[pl.BlockSpec((tm, D_MODEL), lambda i: (i, 0)),
                  pl.BlockSpec((1, D_MODEL), lambda i: (0, 0)),
                  pl.BlockSpec((D_MODEL, _N_IN), lambda i: (0, 0))],
        out_specs=[pl.BlockSpec((tm, w_), lambda i: (i, 0)) for w_, _ in widths],
        out_shape=[jax.ShapeDtypeStruct((n, w_), dt) for w_, dt in widths],
        compiler_params=_cparams(("parallel",)),
        name="inproj",
    )(x2, g, w)


def _t5_buckets(rel):
    nb = REL_BUCKETS // 2
    max_exact = nb // 2
    ret = (rel > 0).astype(np.int32) * nb
    n = np.abs(rel)
    large = max_exact + (np.log(np.maximum(n, 1) / max_exact) / np.log(REL_MAX_DIST / max_exact)
                         * (nb - max_exact)).astype(np.int32)
    large = np.minimum(large, nb - 1)
    return (ret + np.where(n < max_exact, n, large)).astype(np.int32)


_ATTN_GROUPS = ((0, 1, True, False), (0, 1, False, True), (2, 3, True, True), (2, 3, False, False))
_ATTN_GROUP_HEADS = ((0, 2), (1, 3), (4, 6), (5, 7))


def _attn_kernel(q_ref, kp_ref, kc_ref, kn_ref, bias_ref, sink_ref, qg_ref, kg_ref, o_ref):
    n = pl.program_id(1)
    nb = pl.num_programs(1)
    T = ATTN_BLOCK
    q = q_ref[0].astype(F32)
    kv = jnp.concatenate([kp_ref[0], kc_ref[0], kn_ref[0]], axis=0).astype(F32)
    k = kv[:, :LANES]
    v = kv[:, LANES:]
    qg = qg_ref[...]
    kg = kg_ref[...]
    k = k * lax.rsqrt(_half_sums(k * k) * (1.0 / ATTN_HEAD_DIM) + EPS) * kg
    k_b = k.astype(BF16)
    k_sw = pltpu.roll(k, HALF, axis=1).astype(BF16)
    v_b = v.astype(BF16)
    v_sw = pltpu.roll(v, HALF, axis=1).astype(BF16)
    low_q = _lane_is_low((T, LANES))
    qn = []
    for pc in range(4):
        qs = q[:, pc * LANES:(pc + 1) * LANES]
        qn.append(qs * lax.rsqrt(_half_sums(qs * qs) * (1.0 / ATTN_HEAD_DIM) + EPS) * qg)
    col = lax.broadcasted_iota(jnp.int32, (2 * T, 3 * T), 1)
    edge = ((col < T) & (n == 0)) | ((col >= 2 * T) & (n == nb - 1))
    scale = ATTN_HEAD_DIM ** -0.5
    res = []
    for gi, (pa, pb, low, swapped) in enumerate(_ATTN_GROUPS):
        sel = low_q if low else jnp.logical_not(low_q)
        lhs = jnp.concatenate([jnp.where(sel, qn[pa], 0.0), jnp.where(sel, qn[pb], 0.0)], axis=0).astype(BF16)
        s = _dot_nt(lhs, k_sw if swapped else k_b) * scale + bias_ref[gi]
        s = jnp.where(edge, NEG, s)
        sk = sink_ref[gi]
        m = jnp.maximum(jnp.max(s, axis=-1, keepdims=True), sk)
        p = jnp.exp(s - m)
        den = jnp.sum(p, axis=-1, keepdims=True) + jnp.exp(sk - m)
        r = _dot(p.astype(BF16), v_sw if swapped else v_b)
        res.append(r * (1.0 / den))
    for pc, (ge, go) in enumerate(((0, 1), (0, 1), (2, 3), (2, 3))):
        r0 = (pc % 2) * T
        out = jnp.where(low_q, res[ge][r0:r0 + T], res[go][r0:r0 + T])
        o_ref[0, :, pc * LANES:(pc + 1) * LANES] = out.astype(BF16)


def _attn(aq, akv, rel_table, sink, qg, kg):
    B, S, _ = aq.shape
    T = ATTN_BLOCK
    nb = S // T
    t_idx = np.arange(T)[:, None]
    j_idx = np.arange(3 * T)[None, :]
    rel = j_idx - T - t_idx
    in_win = jnp.asarray(np.abs(rel) <= WINDOW)
    bias = jnp.transpose(rel_table[jnp.asarray(_t5_buckets(rel))], (2, 0, 1)).astype(F32)
    bias = jnp.where(in_win[None], bias, NEG)
    bias_g = jnp.stack([jnp.concatenate([bias[a], bias[b]], axis=0) for a, b in _ATTN_GROUP_HEADS])
    sk = sink.astype(F32)
    sink_g = jnp.stack([jnp.concatenate([jnp.full((T, 1), 1.0) * sk[a], jnp.full((T, 1), 1.0) * sk[b]], axis=0)
                        for a, b in _ATTN_GROUP_HEADS])
    qg2 = jnp.tile(qg.astype(F32), 2)[None]
    kg2 = jnp.tile(kg.astype(F32), 2)[None]
    return pl.pallas_call(
        _attn_kernel,
        grid=(B, nb),
        in_specs=[pl.BlockSpec((1, T, ATTN_Q), lambda b, i: (b, i, 0)),
                  pl.BlockSpec((1, T, 2 * ATTN_KV), lambda b, i: (b, jnp.maximum(i - 1, 0), 0)),
                  pl.BlockSpec((1, T, 2 * ATTN_KV), lambda b, i: (b, i, 0)),
                  pl.BlockSpec((1, T, 2 * ATTN_KV), lambda b, i: (b, jnp.minimum(i + 1, nb - 1), 0)),
                  pl.BlockSpec((4, 2 * T, 3 * T), lambda b, i: (0, 0, 0)),
                  pl.BlockSpec((4, 2 * T, 1), lambda b, i: (0, 0, 0)),
                  pl.BlockSpec((1, LANES), lambda b, i: (0, 0)),
                  pl.BlockSpec((1, LANES), lambda b, i: (0, 0))],
        out_specs=pl.BlockSpec((1, T, ATTN_Q), lambda b, i: (b, i, 0)),
        out_shape=jax.ShapeDtypeStruct((B, S, ATTN_Q), BF16),
        compiler_params=_cparams(("parallel", "parallel")),
        name="attn",
    )(aq, akv, akv, akv, bias_g, sink_g, qg2, kg2)


def _split_hi_lo(x):
    hi = x.astype(BF16)
    lo = (x - hi.astype(F32)).astype(BF16)
    return hi, lo


def _dn_prep_kernel(xp_ref, xc_ref, xn_ref, ba_ref, cw_ref, alog_ref, dtb_ref, tri_ref, expand_ref,
                    qkv_ref, betax_ref, gcx_ref, gct_ref):
    i = pl.program_id(1)
    nt = pl.num_programs(1)
    tm = xc_ref.shape[1]
    prev = jnp.where(i == 0, 0.0, xp_ref[0].astype(F32))
    nxt = jnp.where(i == nt - 1, 0.0, xn_ref[0].astype(F32))
    xe = jnp.concatenate([prev, xc_ref[0].astype(F32), nxt], axis=0)
    rows = tm + 2 * HALO
    acc = None
    for j in range(DN_CONV):
        sh = (DN_CONV // 2 - j) % rows
        xs = xe if sh == 0 else pltpu.roll(xe, sh, axis=0)
        term = xs[HALO:HALO + tm] * cw_ref[j:j + 1, :]
        acc = term if acc is None else acc + term
    y = _silu(acc)
    for pc in range(2 * DN_QK // LANES):
        ys = y[:, pc * LANES:(pc + 1) * LANES]
        yn = ys * lax.rsqrt(_half_sums(ys * ys) + EPS)
        if pc < DN_QK // LANES:
            yn = yn * (DN_KEY_DIM ** -0.5)
        qkv_ref[0, :, pc * LANES:(pc + 1) * LANES] = yn.astype(BF16)
    qkv_ref[0, :, 2 * DN_QK:] = y[:, 2 * DN_QK:].astype(BF16)

    ba = ba_ref[0]
    nh2 = 2 * DN_HEADS
    beta = _sigmoid(ba[:, :nh2])
    z = ba[:, nh2:2 * nh2] + dtb_ref[...]
    sp = jnp.maximum(z, 0.0) + jnp.log1p(jnp.exp(-jnp.abs(z)))
    g = -jnp.exp(alog_ref[...]) * sp
    g_hi, g_lo = _split_hi_lo(g)
    gg = jnp.concatenate([g_hi, g_lo], axis=1)
    pre = _dot(tri_ref[0], gg)
    suf = _dot(tri_ref[1], gg)
    lane16 = lax.broadcasted_iota(jnp.int32, (tm, nh2), 1)
    gc = jnp.where(lane16 < DN_HEADS, pre[:, :nh2] + pre[:, nh2:], suf[:, :nh2] + suf[:, nh2:])
    b_hi, b_lo = _split_hi_lo(beta)
    c_hi, c_lo = _split_hi_lo(gc)
    c_lo2 = (gc - c_hi.astype(F32) - c_lo.astype(F32)).astype(BF16)
    ex = expand_ref[...]
    betax_ref[0] = _dot(b_hi, ex) + _dot(b_lo, ex)
    gcx_ref[0] = _dot(c_hi, ex) + _dot(c_lo, ex) + _dot(c_lo2, ex)
    gct = jnp.concatenate([gc, jnp.zeros((tm, LANES - nh2), F32)], axis=1).T
    for c in range(tm // DN_CHUNK):
        gct_ref[0, c] = gct[:nh2, c * DN_CHUNK:(c + 1) * DN_CHUNK]


def _dn_prep(dqkv, ba, conv_w, a_log, dt_bias):
    B, S, W = dqkv.shape
    tm = ROW_TILE
    nt = S // tm
    hb = tm // HALO
    r = np.arange(tm)
    same = (r[:, None] // DN_CHUNK) == (r[None, :] // DN_CHUNK)
    tri = np.stack([same & (r[:, None] >= r[None, :]), same & (r[:, None] <= r[None, :])]).astype(np.float32)
    expand = np.repeat(np.eye(2 * DN_HEADS, dtype=np.float32), HALF, axis=1)
    nh2 = 2 * DN_HEADS
    return pl.pallas_call(
        _dn_prep_kernel,
        grid=(B, nt),
        in_specs=[pl.BlockSpec((1, HALO, W), lambda b, i: (b, jnp.maximum(i * hb - 1, 0), 0)),
                  pl.BlockSpec((1, tm, W), lambda b, i: (b, i, 0)),
                  pl.BlockSpec((1, HALO, W), lambda b, i: (b, jnp.minimum((i + 1) * hb, S // HALO - 1), 0)),
                  pl.BlockSpec((1, tm, LANES), lambda b, i: (b, i, 0)),
                  pl.BlockSpec((DN_CONV, W), lambda b, i: (0, 0)),
                  pl.BlockSpec((1, nh2), lambda b, i: (0, 0)),
                  pl.BlockSpec((1, nh2), lambda b, i: (0, 0)),
                  pl.BlockSpec((2, tm, tm), lambda b, i: (0, 0, 0)),
                  pl.BlockSpec((nh2, nh2 * HALF), lambda b, i: (0, 0))],
        out_specs=[pl.BlockSpec((1, tm, W), lambda b, i: (b, i, 0)),
                   pl.BlockSpec((1, tm, nh2 * HALF), lambda b, i: (b, i, 0)),
                   pl.BlockSpec((1, tm, nh2 * HALF), lambda b, i: (b, i, 0)),
                   pl.BlockSpec((1, tm // DN_CHUNK, nh2, DN_CHUNK), lambda b, i: (b, i, 0, 0))],
        out_shape=[jax.ShapeDtypeStruct((B, S, W), BF16),
                   jax.ShapeDtypeStruct((B, S, nh2 * HALF), F32),
                   jax.ShapeDtypeStruct((B, S, nh2 * HALF), F32),
                   jax.ShapeDtypeStruct((B, S // DN_CHUNK, nh2, DN_CHUNK), F32)],
        compiler_params=_cparams(("parallel", "parallel")),
        name="dn_prep",
    )(dqkv, dqkv, dqkv, ba, conv_w.astype(F32), a_log.reshape(1, nh2).astype(F32),
      dt_bias.reshape(1, nh2).astype(F32), jnp.asarray(tri, BF16), jnp.asarray(expand, BF16))


def _block_diag(x2, low):
    zero = jnp.zeros_like(x2)
    return jnp.concatenate([jnp.where(low, x2, zero), jnp.where(low, zero, x2)], axis=0)


def _dn_pair_step(d, q2, k2, v2, beta2, gcol2, grow2, s2):
    C = DN_CHUNK
    low = _lane_is_low((C, LANES))
    r = lax.broadcasted_iota(jnp.int32, (C, LANES), 0)
    m = lax.broadcasted_iota(jnp.int32, (C, LANES), 1) % HALF
    incl = (r >= m) if d == 0 else (r <= m)
    strict = (r > m) if d == 0 else (r < m)
    diag = r == m
    decay = jnp.where(incl, jnp.exp(jnp.where(incl, gcol2 - grow2, 0.0)), 0.0)
    gtot2 = gcol2[C - 1:C] if d == 0 else gcol2[0:1]
    e_col = jnp.exp(gcol2)
    k2f = k2.astype(F32)
    kstack = _block_diag(k2, low)
    qkk = _dot_nt(jnp.concatenate([q2, k2], axis=0), kstack)
    qk = qkk[:C]
    kk = qkk[C:]
    p = jnp.where(strict, -(kk * beta2 * decay), 0.0)
    t = jnp.where(diag, 1.0, 0.0) + p
    for _ in range(5):
        p = _dot(p.astype(BF16), _block_diag(p.astype(BF16), low))
        t = t + _dot(t.astype(BF16), _block_diag(p.astype(BF16), low))
    vb = (v2.astype(F32) * beta2).astype(BF16)
    kbe = (k2f * beta2 * e_col).astype(BF16)
    rhs = jnp.concatenate([_block_diag(vb, low), _block_diag(kbe, low)], axis=1)
    sol = _dot(t.astype(BF16), rhs)
    u = sol[:, :LANES]
    w = sol[:, LANES:]
    q_in = q2.astype(F32) * e_col
    k_up = k2f * jnp.exp(gtot2 - gcol2)
    ws = _dot(jnp.concatenate([w, q_in], axis=0).astype(BF16), s2.astype(BF16))
    v_new = u - ws[:C]
    v_new_b = v_new.astype(BF16)
    o2 = ws[C:] + _dot((qk * decay).astype(BF16), _block_diag(v_new_b, low))
    rr = lax.broadcasted_iota(jnp.int32, (LANES, LANES), 0) < HALF
    cc = lax.broadcasted_iota(jnp.int32, (LANES, LANES), 1) < HALF
    upd = _dot_tn(k_up.astype(BF16), v_new_b)
    s2n = s2 * jnp.exp(gtot2) + jnp.where(rr == cc, upd, 0.0)
    return o2, s2n


def _dn_scan_kernel(qf_ref, qb_ref, bf_ref, bb_ref, gf_ref, gb_ref, tf_ref, tb_ref, of_ref, ob_ref, st_ref):
    @pl.when(pl.program_id(1) == 0)
    def _():
        st_ref[...] = jnp.zeros_like(st_ref)

    for d, (q_ref, b_ref, g_ref, t_ref, o_ref) in enumerate(
            ((qf_ref, bf_ref, gf_ref, tf_ref, of_ref), (qb_ref, bb_ref, gb_ref, tb_ref, ob_ref))):
        for pr in range(DN_HEADS // 2):
            sl = slice(pr * LANES, (pr + 1) * LANES)
            q2 = q_ref[0, :, pr * LANES:(pr + 1) * LANES]
            k2 = q_ref[0, :, DN_QK + pr * LANES:DN_QK + (pr + 1) * LANES]
            v2 = q_ref[0, :, 2 * DN_QK + pr * LANES:2 * DN_QK + (pr + 1) * LANES]
            grow2 = jnp.concatenate([t_ref[0, 0, 2 * pr:2 * pr + 1, :], t_ref[0, 0, 2 * pr + 1:2 * pr + 2, :]],
                                    axis=1)
            o2, s2n = _dn_pair_step(d, q2, k2, v2, b_ref[0, :, sl], g_ref[0, :, sl], grow2, st_ref[d, pr])
            st_ref[d, pr] = s2n
            o_ref[0, :, sl] = o2.astype(BF16)


def _dn_scan(qkvn, betax, gcx, gct):
    B, S, W = qkvn.shape
    C = DN_CHUNK
    nc = S // C
    hw = DN_HEADS * HALF
    fwd = lambda b, i: (b, i, 0)
    bwd = lambda b, i: (b, nc - 1 - i, 0)
    return pl.pallas_call(
        _dn_scan_kernel,
        grid=(B, nc),
        in_specs=[pl.BlockSpec((1, C, W), fwd),
                  pl.BlockSpec((1, C, W), bwd),
                  pl.BlockSpec((1, C, hw), lambda b, i: (b, i, 0)),
                  pl.BlockSpec((1, C, hw), lambda b, i: (b, nc - 1 - i, 1)),
                  pl.BlockSpec((1, C, hw), lambda b, i: (b, i, 0)),
                  pl.BlockSpec((1, C, hw), lambda b, i: (b, nc - 1 - i, 1)),
                  pl.BlockSpec((1, 1, DN_HEADS, C), lambda b, i: (b, i, 0, 0)),
                  pl.BlockSpec((1, 1, DN_HEADS, C), lambda b, i: (b, nc - 1 - i, 1, 0))],
        out_specs=[pl.BlockSpec((1, C, DN_V), fwd), pl.BlockSpec((1, C, DN_V), bwd)],
        out_shape=[jax.ShapeDtypeStruct((B, S, DN_V), BF16), jax.ShapeDtypeStruct((B, S, DN_V), BF16)],
        scratch_shapes=[pltpu.VMEM((2, DN_HEADS // 2, LANES, LANES), F32)],
        compiler_params=_cparams(("parallel", "arbitrary")),
        name="dn_scan",
    )(qkvn, qkvn, betax, betax, gcx, gcx, gct, gct)


def _mem_kv_kernel(m_ref, g_ref, w_ref, kg_ref, k_ref, v_ref):
    x = m_ref[...]
    ms = jnp.mean(x * x, axis=-1, keepdims=True)
    h = ((x * lax.rsqrt(ms + EPS)) * g_ref[...]).astype(BF16)
    kv = _dot(h, w_ref[...])
    for hd in range(MEM_HEADS):
        kh = kv[:, hd * LANES:(hd + 1) * LANES]
        kms = jnp.mean(kh * kh, axis=-1, keepdims=True)
        k_ref[:, hd * LANES:(hd + 1) * LANES] = ((kh * lax.rsqrt(kms + EPS)) * kg_ref[...]).astype(BF16)
    v_ref[...] = kv[:, MEM_Q:].astype(BF16)


def _mem_kv(mem2, g, w_kv, kg):
    n = mem2.shape[0]
    return pl.pallas_call(
        _mem_kv_kernel,
        grid=(n // N_MEM,),
        in_specs=[pl.BlockSpec((N_MEM, D_MODEL), lambda i: (i, 0)),
                  pl.BlockSpec((1, D_MODEL), lambda i: (0, 0)),
                  pl.BlockSpec((D_MODEL, 2 * MEM_Q), lambda i: (0, 0)),
                  pl.BlockSpec((1, MEM_HEAD_DIM), lambda i: (0, 0))],
        out_specs=[pl.BlockSpec((N_MEM, MEM_Q), lambda i: (i, 0)), pl.BlockSpec((N_MEM, MEM_Q), lambda i: (i, 0))],
        out_shape=[jax.ShapeDtypeStruct((n, MEM_Q), BF16), jax.ShapeDtypeStruct((n, MEM_Q), BF16)],
        compiler_params=_cparams(("parallel",)),
        name="mem_kv",
    )(mem2, g, w_kv, kg)


def _mem_attn_kernel(q_ref, k_ref, v_ref, qg_ref, o_ref):
    scale = MEM_HEAD_DIM ** -0.5
    for hd in range(MEM_HEADS):
        sl = slice(hd * LANES, (hd + 1) * LANES)
        q = q_ref[0, :, sl].astype(F32)
        qms = jnp.mean(q * q, axis=-1, keepdims=True)
        qn = ((q * lax.rsqrt(qms + EPS)) * qg_ref[...]).astype(BF16)
        s = _dot_nt(qn, k_ref[0, :, sl]) * scale
        m = jnp.max(s, axis=-1, keepdims=True)
        p = jnp.exp(s - m)
        den = jnp.sum(p, axis=-1, keepdims=True)
        o = _dot(p.astype(BF16), v_ref[0, :, sl]) * (1.0 / den)
        o_ref[0, :, sl] = o.astype(BF16)


def _mem_attn(mq, k, v, qg):
    B, S, _ = mq.shape
    tm = ROW_TILE
    return pl.pallas_call(
        _mem_attn_kernel,
        grid=(B, S // tm),
        in_specs=[pl.BlockSpec((1, tm, MEM_Q), lambda b, i: (b, i, 0)),
                  pl.BlockSpec((1, N_MEM, MEM_Q), lambda b, i: (b, 0, 0)),
                  pl.BlockSpec((1, N_MEM, MEM_Q), lambda b, i: (b, 0, 0)),
                  pl.BlockSpec((1, MEM_HEAD_DIM), lambda b, i: (0, 0))],
        out_specs=pl.BlockSpec((1, tm, MEM_Q), lambda b, i: (b, i, 0)),
        out_shape=jax.ShapeDtypeStruct((B, S, MEM_Q), BF16),
        compiler_params=_cparams(("parallel", "parallel")),
        name="mem_attn",
    )(mq, k, v, qg)


def _merge_kernel(x_ref, ya_ref, of_ref, ob_ref, z_ref, ym_ref, gate_ref, og_ref, wa_ref, wd_ref, wm_ref,
                  wo_ref, o_ref):
    o = of_ref[...].astype(F32) + ob_ref[...].astype(F32)
    z = z_ref[...].astype(F32)
    og = og_ref[...]
    parts = []
    for pc in range(DN_V // LANES):
        os_ = o[:, pc * LANES:(pc + 1) * LANES]
        on = os_ * lax.rsqrt(_half_sums(os_ * os_) * (1.0 / DN_VALUE_DIM) + EPS) * og
        parts.append((on * _silu(z[:, pc * LANES:(pc + 1) * LANES])).astype(BF16))
    y_dn = jnp.concatenate(parts, axis=1)
    g = gate_ref[...].astype(F32)
    merged = (g[:, :D_MODEL] * _dot(ya_ref[...], wa_ref[...])
              + g[:, D_MODEL:2 * D_MODEL] * _dot(y_dn, wd_ref[...])
              + g[:, 2 * D_MODEL:] * _dot(ym_ref[...], wm_ref[...]))
    o_ref[...] = x_ref[...] + _dot(merged.astype(BF16), wo_ref[...])


def _merge(x2, ya, of, ob, z, ym, gates, og, wa, wd, wm, wo):
    n = x2.shape[0]
    tm = ROW_TILE
    row = lambda w_: pl.BlockSpec((tm, w_), lambda i: (i, 0))
    full = lambda a, b: pl.BlockSpec((a, b), lambda i: (0, 0))
    return pl.pallas_call(
        _merge_kernel,
        grid=(n // tm,),
        in_specs=[row(D_MODEL), row(ATTN_Q), row(DN_V), row(DN_V), row(DN_V), row(MEM_Q), row(N_BRANCH * D_MODEL),
                  full(1, LANES), full(ATTN_Q, D_MODEL), full(DN_V, D_MODEL), full(MEM_Q, D_MODEL),
                  full(D_MODEL, D_MODEL)],
        out_specs=row(D_MODEL),
        out_shape=jax.ShapeDtypeStruct((n, D_MODEL), F32),
        compiler_params=_cparams(("parallel",)),
        name="merge",
    )(x2, ya, of, ob, z, ym, gates, og, wa, wd, wm, wo)


def _ffn_kernel(xp_ref, xc_ref, xn_ref, g_ref, wu_ref, cw_ref, cb_ref, wd_ref, o_ref, acc_ref):
    i = pl.program_id(1)
    nt = pl.num_programs(1)
    tm = xc_ref.shape[1]
    xc = xc_ref[0]
    prev = jnp.where(i == 0, 0.0, xp_ref[0])
    nxt = jnp.where(i == nt - 1, 0.0, xn_ref[0])
    xe = jnp.concatenate([prev, xc, nxt], axis=0)
    ms = jnp.mean(xe * xe, axis=-1, keepdims=True)
    h = ((xe * lax.rsqrt(ms + EPS)) * g_ref[...]).astype(BF16)
    rows = tm + 2 * HALO
    acc_ref[...] = xc

    def conv(u, c0):
        out = None
        for j in range(FFN_CONV):
            sh = (FFN_CONV // 2 - j) % rows
            us = u if sh == 0 else pltpu.roll(u, sh, axis=0)
            term = us[HALO:HALO + tm] * cw_ref[j:j + 1, c0:c0 + FF_CHUNK]
            out = term if out is None else out + term
        return out + cb_ref[:, c0:c0 + FF_CHUNK]

    for c in range(D_FF // FF_CHUNK):
        c0 = c * FF_CHUNK
        ug = conv(_dot(h, wu_ref[:, c0:c0 + FF_CHUNK]), c0)
        uv = conv(_dot(h, wu_ref[:, D_FF + c0:D_FF + c0 + FF_CHUNK]), D_FF + c0)
        act = (_silu(ug) * uv).astype(BF16)
        acc_ref[...] += _dot(act, wd_ref[c0:c0 + FF_CHUNK, :])
    o_ref[0] = acc_ref[...]


def _ffn(x1, g, wu, cw, cb, wd):
    B, S, _ = x1.shape
    tm = ROW_TILE
    nt = S // tm
    hb = tm // HALO
    return pl.pallas_call(
        _ffn_kernel,
        grid=(B, nt),
        in_specs=[pl.BlockSpec((1, HALO, D_MODEL), lambda b, i: (b, jnp.maximum(i * hb - 1, 0), 0)),
                  pl.BlockSpec((1, tm, D_MODEL), lambda b, i: (b, i, 0)),
                  pl.BlockSpec((1, HALO, D_MODEL), lambda b, i: (b, jnp.minimum((i + 1) * hb, S // HALO - 1), 0)),
                  pl.BlockSpec((1, D_MODEL), lambda b, i: (0, 0)),
                  pl.BlockSpec((D_MODEL, 2 * D_FF), lambda b, i: (0, 0)),
                  pl.BlockSpec((FFN_CONV, 2 * D_FF), lambda b, i: (0, 0)),
                  pl.BlockSpec((1, 2 * D_FF), lambda b, i: (0, 0)),
                  pl.BlockSpec((D_FF, D_MODEL), lambda b, i: (0, 0))],
        out_specs=pl.BlockSpec((1, tm, D_MODEL), lambda b, i: (b, i, 0)),
        out_shape=jax.ShapeDtypeStruct((B, S, D_MODEL), F32),
        scratch_shapes=[pltpu.VMEM((tm, D_MODEL), F32)],
        compiler_params=_cparams(("parallel", "parallel")),
        name="ffn",
    )(x1, x1, x1, g, wu, cw, cb, wd)


def _permute_w_in(w):
    idx = np.cumsum((0,) + IN_SPLITS)
    seg = lambda k: w[:, idx[k]:idx[k + 1]]
    aq, ak, av, dq, dk, dv, dz, db, da, mq, gates = (seg(k) for k in range(11))
    pad = jnp.zeros((w.shape[0], LANES - 4 * DN_HEADS), w.dtype)
    return jnp.concatenate([aq, ak, av, dq, dk, dv, dz, mq, gates, db, da, pad], axis=1).astype(BF16)


def _layer(x, mem, rel_bias_table, p):
    B, S, D = x.shape
    n = B * S
    x2 = x.reshape(n, D)
    row = lambda a: a.reshape(1, -1).astype(F32)
    aq, akv, dqkv, dz, mq, gates, ba = _inproj(x2, row(p["norm_mix_g"]), _permute_w_in(p["w_in"]))
    y_attn = _attn(aq.reshape(B, S, -1), akv.reshape(B, S, -1), rel_bias_table, p["attn_sink"],
                   p["attn_q_norm_g"], p["attn_k_norm_g"])
    qkvn, betax, gcx, gct = _dn_prep(dqkv.reshape(B, S, -1), ba.reshape(B, S, -1), p["dn_conv_w"],
                                     p["dn_a_log"], p["dn_dt_bias"])
    o_f, o_b = _dn_scan(qkvn, betax, gcx, gct)
    mk, mv = _mem_kv(mem.reshape(B * N_MEM, D), row(p["mem_norm_g"]), p["mem_w_kv"].astype(BF16),
                     row(p["mem_k_norm_g"]))
    y_mem = _mem_attn(mq.reshape(B, S, -1), mk.reshape(B, N_MEM, -1), mv.reshape(B, N_MEM, -1),
                      row(p["mem_q_norm_g"]))
    og2 = jnp.tile(p["dn_out_norm_g"].astype(F32), 2)[None]
    x1 = _merge(x2, y_attn.reshape(n, -1), o_f.reshape(n, -1), o_b.reshape(n, -1), dz, y_mem.reshape(n, -1), gates,
                og2, p["w_br_attn"].astype(BF16), p["w_br_dn"].astype(BF16), p["w_br_mem"].astype(BF16),
                p["w_out"].astype(BF16))
    return _ffn(x1.reshape(B, S, D), row(p["norm_ffn_g"]), p["ffn_w_up"].astype(BF16), p["ffn_conv_w"].astype(F32),
                row(p["ffn_conv_b"]), p["ffn_w_down"].astype(BF16))


_LAYER_PARAMS = ("norm_mix_g", "w_in", "attn_q_norm_g", "attn_k_norm_g", "attn_sink", "dn_conv_w", "dn_a_log",
                 "dn_dt_bias", "dn_out_norm_g", "mem_norm_g", "mem_w_kv", "mem_q_norm_g", "mem_k_norm_g",
                 "w_br_attn", "w_br_dn", "w_br_mem", "w_out", "norm_ffn_g", "ffn_w_up", "ffn_conv_w", "ffn_conv_b",
                 "ffn_w_down")


def kernel(x, mem, rel_bias_table, norm_mix_g, w_in, attn_q_norm_g, attn_k_norm_g, attn_sink, dn_conv_w, dn_a_log,
           dn_dt_bias, dn_out_norm_g, mem_norm_g, mem_w_kv, mem_q_norm_g, mem_k_norm_g, w_br_attn, w_br_dn,
           w_br_mem, w_out, norm_ffn_g, ffn_w_up, ffn_conv_w, ffn_conv_b, ffn_w_down):
    stacked = dict(zip(_LAYER_PARAMS, (norm_mix_g, w_in, attn_q_norm_g, attn_k_norm_g, attn_sink, dn_conv_w,
                                       dn_a_log, dn_dt_bias, dn_out_norm_g, mem_norm_g, mem_w_kv, mem_q_norm_g,
                                       mem_k_norm_g, w_br_attn, w_br_dn, w_br_mem, w_out, norm_ffn_g, ffn_w_up,
                                       ffn_conv_w, ffn_conv_b, ffn_w_down)))
    depth = w_in.shape[0]
    for l in range(depth):
        x = _layer(x, mem, rel_bias_table, {k: v[l] for k, v in stacked.items()})
    return x
```

```python
import functools
import math

import numpy as np
import jax
import jax.numpy as jnp
from jax import lax
from jax.experimental import pallas as pl
from jax.experimental.pallas import tpu as pltpu

F32 = jnp.float32
BF16 = jnp.bfloat16

EPS = 1e-6
D_MODEL = 1024
N_MEM = 256
ATTN_HEADS = 8
ATTN_KV_HEADS = 2
ATTN_HEAD_DIM = 64
WINDOW = 128
ATTN_BLOCK = 128
REL_BUCKETS = 32
REL_MAX_DIST = 128
DN_HEADS = 8
DN_KEY_DIM = 64
DN_VALUE_DIM = 64
DN_CONV = 5
DN_CHUNK = 64
MEM_HEADS = 4
MEM_HEAD_DIM = 128
D_FF = 2816
FFN_CONV = 3
N_BRANCH = 3

ATTN_Q = ATTN_HEADS * ATTN_HEAD_DIM
ATTN_KV = ATTN_KV_HEADS * ATTN_HEAD_DIM
DN_QK = DN_HEADS * DN_KEY_DIM
DN_V = DN_HEADS * DN_VALUE_DIM
MEM_Q = MEM_HEADS * MEM_HEAD_DIM
IN_SPLITS = (ATTN_Q, ATTN_KV, ATTN_KV, DN_QK, DN_QK, DN_V, DN_V, 2 * DN_HEADS, 2 * DN_HEADS, MEM_Q,
             N_BRANCH * D_MODEL)

LANES = 128
HALF = 64
HALO = 8
NEG = -1e30
VMEM_LIMIT = 56 * 1024 * 1024

ROW_TILE = 512
FF_CHUNK = 256


def _cparams(sem):
    return pltpu.CompilerParams(dimension_semantics=sem, vmem_limit_bytes=VMEM_LIMIT)


def _dot(a, b):
    return jnp.dot(a, b, preferred_element_type=F32)


def _dot_nt(a, b):
    return lax.dot_general(a, b, (((1,), (1,)), ((), ())), preferred_element_type=F32)


def _dot_tn(a, b):
    return lax.dot_general(a, b, (((0,), (0,)), ((), ())), preferred_element_type=F32)


def _lane_is_low(shape):
    lane = lax.broadcasted_iota(jnp.int32, shape, len(shape) - 1)
    return (lane % LANES) < HALF


def _half_sums(sq):
    low = _lane_is_low(sq.shape)
    s_lo = jnp.sum(jnp.where(low, sq, 0.0), axis=-1, keepdims=True)
    s_hi = jnp.sum(jnp.where(low, 0.0, sq), axis=-1, keepdims=True)
    return jnp.where(low, s_lo, s_hi)


def _silu(x):
    return x * (1.0 / (1.0 + jnp.exp(-x)))


def _sigmoid(x):
    return 1.0 / (1.0 + jnp.exp(-x))


_C_AQ = (0, 512)
_C_AKV = (512, 768)
_C_DQKV = (768, 2304)
_C_DZ = (2304, 2816)
_C_MQ = (2816, 3328)
_C_GATE = (3328, 6400)
_C_BA = (6400, 6528)
_N_IN = 6528


def _inproj_kernel(x_ref, g_ref, w_ref, aq_ref, akv_ref, dqkv_ref, dz_ref, mq_ref, gate_ref, ba_ref):
    x = x_ref[...]
    ms = jnp.mean(x * x, axis=-1, keepdims=True)
    h = ((x * lax.rsqrt(ms + EPS)) * g_ref[...]).astype(BF16)

    def proj(c):
        return _dot(h, w_ref[:, c[0]:c[1]])

    aq_ref[...] = proj(_C_AQ).astype(BF16)
    akv_ref[...] = proj(_C_AKV).astype(BF16)
    dqkv_ref[...] = proj(_C_DQKV).astype(BF16)
    dz_ref[...] = proj(_C_DZ).astype(BF16)
    mq_ref[...] = proj(_C_MQ).astype(BF16)
    gate_ref[...] = _sigmoid(proj(_C_GATE)).astype(BF16)
    ba_ref[...] = proj(_C_BA)


def _inproj(x2, g, w):
    n = x2.shape[0]
    tm = ROW_TILE
    widths = [(512, BF16), (256, BF16), (1536, BF16), (512, BF16), (512, BF16), (3072, BF16), (128, F32)]
    return pl.pallas_call(
        _inproj_kernel,
        grid=(n // tm,),
        in_specs=[pl.BlockSpec((tm, D_MODEL), lambda i: (i, 0)),
                  pl.BlockSpec((1, D_MODEL), lambda i: (0, 0)),
                  pl.BlockSpec((D_MODEL, _N_IN), lambda i: (0, 0))],
        out_specs=[pl.BlockSpec((tm, w_), lambda i: (i, 0)) for w_, _ in widths],
        out_shape=[jax.ShapeDtypeStruct((n, w_), dt) for w_, dt in widths],
        compiler_params=_cparams(("parallel",)),
        name="inproj",
    )(x2, g, w)


def _t5_buckets(rel):
    nb = REL_BUCKETS // 2
    max_exact = nb // 2
    ret = (rel > 0).astype(np.int32) * nb
    n = np.abs(rel)
    large = max_exact + (np.log(np.maximum(n, 1) / max_exact) / np.log(REL_MAX_DIST / max_exact)
                         * (nb - max_exact)).astype(np.int32)
    large = np.minimum(large, nb - 1)
    return (ret + np.where(n < max_exact, n, large)).astype(np.int32)


_ATTN_GROUPS = ((0, 1, True, False), (0, 1, False, True), (2, 3, True, True), (2, 3, False, False))
_ATTN_GROUP_HEADS = ((0, 2), (1, 3), (4, 6), (5, 7))


def _attn_kernel(q_ref, kp_ref, kc_ref, kn_ref, bias_ref, sink_ref, qg_ref, kg_ref, o_ref):
    n = pl.program_id(1)
    nb = pl.num_programs(1)
    T = ATTN_BLOCK
    q = q_ref[0].astype(F32)
    kv = jnp.concatenate([kp_ref[0], kc_ref[0], kn_ref[0]], axis=0).astype(F32)
    k = kv[:, :LANES]
    v = kv[:, LANES:]
    qg = qg_ref[...]
    kg = kg_ref[...]
    k = k * lax.rsqrt(_half_sums(k * k) * (1.0 / ATTN_HEAD_DIM) + EPS) * kg
    k_b = k.astype(BF16)
    k_sw = pltpu.roll(k, HALF, axis=1).astype(BF16)
    v_b = v.astype(BF16)
    v_sw = pltpu.roll(v, HALF, axis=1).astype(BF16)
    low_q = _lane_is_low((T, LANES))
    qn = []
    for pc in range(4):
        qs = q[:, pc * LANES:(pc + 1) * LANES]
        qn.append(qs * lax.rsqrt(_half_sums(qs * qs) * (1.0 / ATTN_HEAD_DIM) + EPS) * qg)
    col = lax.broadcasted_iota(jnp.int32, (2 * T, 3 * T), 1)
    edge = ((col < T) & (n == 0)) | ((col >= 2 * T) & (n == nb - 1))
    scale = ATTN_HEAD_DIM ** -0.5
    res = []
    for gi, (pa, pb, low, swapped) in enumerate(_ATTN_GROUPS):
        sel = low_q if low else jnp.logical_not(low_q)
        lhs = jnp.concatenate([jnp.where(sel, qn[pa], 0.0), jnp.where(sel, qn[pb], 0.0)], axis=0).astype(BF16)
        s = _dot_nt(lhs, k_sw if swapped else k_b) * scale + bias_ref[gi]
        s = jnp.where(edge, NEG, s)
        sk = sink_ref[gi]
        m = jnp.maximum(jnp.max(s, axis=-1, keepdims=True), sk)
        p = jnp.exp(s - m)
        den = jnp.sum(p, axis=-1, keepdims=True) + jnp.exp(sk - m)
        r = _dot(p.astype(BF16), v_sw if swapped else v_b)
        res.append(r * (1.0 / den))
    for pc, (ge, go) in enumerate(((0, 1), (0, 1), (2, 3), (2, 3))):
        r0 = (pc % 2) * T
        out = jnp.where(low_q, res[ge][r0:r0 + T], res[go][r0:r0 + T])
        o_ref[0, :, pc * LANES:(pc + 1) * LANES] = out.astype(BF16)


def _attn(aq, akv, rel_table, sink, qg, kg):
    B, S, _ = aq.shape
    T = ATTN_BLOCK
    nb = S // T
    t_idx = np.arange(T)[:, None]
    j_idx = np.arange(3 * T)[None, :]
    rel = j_idx - T - t_idx
    in_win = jnp.asarray(np.abs(rel) <= WINDOW)
    onehot = jnp.asarray(np.eye(REL_BUCKETS, dtype=np.float32)[_t5_buckets(rel)])
    bias = jnp.einsum("tjr,rh->htj", onehot, rel_table.astype(F32), precision=lax.Precision.HIGHEST)
    bias = jnp.where(in_win[None], bias, NEG)
    bias_g = jnp.stack([jnp.concatenate([bias[a], bias[b]], axis=0) for a, b in _ATTN_GROUP_HEADS])
    sk = sink.astype(F32)
    sink_g = jnp.stack([jnp.concatenate([jnp.full((T, 1), 1.0) * sk[a], jnp.full((T, 1), 1.0) * sk[b]], axis=0)
                        for a, b in _ATTN_GROUP_HEADS])
    qg2 = jnp.tile(qg.astype(F32), 2)[None]
    kg2 = jnp.tile(kg.astype(F32), 2)[None]
    return pl.pallas_call(
        _attn_kernel,
        grid=(B, nb),
        in_specs=[pl.BlockSpec((1, T, ATTN_Q), lambda b, i: (b, i, 0)),
                  pl.BlockSpec((1, T, 2 * ATTN_KV), lambda b, i: (b, jnp.maximum(i - 1, 0), 0)),
                  pl.BlockSpec((1, T, 2 * ATTN_KV), lambda b, i: (b, i, 0)),
                  pl.BlockSpec((1, T, 2 * ATTN_KV), lambda b, i: (b, jnp.minimum(i + 1, nb - 1), 0)),
                  pl.BlockSpec((4, 2 * T, 3 * T), lambda b, i: (0, 0, 0)),
                  pl.BlockSpec((4, 2 * T, 1), lambda b, i: (0, 0, 0)),
                  pl.BlockSpec((1, LANES), lambda b, i: (0, 0)),
                  pl.BlockSpec((1, LANES), lambda b, i: (0, 0))],
        out_specs=pl.BlockSpec((1, T, ATTN_Q), lambda b, i: (b, i, 0)),
        out_shape=jax.ShapeDtypeStruct((B, S, ATTN_Q), BF16),
        compiler_params=_cparams(("parallel", "parallel")),
        name="attn",
    )(aq, akv, akv, akv, bias_g, sink_g, qg2, kg2)


def _split_hi_lo(x):
    hi = x.astype(BF16)
    lo = (x - hi.astype(F32)).astype(BF16)
    return hi, lo


def _dn_prep_kernel(xp_ref, xc_ref, xn_ref, ba_ref, cw_ref, alog_ref, dtb_ref, tri_ref, expand_ref,
                    qkv_ref, betax_ref, gcx_ref, gct_ref):
    i = pl.program_id(1)
    nt = pl.num_programs(1)
    tm = xc_ref.shape[1]
    prev = jnp.where(i == 0, 0.0, xp_ref[0].astype(F32))
    nxt = jnp.where(i == nt - 1, 0.0, xn_ref[0].astype(F32))
    xe = jnp.concatenate([prev, xc_ref[0].astype(F32), nxt], axis=0)
    rows = tm + 2 * HALO
    acc = None
    for j in range(DN_CONV):
        sh = (DN_CONV // 2 - j) % rows
        xs = xe if sh == 0 else pltpu.roll(xe, sh, axis=0)
        term = xs[HALO:HALO + tm] * cw_ref[j:j + 1, :]
        acc = term if acc is None else acc + term
    y = _silu(acc)
    for pc in range(2 * DN_QK // LANES):
        ys = y[:, pc * LANES:(pc + 1) * LANES]
        yn = ys * lax.rsqrt(_half_sums(ys * ys) + EPS)
        if pc < DN_QK // LANES:
            yn = yn * (DN_KEY_DIM ** -0.5)
        qkv_ref[0, :, pc * LANES:(pc + 1) * LANES] = yn.astype(BF16)
    qkv_ref[0, :, 2 * DN_QK:] = y[:, 2 * DN_QK:].astype(BF16)

    ba = ba_ref[0]
    nh2 = 2 * DN_HEADS
    beta = _sigmoid(ba[:, :nh2])
    z = ba[:, nh2:2 * nh2] + dtb_ref[...]
    sp = jnp.maximum(z, 0.0) + jnp.log1p(jnp.exp(-jnp.abs(z)))
    g = -jnp.exp(alog_ref[...]) * sp
    g_hi, g_lo = _split_hi_lo(g)
    gg = jnp.concatenate([g_hi, g_lo], axis=1)
    pre = _dot(tri_ref[0], gg)
    suf = _dot(tri_ref[1], gg)
    lane16 = lax.broadcasted_iota(jnp.int32, (tm, nh2), 1)
    gc = jnp.where(lane16 < DN_HEADS, pre[:, :nh2] + pre[:, nh2:], suf[:, :nh2] + suf[:, nh2:])
    b_hi, b_lo = _split_hi_lo(beta)
    c_hi, c_lo = _split_hi_lo(gc)
    c_lo2 = (gc - c_hi.astype(F32) - c_lo.astype(F32)).astype(BF16)
    ex = expand_ref[...]
    betax_ref[0] = _dot(b_hi, ex) + _dot(b_lo, ex)
    gcx_ref[0] = _dot(c_hi, ex) + _dot(c_lo, ex) + _dot(c_lo2, ex)
    gct = jnp.concatenate([gc, jnp.zeros((tm, LANES - nh2), F32)], axis=1).T
    for c in range(tm // DN_CHUNK):
        gct_ref[0, c] = gct[:nh2, c * DN_CHUNK:(c + 1) * DN_CHUNK]


def _dn_prep(dqkv, ba, conv_w, a_log, dt_bias):
    B, S, W = dqkv.shape
    tm = ROW_TILE
    nt = S // tm
    hb = tm // HALO
    r = np.arange(tm)
    same = (r[:, None] // DN_CHUNK) == (r[None, :] // DN_CHUNK)
    tri = np.stack([same & (r[:, None] >= r[None, :]), same & (r[:, None] <= r[None, :])]).astype(np.float32)
    expand = np.repeat(np.eye(2 * DN_HEADS, dtype=np.float32), HALF, axis=1)
    nh2 = 2 * DN_HEADS
    return pl.pallas_call(
        _dn_prep_kernel,
        grid=(B, nt),
        in_specs=[pl.BlockSpec((1, HALO, W), lambda b, i: (b, jnp.maximum(i * hb - 1, 0), 0)),
                  pl.BlockSpec((1, tm, W), lambda b, i: (b, i, 0)),
                  pl.BlockSpec((1, HALO, W), lambda b, i: (b, jnp.minimum((i + 1) * hb, S // HALO - 1), 0)),
                  pl.BlockSpec((1, tm, LANES), lambda b, i: (b, i, 0)),
                  pl.BlockSpec((DN_CONV, W), lambda b, i: (0, 0)),
                  pl.BlockSpec((1, nh2), lambda b, i: (0, 0)),
                  pl.BlockSpec((1, nh2), lambda b, i: (0, 0)),
                  pl.BlockSpec((2, tm, tm), lambda b, i: (0, 0, 0)),
                  pl.BlockSpec((nh2, nh2 * HALF), lambda b, i: (0, 0))],
        out_specs=[pl.BlockSpec((1, tm, W), lambda b, i: (b, i, 0)),
                   pl.BlockSpec((1, tm, nh2 * HALF), lambda b, i: (b, i, 0)),
                   pl.BlockSpec((1, tm, nh2 * HALF), lambda b, i: (b, i, 0)),
                   pl.BlockSpec((1, tm // DN_CHUNK, nh2, DN_CHUNK), lambda b, i: (b, i, 0, 0))],
        out_shape=[jax.ShapeDtypeStruct((B, S, W), BF16),
                   jax.ShapeDtypeStruct((B, S, nh2 * HALF), F32),
                   jax.ShapeDtypeStruct((B, S, nh2 * HALF), F32),
                   jax.ShapeDtypeStruct((B, S // DN_CHUNK, nh2, DN_CHUNK), F32)],
        compiler_params=_cparams(("parallel", "parallel")),
        name="dn_prep",
    )(dqkv, dqkv, dqkv, ba, conv_w.astype(F32), a_log.reshape(1, nh2).astype(F32),
      dt_bias.reshape(1, nh2).astype(F32), jnp.asarray(tri, BF16), jnp.asarray(expand, BF16))


def _block_diag(x2, low):
    zero = jnp.zeros_like(x2)
    return jnp.concatenate([jnp.where(low, x2, zero), jnp.where(low, zero, x2)], axis=0)


DN_PAIRS = DN_HEADS // 2
DN_CHUNKS_PER_STEP = 2
DN_SCAN_CHUNKS = 4


def _dn_chunk_kernel(qkv_ref, bx_ref, gx_ref, gt_ref, u_ref, w_ref, qkd_ref, qin_ref, kupt_ref, egt_ref):
    C = DN_CHUNK
    low = _lane_is_low((C, LANES))
    r = lax.broadcasted_iota(jnp.int32, (C, LANES), 0)
    m = lax.broadcasted_iota(jnp.int32, (C, LANES), 1) % HALF
    eye = jnp.where(r == m, 1.0, 0.0)
    probs = []
    for ch in range(DN_CHUNKS_PER_STEP):
        rows = slice(ch * C, (ch + 1) * C)
        for pr in range(DN_PAIRS):
            q2 = qkv_ref[0, rows, pr * LANES:(pr + 1) * LANES]
            k2 = qkv_ref[0, rows, DN_QK + pr * LANES:DN_QK + (pr + 1) * LANES]
            v2 = qkv_ref[0, rows, 2 * DN_QK + pr * LANES:2 * DN_QK + (pr + 1) * LANES]
            q2f, k2f, v2f = q2.astype(F32), k2.astype(F32), v2.astype(F32)
            qkk = _dot_nt(jnp.concatenate([q2, k2], axis=0), _block_diag(k2, low))
            qk, kk = qkk[:C], qkk[C:]
            for d in range(2):
                col = slice(d * DN_QK + pr * LANES, d * DN_QK + (pr + 1) * LANES)
                beta2 = bx_ref[0, rows, col]
                gcol2 = gx_ref[0, rows, col]
                hrow = d * DN_HEADS + 2 * pr
                grow2 = jnp.concatenate([gt_ref[0, ch, hrow:hrow + 1, :], gt_ref[0, ch, hrow + 1:hrow + 2, :]],
                                        axis=1)
                incl = (r >= m) if d == 0 else (r <= m)
                strict = (r > m) if d == 0 else (r < m)
                decay = jnp.where(incl, jnp.exp(jnp.where(incl, gcol2 - grow2, 0.0)), 0.0)
                gtot2 = gcol2[C - 1:C] if d == 0 else gcol2[0:1]
                e_col = jnp.exp(gcol2)
                p = jnp.where(strict, -(kk * beta2 * decay), 0.0)
                vb = (v2f * beta2).astype(BF16)
                kbe = (k2f * beta2 * e_col).astype(BF16)
                rhs = jnp.concatenate([_block_diag(vb, low), _block_diag(kbe, low)], axis=1)
                qkd_ref[0, rows, col] = (qk * decay).astype(BF16)
                qin_ref[0, rows, col] = (q2f * e_col).astype(BF16)
                kupt_ref[0, ch, d, pr] = (k2f * jnp.exp(gtot2 - gcol2)).T.astype(BF16)
                egt_ref[0, ch, d, pr:pr + 1, :] = jnp.exp(gtot2)
                probs.append([p, eye + p, rhs, rows, col])
    for pb in probs:
        pb[0] = _dot(pb[0].astype(BF16), _block_diag(pb[0].astype(BF16), low))
    for _ in range(4):
        for pb in probs:
            both = _dot(jnp.concatenate([pb[1], pb[0]], axis=0).astype(BF16),
                        _block_diag(pb[0].astype(BF16), low))
            pb[1] = pb[1] + both[:C]
            pb[0] = both[C:]
    for p, t, rhs, rows, col in probs:
        t = t + _dot(t.astype(BF16), _block_diag(p.astype(BF16), low))
        sol = _dot(t.astype(BF16), rhs)
        u_ref[0, rows, col] = sol[:, :LANES].astype(BF16)
        w_ref[0, rows, col] = sol[:, LANES:].astype(BF16)


def _dn_chunk(qkvn, betax, gcx, gct):
    B, S, W = qkvn.shape
    C = DN_CHUNK
    cps = DN_CHUNKS_PER_STEP
    nc = S // C
    wide = 2 * DN_QK
    blk = lambda w_: pl.BlockSpec((1, cps * C, w_), lambda b, i: (b, i, 0))
    return pl.pallas_call(
        _dn_chunk_kernel,
        grid=(B, nc // cps),
        in_specs=[blk(W), blk(wide), blk(wide),
                  pl.BlockSpec((1, cps, 2 * DN_HEADS, C), lambda b, i: (b, i, 0, 0))],
        out_specs=[blk(wide), blk(wide), blk(wide), blk(wide),
                   pl.BlockSpec((1, cps, 2, DN_PAIRS, LANES, C), lambda b, i: (b, i, 0, 0, 0, 0)),
                   pl.BlockSpec((1, cps, 2, DN_PAIRS, LANES), lambda b, i: (b, i, 0, 0, 0))],
        out_shape=[jax.ShapeDtypeStruct((B, S, wide), BF16)] * 4
        + [jax.ShapeDtypeStruct((B, nc, 2, DN_PAIRS, LANES, C), BF16),
           jax.ShapeDtypeStruct((B, nc, 2, DN_PAIRS, LANES), F32)],
        compiler_params=_cparams(("parallel", "parallel")),
        name="dn_chunk",
    )(qkvn, betax, gcx, gct)


def _dn_scan_kernel(uf, wf, qkdf, qinf, kuptf, egtf, ub, wb, qkdb, qinb, kuptb, egtb, of_ref, ob_ref, st_ref):
    @pl.when(pl.program_id(1) == 0)
    def _():
        st_ref[...] = jnp.zeros_like(st_ref)

    C = DN_CHUNK
    low = _lane_is_low((C, LANES))
    rr = lax.broadcasted_iota(jnp.int32, (LANES, LANES), 0) < HALF
    cc = lax.broadcasted_iota(jnp.int32, (LANES, LANES), 1) < HALF
    on_diag = rr == cc
    dirs = ((uf, wf, qkdf, qinf, kuptf, egtf, of_ref), (ub, wb, qkdb, qinb, kuptb, egtb, ob_ref))
    prob_ids = [(d, pr) for d in range(2) for pr in range(DN_PAIRS)]
    state = {dp: st_ref[dp[0], dp[1]] for dp in prob_ids}
    for step in range(DN_SCAN_CHUNKS):
        ws, vb = {}, {}
        for d, pr in prob_ids:
            u_ref, w_ref, qkd_ref, qin_ref, kupt_ref, egt_ref, o_ref = dirs[d]
            ch = step if d == 0 else DN_SCAN_CHUNKS - 1 - step
            rows = slice(ch * C, (ch + 1) * C)
            sl = slice(pr * LANES, (pr + 1) * LANES)
            lhs = jnp.concatenate([w_ref[0, rows, sl], qin_ref[0, rows, sl]], axis=0)
            ws[d, pr] = _dot(lhs, state[d, pr].astype(BF16))
        for d, pr in prob_ids:
            u_ref = dirs[d][0]
            ch = step if d == 0 else DN_SCAN_CHUNKS - 1 - step
            rows = slice(ch * C, (ch + 1) * C)
            sl = slice(pr * LANES, (pr + 1) * LANES)
            vb[d, pr] = (u_ref[0, rows, sl].astype(F32) - ws[d, pr][:C]).astype(BF16)
        for d, pr in prob_ids:
            u_ref, w_ref, qkd_ref, qin_ref, kupt_ref, egt_ref, o_ref = dirs[d]
            ch = step if d == 0 else DN_SCAN_CHUNKS - 1 - step
            rows = slice(ch * C, (ch + 1) * C)
            sl = slice(pr * LANES, (pr + 1) * LANES)
            upd = _dot(kupt_ref[0, ch, 0, pr], vb[d, pr])
            state[d, pr] = state[d, pr] * egt_ref[0, ch, 0, pr:pr + 1, :] + jnp.where(on_diag, upd, 0.0)
            o2 = ws[d, pr][C:] + _dot(qkd_ref[0, rows, sl], _block_diag(vb[d, pr], low))
            o_ref[0, rows, sl] = o2.astype(BF16)
    for d, pr in prob_ids:
        st_ref[d, pr] = state[d, pr]


def _dn_scan(u, w, qkd, qin, kupt, egt):
    B, S, _ = u.shape
    C = DN_CHUNK
    sc = DN_SCAN_CHUNKS
    nb = S // (C * sc)
    fwd3 = lambda b, i: (b, i, 0)
    bwd3 = lambda b, i: (b, nb - 1 - i, 1)
    row_f = pl.BlockSpec((1, sc * C, DN_QK), fwd3)
    row_b = pl.BlockSpec((1, sc * C, DN_QK), bwd3)
    kup_f = pl.BlockSpec((1, sc, 1, DN_PAIRS, LANES, C), lambda b, i: (b, i, 0, 0, 0, 0))
    kup_b = pl.BlockSpec((1, sc, 1, DN_PAIRS, LANES, C), lambda b, i: (b, nb - 1 - i, 1, 0, 0, 0))
    egt_f = pl.BlockSpec((1, sc, 1, DN_PAIRS, LANES), lambda b, i: (b, i, 0, 0, 0))
    egt_b = pl.BlockSpec((1, sc, 1, DN_PAIRS, LANES), lambda b, i: (b, nb - 1 - i, 1, 0, 0))
    return pl.pallas_call(
        _dn_scan_kernel,
        grid=(B, nb),
        in_specs=[row_f, row_f, row_f, row_f, kup_f, egt_f, row_b, row_b, row_b, row_b, kup_b, egt_b],
        out_specs=[pl.BlockSpec((1, sc * C, DN_V), fwd3),
                   pl.BlockSpec((1, sc * C, DN_V), lambda b, i: (b, nb - 1 - i, 0))],
        out_shape=[jax.ShapeDtypeStruct((B, S, DN_V), BF16), jax.ShapeDtypeStruct((B, S, DN_V), BF16)],
        scratch_shapes=[pltpu.VMEM((2, DN_PAIRS, LANES, LANES), F32)],
        compiler_params=_cparams(("parallel", "arbitrary")),
        name="dn_scan",
    )(u, w, qkd, qin, kupt, egt, u, w, qkd, qin, kupt, egt)


def _mem_kv_kernel(m_ref, g_ref, w_ref, kg_ref, k_ref, v_ref):
    x = m_ref[...]
    ms = jnp.mean(x * x, axis=-1, keepdims=True)
    h = ((x * lax.rsqrt(ms + EPS)) * g_ref[...]).astype(BF16)
    kv = _dot(h, w_ref[...])
    for hd in range(MEM_HEADS):
        kh = kv[:, hd * LANES:(hd + 1) * LANES]
        kms = jnp.mean(kh * kh, axis=-1, keepdims=True)
        k_ref[:, hd * LANES:(hd + 1) * LANES] = ((kh * lax.rsqrt(kms + EPS)) * kg_ref[...]).astype(BF16)
    v_ref[...] = kv[:, MEM_Q:].astype(BF16)


def _mem_kv(mem2, g, w_kv, kg):
    n = mem2.shape[0]
    return pl.pallas_call(
        _mem_kv_kernel,
        grid=(n // N_MEM,),
        in_specs=[pl.BlockSpec((N_MEM, D_MODEL), lambda i: (i, 0)),
                  pl.BlockSpec((1, D_MODEL), lambda i: (0, 0)),
                  pl.BlockSpec((D_MODEL, 2 * MEM_Q), lambda i: (0, 0)),
                  pl.BlockSpec((1, MEM_HEAD_DIM), lambda i: (0, 0))],
        out_specs=[pl.BlockSpec((N_MEM, MEM_Q), lambda i: (i, 0)), pl.BlockSpec((N_MEM, MEM_Q), lambda i: (i, 0))],
        out_shape=[jax.ShapeDtypeStruct((n, MEM_Q), BF16), jax.ShapeDtypeStruct((n, MEM_Q), BF16)],
        compiler_params=_cparams(("parallel",)),
        name="mem_kv",
    )(mem2, g, w_kv, kg)


def _mem_attn_kernel(q_ref, k_ref, v_ref, qg_ref, o_ref):
    scale = MEM_HEAD_DIM ** -0.5
    for hd in range(MEM_HEADS):
        sl = slice(hd * LANES, (hd + 1) * LANES)
        q = q_ref[0, :, sl].astype(F32)
        qms = jnp.mean(q * q, axis=-1, keepdims=True)
        qn = ((q * lax.rsqrt(qms + EPS)) * qg_ref[...]).astype(BF16)
        s = _dot_nt(qn, k_ref[0, :, sl]) * scale
        m = jnp.max(s, axis=-1, keepdims=True)
        p = jnp.exp(s - m)
        den = jnp.sum(p, axis=-1, keepdims=True)
        o = _dot(p.astype(BF16), v_ref[0, :, sl]) * (1.0 / den)
        o_ref[0, :, sl] = o.astype(BF16)


def _mem_attn(mq, k, v, qg):
    B, S, _ = mq.shape
    tm = ROW_TILE
    return pl.pallas_call(
        _mem_attn_kernel,
        grid=(B, S // tm),
        in_specs=[pl.BlockSpec((1, tm, MEM_Q), lambda b, i: (b, i, 0)),
                  pl.BlockSpec((1, N_MEM, MEM_Q), lambda b, i: (b, 0, 0)),
                  pl.BlockSpec((1, N_MEM, MEM_Q), lambda b, i: (b, 0, 0)),
                  pl.BlockSpec((1, MEM_HEAD_DIM), lambda b, i: (0, 0))],
        out_specs=pl.BlockSpec((1, tm, MEM_Q), lambda b, i: (b, i, 0)),
        out_shape=jax.ShapeDtypeStruct((B, S, MEM_Q), BF16),
        compiler_params=_cparams(("parallel", "parallel")),
        name="mem_attn",
    )(mq, k, v, qg)


def _merge_kernel(x_ref, ya_ref, of_ref, ob_ref, z_ref, ym_ref, gate_ref, og_ref, wa_ref, wd_ref, wm_ref,
                  wo_ref, o_ref):
    o = of_ref[...].astype(F32) + ob_ref[...].astype(F32)
    z = z_ref[...].astype(F32)
    og = og_ref[...]
    parts = []
    for pc in range(DN_V // LANES):
        os_ = o[:, pc * LANES:(pc + 1) * LANES]
        on = os_ * lax.rsqrt(_half_sums(os_ * os_) * (1.0 / DN_VALUE_DIM) + EPS) * og
        parts.append((on * _silu(z[:, pc * LANES:(pc + 1) * LANES])).astype(BF16))
    y_dn = jnp.concatenate(parts, axis=1)
    g = gate_ref[...].astype(F32)
    merged = (g[:, :D_MODEL] * _dot(ya_ref[...], wa_ref[...])
              + g[:, D_MODEL:2 * D_MODEL] * _dot(y_dn, wd_ref[...])
              + g[:, 2 * D_MODEL:] * _dot(ym_ref[...], wm_ref[...]))
    o_ref[...] = x_ref[...] + _dot(merged.astype(BF16), wo_ref[...])


def _merge(x2, ya, of, ob, z, ym, gates, og, wa, wd, wm, wo):
    n = x2.shape[0]
    tm = ROW_TILE
    row = lambda w_: pl.BlockSpec((tm, w_), lambda i: (i, 0))
    full = lambda a, b: pl.BlockSpec((a, b), lambda i: (0, 0))
    return pl.pallas_call(
        _merge_kernel,
        grid=(n // tm,),
        in_specs=[row(D_MODEL), row(ATTN_Q), row(DN_V), row(DN_V), row(DN_V), row(MEM_Q), row(N_BRANCH * D_MODEL),
                  full(1, LANES), full(ATTN_Q, D_MODEL), full(DN_V, D_MODEL), full(MEM_Q, D_MODEL),
                  full(D_MODEL, D_MODEL)],
        out_specs=row(D_MODEL),
        out_shape=jax.ShapeDtypeStruct((n, D_MODEL), F32),
        compiler_params=_cparams(("parallel",)),
        name="merge",
    )(x2, ya, of, ob, z, ym, gates, og, wa, wd, wm, wo)


def _ffn_kernel(xp_ref, xc_ref, xn_ref, g_ref, wu_ref, cw_ref, cb_ref, wd_ref, o_ref, acc_ref):
    i = pl.program_id(1)
    nt = pl.num_programs(1)
    tm = xc_ref.shape[1]
    xc = xc_ref[0]
    prev = jnp.where(i == 0, 0.0, xp_ref[0])
    nxt = jnp.where(i == nt - 1, 0.0, xn_ref[0])
    xe = jnp.concatenate([prev, xc, nxt], axis=0)
    ms = jnp.mean(xe * xe, axis=-1, keepdims=True)
    h = ((xe * lax.rsqrt(ms + EPS)) * g_ref[...]).astype(BF16)
    rows = tm + 2 * HALO
    acc_ref[...] = xc

    def conv(u, c0):
        out = None
        for j in range(FFN_CONV):
            sh = (FFN_CONV // 2 - j) % rows
            us = u if sh == 0 else pltpu.roll(u, sh, axis=0)
            term = us[HALO:HALO + tm] * cw_ref[j:j + 1, c0:c0 + FF_CHUNK]
            out = term if out is None else out + term
        return out + cb_ref[:, c0:c0 + FF_CHUNK]

    for c in range(D_FF // FF_CHUNK):
        c0 = c * FF_CHUNK
        ug = conv(_dot(h, wu_ref[:, c0:c0 + FF_CHUNK]), c0)
        uv = conv(_dot(h, wu_ref[:, D_FF + c0:D_FF + c0 + FF_CHUNK]), D_FF + c0)
        act = (_silu(ug) * uv).astype(BF16)
        acc_ref[...] += _dot(act, wd_ref[c0:c0 + FF_CHUNK, :])
    o_ref[0] = acc_ref[...]


def _ffn(x1, g, wu, cw, cb, wd):
    B, S, _ = x1.shape
    tm = ROW_TILE
    nt = S // tm
    hb = tm // HALO
    return pl.pallas_call(
        _ffn_kernel,
        grid=(B, nt),
        in_specs=[pl.BlockSpec((1, HALO, D_MODEL), lambda b, i: (b, jnp.maximum(i * hb - 1, 0), 0)),
                  pl.BlockSpec((1, tm, D_MODEL), lambda b, i: (b, i, 0)),
                  pl.BlockSpec((1, HALO, D_MODEL), lambda b, i: (b, jnp.minimum((i + 1) * hb, S // HALO - 1), 0)),
                  pl.BlockSpec((1, D_MODEL), lambda b, i: (0, 0)),
                  pl.BlockSpec((D_MODEL, 2 * D_FF), lambda b, i: (0, 0)),
                  pl.BlockSpec((FFN_CONV, 2 * D_FF), lambda b, i: (0, 0)),
                  pl.BlockSpec((1, 2 * D_FF), lambda b, i: (0, 0)),
                  pl.BlockSpec((D_FF, D_MODEL), lambda b, i: (0, 0))],
        out_specs=pl.BlockSpec((1, tm, D_MODEL), lambda b, i: (b, i, 0)),
        out_shape=jax.ShapeDtypeStruct((B, S, D_MODEL), F32),
        scratch_shapes=[pltpu.VMEM((tm, D_MODEL), F32)],
        compiler_params=_cparams(("parallel", "parallel")),
        name="ffn",
    )(x1, x1, x1, g, wu, cw, cb, wd)


def _permute_w_in(w):
    idx = np.cumsum((0,) + IN_SPLITS)
    seg = lambda k: w[:, idx[k]:idx[k + 1]]
    aq, ak, av, dq, dk, dv, dz, db, da, mq, gates = (seg(k) for k in range(11))
    pad = jnp.zeros((w.shape[0], LANES - 4 * DN_HEADS), w.dtype)
    return jnp.concatenate([aq, ak, av, dq, dk, dv, dz, mq, gates, db, da, pad], axis=1).astype(BF16)


def _layer(x, mem, rel_bias_table, p):
    B, S, D = x.shape
    n = B * S
    x2 = x.reshape(n, D)
    row = lambda a: a.reshape(1, -1).astype(F32)
    aq, akv, dqkv, dz, mq, gates, ba = _inproj(x2, row(p["norm_mix_g"]), _permute_w_in(p["w_in"]))
    y_attn = _attn(aq.reshape(B, S, -1), akv.reshape(B, S, -1), rel_bias_table, p["attn_sink"],
                   p["attn_q_norm_g"], p["attn_k_norm_g"])
    qkvn, betax, gcx, gct = _dn_prep(dqkv.reshape(B, S, -1), ba.reshape(B, S, -1), p["dn_conv_w"],
                                     p["dn_a_log"], p["dn_dt_bias"])
    o_f, o_b = _dn_scan(*_dn_chunk(qkvn, betax, gcx, gct))
    mk, mv = _mem_kv(mem.reshape(B * N_MEM, D), row(p["mem_norm_g"]), p["mem_w_kv"].astype(BF16),
                     row(p["mem_k_norm_g"]))
    y_mem = _mem_attn(mq.reshape(B, S, -1), mk.reshape(B, N_MEM, -1), mv.reshape(B, N_MEM, -1),
                      row(p["mem_q_norm_g"]))
    og2 = jnp.tile(p["dn_out_norm_g"].astype(F32), 2)[None]
    x1 = _merge(x2, y_attn.reshape(n, -1), o_f.reshape(n, -1), o_b.reshape(n, -1), dz, y_mem.reshape(n, -1), gates,
                og2, p["w_br_attn"].astype(BF16), p["w_br_dn"].astype(BF16), p["w_br_mem"].astype(BF16),
                p["w_out"].astype(BF16))
    return _ffn(x1.reshape(B, S, D), row(p["norm_ffn_g"]), p["ffn_w_up"].astype(BF16), p["ffn_conv_w"].astype(F32),
                row(p["ffn_conv_b"]), p["ffn_w_down"].astype(BF16))


_LAYER_PARAMS = ("norm_mix_g", "w_in", "attn_q_norm_g", "attn_k_norm_g", "attn_sink", "dn_conv_w", "dn_a_log",
                 "dn_dt_bias", "dn_out_norm_g", "mem_norm_g", "mem_w_kv", "mem_q_norm_g", "mem_k_norm_g",
                 "w_br_attn", "w_br_dn", "w_br_mem", "w_out", "norm_ffn_g", "ffn_w_up", "ffn_conv_w", "ffn_conv_b",
                 "ffn_w_down")


def kernel(x, mem, rel_bias_table, norm_mix_g, w_in, attn_q_norm_g, attn_k_norm_g, attn_sink, dn_conv_w, dn_a_log,
           dn_dt_bias, dn_out_norm_g, mem_norm_g, mem_w_kv, mem_q_norm_g, mem_k_norm_g, w_br_attn, w_br_dn,
           w_br_mem, w_out, norm_ffn_g, ffn_w_up, ffn_conv_w, ffn_conv_b, ffn_w_down):
    stacked = dict(zip(_LAYER_PARAMS, (norm_mix_g, w_in, attn_q_norm_g, attn_k_norm_g, attn_sink, dn_conv_w,
                                       dn_a_log, dn_dt_bias, dn_out_norm_g, mem_norm_g, mem_w_kv, mem_q_norm_g,
                                       mem_k_norm_g, w_br_attn, w_br_dn, w_br_mem, w_out, norm_ffn_g, ffn_w_up,
                                       ffn_conv_w, ffn_conv_b, ffn_w_down)))
    depth = w_in.shape[0]
    for l in range(depth):
        x = _layer(x, mem, rel_bias_table, {k: v[l] for k, v in stacked.items()})
    return x
```

```python
import functools
import math

import numpy as np
import jax
import jax.numpy as jnp
from jax import lax
from jax.experimental import pallas as pl
from jax.experimental.pallas import tpu as pltpu

F32 = jnp.float32
BF16 = jnp.bfloat16

EPS = 1e-6
D_MODEL = 1024
N_MEM = 256
ATTN_HEADS = 8
ATTN_KV_HEADS = 2
ATTN_HEAD_DIM = 64
WINDOW = 128
ATTN_BLOCK = 128
REL_BUCKETS = 32
REL_MAX_DIST = 128
DN_HEADS = 8
DN_KEY_DIM = 64
DN_VALUE_DIM = 64
DN_CONV = 5
DN_CHUNK = 64
MEM_HEADS = 4
MEM_HEAD_DIM = 128
D_FF = 2816
FFN_CONV = 3
N_BRANCH = 3

ATTN_Q = ATTN_HEADS * ATTN_HEAD_DIM
ATTN_KV = ATTN_KV_HEADS * ATTN_HEAD_DIM
DN_QK = DN_HEADS * DN_KEY_DIM
DN_V = DN_HEADS * DN_VALUE_DIM
MEM_Q = MEM_HEADS * MEM_HEAD_DIM
IN_SPLITS = (ATTN_Q, ATTN_KV, ATTN_KV, DN_QK, DN_QK, DN_V, DN_V, 2 * DN_HEADS, 2 * DN_HEADS, MEM_Q,
             N_BRANCH * D_MODEL)

LANES = 128
HALF = 64
HALO = 8
NEG = -1e30
VMEM_LIMIT = 56 * 1024 * 1024

ROW_TILE = 512
FFN_ROW_TILE = 1024
FF_CHUNK = 256


LOG2E = math.log2(math.e)


def _cparams(sem):
    return pltpu.CompilerParams(dimension_semantics=sem, vmem_limit_bytes=VMEM_LIMIT)


def _resident(shape):
    zeros = (0,) * len(shape)
    return pl.BlockSpec(shape, lambda *_: zeros, pipeline_mode=pl.Buffered(1))


def _dot(a, b):
    return jnp.dot(a, b, preferred_element_type=F32)


def _dot_nt(a, b):
    return lax.dot_general(a, b, (((1,), (1,)), ((), ())), preferred_element_type=F32)


def _dot_tn(a, b):
    return lax.dot_general(a, b, (((0,), (0,)), ((), ())), preferred_element_type=F32)


def _lane_is_low(shape):
    lane = lax.broadcasted_iota(jnp.int32, shape, len(shape) - 1)
    return (lane % LANES) < HALF


def _half_sums(sq):
    low = _lane_is_low(sq.shape)
    s_lo = jnp.sum(jnp.where(low, sq, 0.0), axis=-1, keepdims=True)
    s_hi = jnp.sum(jnp.where(low, 0.0, sq), axis=-1, keepdims=True)
    return jnp.where(low, s_lo, s_hi)


def _silu(x):
    return x * (1.0 / (1.0 + jnp.exp(-x)))


def _sigmoid(x):
    return 1.0 / (1.0 + jnp.exp(-x))


_C_AQ = (0, 512)
_C_AKV = (512, 768)
_C_DQKV = (768, 2304)
_C_DZ = (2304, 2816)
_C_MQ = (2816, 3328)
_C_GATE = (3328, 6400)
_C_BA = (6400, 6528)
_N_IN = 6528


def _head_rmsnorm(t, gain2):
    return t * lax.rsqrt(_half_sums(t * t) * (1.0 / ATTN_HEAD_DIM) + EPS) * gain2


def _inproj_kernel(x_ref, g_ref, w_ref, qg_ref, kg_ref, aq_ref, akv_ref, dqkv_ref, dz_ref, mq_ref, gate_ref, ba_ref):
    x = x_ref[...]
    ms = jnp.mean(x * x, axis=-1, keepdims=True)
    h = ((x * lax.rsqrt(ms + EPS)) * g_ref[...]).astype(BF16)

    def proj(c):
        return _dot(h, w_ref[:, c[0]:c[1]])

    aq = proj(_C_AQ)
    q_scale = ATTN_HEAD_DIM ** -0.5 * LOG2E
    for pc in range(ATTN_Q // LANES):
        sl = slice(pc * LANES, (pc + 1) * LANES)
        aq_ref[:, sl] = (_head_rmsnorm(aq[:, sl], qg_ref[...]) * q_scale).astype(BF16)
    akv = proj(_C_AKV)
    kn = _head_rmsnorm(akv[:, :LANES], kg_ref[...])
    av = akv[:, LANES:]
    akv_ref[:, 0 * LANES:1 * LANES] = kn.astype(BF16)
    akv_ref[:, 1 * LANES:2 * LANES] = av.astype(BF16)
    akv_ref[:, 2 * LANES:3 * LANES] = pltpu.roll(kn, HALF, axis=1).astype(BF16)
    akv_ref[:, 3 * LANES:4 * LANES] = pltpu.roll(av, HALF, axis=1).astype(BF16)
    dqkv_ref[...] = proj(_C_DQKV).astype(BF16)
    dz_ref[...] = proj(_C_DZ).astype(BF16)
    mq_ref[...] = proj(_C_MQ).astype(BF16)
    gate_ref[...] = _sigmoid(proj(_C_GATE)).astype(BF16)
    ba_ref[...] = proj(_C_BA)


def _inproj(x2, g, w, qg2, kg2):
    n = x2.shape[0]
    tm = ROW_TILE
    widths = [(512, BF16), (512, BF16), (1536, BF16), (512, BF16), (512, BF16), (3072, BF16), (128, F32)]
    return pl.pallas_call(
        _inproj_kernel,
        grid=(n // tm,),
        in_specs=[pl.BlockSpec((tm, D_MODEL), lambda i: (i, 0)),
                  _resident((1, D_MODEL)), _resident((D_MODEL, _N_IN)), _resident((1, LANES)), _resident((1, LANES))],
        out_specs=[pl.BlockSpec((tm, w_), lambda i: (i, 0)) for w_, _ in widths],
        out_shape=[jax.ShapeDtypeStruct((n, w_), dt) for w_, dt in widths],
        compiler_params=_cparams(("parallel",)),
        name="inproj",
    )(x2, g, w, qg2, kg2)


def _t5_buckets(rel):
    nb = REL_BUCKETS // 2
    max_exact = nb // 2
    ret = (rel > 0).astype(np.int32) * nb
    n = np.abs(rel)
    large = max_exact + (np.log(np.maximum(n, 1) / max_exact) / np.log(REL_MAX_DIST / max_exact)
                         * (nb - max_exact)).astype(np.int32)
    large = np.minimum(large, nb - 1)
    return (ret + np.where(n < max_exact, n, large)).astype(np.int32)


_ATTN_GROUPS = ((0, 1, True, False), (0, 1, False, True), (2, 3, True, True), (2, 3, False, False))
_ATTN_GROUP_HEADS = ((0, 2), (1, 3), (4, 6), (5, 7))


def _attn_kernel(q_ref, kp_ref, kc_ref, kn_ref, bias_ref, sink_ref, o_ref):
    T = ATTN_BLOCK
    kv = jnp.concatenate([kp_ref[0], kc_ref[0], kn_ref[0]], axis=0)
    ones = jnp.ones((3 * T, LANES), BF16)
    k_n = kv[:, 0 * LANES:1 * LANES]
    v_n = jnp.concatenate([kv[:, 1 * LANES:2 * LANES], ones], axis=1)
    k_s = kv[:, 2 * LANES:3 * LANES]
    v_s = jnp.concatenate([kv[:, 3 * LANES:4 * LANES], ones], axis=1)
    low_q = _lane_is_low((T, LANES))
    zero = jnp.zeros((T, LANES), BF16)
    res = []
    for gi, (pa, pb, low, swapped) in enumerate(_ATTN_GROUPS):
        sel = low_q if low else jnp.logical_not(low_q)
        lhs = jnp.concatenate([jnp.where(sel, q_ref[0, :, pa * LANES:(pa + 1) * LANES], zero),
                               jnp.where(sel, q_ref[0, :, pb * LANES:(pb + 1) * LANES], zero)], axis=0)
        s = _dot_nt(lhs, k_s if swapped else k_n) + bias_ref[0, gi]
        sk = sink_ref[gi]
        m = jnp.maximum(jnp.max(s, axis=-1, keepdims=True), sk)
        p = jnp.exp2(s - m)
        r = _dot(p.astype(BF16), v_s if swapped else v_n)
        den = r[:, LANES:] + jnp.exp2(sk - m)
        res.append(r[:, :LANES] * (1.0 / den))
    for pc, (ge, go) in enumerate(((0, 1), (0, 1), (2, 3), (2, 3))):
        r0 = (pc % 2) * T
        out = jnp.where(low_q, res[ge][r0:r0 + T], res[go][r0:r0 + T])
        o_ref[0, :, pc * LANES:(pc + 1) * LANES] = out.astype(BF16)


def _attn(aq, akv, rel_table, sink):
    B, S, _ = aq.shape
    T = ATTN_BLOCK
    nb = S // T
    assert nb >= 2
    t_idx = np.arange(T)[:, None]
    j_idx = np.arange(3 * T)[None, :]
    rel = j_idx - T - t_idx
    onehot = jnp.asarray(np.eye(REL_BUCKETS, dtype=np.float32)[_t5_buckets(rel)])
    bias = jnp.einsum("tjr,rh->htj", onehot, rel_table.astype(F32), precision=lax.Precision.HIGHEST) * LOG2E
    in_win = np.abs(rel) <= WINDOW
    edge_ok = np.stack([in_win & (j_idx >= T), in_win, in_win & (j_idx < 2 * T)])
    bias = jnp.where(jnp.asarray(edge_ok)[:, None], bias[None], NEG)
    bias_g = jnp.stack([jnp.concatenate([bias[:, a], bias[:, b]], axis=1) for a, b in _ATTN_GROUP_HEADS], axis=1)
    sk = sink.astype(F32) * LOG2E
    sink_g = jnp.stack([jnp.concatenate([jnp.full((T, 1), 1.0) * sk[a], jnp.full((T, 1), 1.0) * sk[b]], axis=0)
                        for a, b in _ATTN_GROUP_HEADS])
    kv_w = akv.shape[-1]
    return pl.pallas_call(
        _attn_kernel,
        grid=(B, nb),
        in_specs=[pl.BlockSpec((1, T, ATTN_Q), lambda b, i: (b, i, 0)),
                  pl.BlockSpec((1, T, kv_w), lambda b, i: (b, jnp.maximum(i - 1, 0), 0)),
                  pl.BlockSpec((1, T, kv_w), lambda b, i: (b, i, 0)),
                  pl.BlockSpec((1, T, kv_w), lambda b, i: (b, jnp.minimum(i + 1, nb - 1), 0)),
                  pl.BlockSpec((1, 4, 2 * T, 3 * T),
                               lambda b, i: (jnp.where(i == 0, 0, jnp.where(i == nb - 1, 2, 1)), 0, 0, 0)),
                  pl.BlockSpec((4, 2 * T, 1), lambda b, i: (0, 0, 0))],
        out_specs=pl.BlockSpec((1, T, ATTN_Q), lambda b, i: (b, i, 0)),
        out_shape=jax.ShapeDtypeStruct((B, S, ATTN_Q), BF16),
        compiler_params=_cparams(("parallel", "parallel")),
        name="attn",
    )(aq, akv, akv, akv, bias_g, sink_g)


def _split_hi_lo(x):
    hi = x.astype(BF16)
    lo = (x - hi.astype(F32)).astype(BF16)
    return hi, lo


def _dn_prep_kernel(xp_ref, xc_ref, xn_ref, ba_ref, cw_ref, alog_ref, dtb_ref, tri_ref, expand_ref,
                    qkv_ref, betax_ref, gcx_ref, gct_ref):
    i = pl.program_id(1)
    nt = pl.num_programs(1)
    tm = xc_ref.shape[1]
    prev = jnp.where(i == 0, 0.0, xp_ref[0].astype(F32))
    nxt = jnp.where(i == nt - 1, 0.0, xn_ref[0].astype(F32))
    xe = jnp.concatenate([prev, xc_ref[0].astype(F32), nxt], axis=0)
    rows = tm + 2 * HALO
    acc = None
    for j in range(DN_CONV):
        sh = (DN_CONV // 2 - j) % rows
        xs = xe if sh == 0 else pltpu.roll(xe, sh, axis=0)
        term = xs[HALO:HALO + tm] * cw_ref[j:j + 1, :]
        acc = term if acc is None else acc + term
    y = _silu(acc)
    for pc in range(2 * DN_QK // LANES):
        ys = y[:, pc * LANES:(pc + 1) * LANES]
        yn = ys * lax.rsqrt(_half_sums(ys * ys) + EPS)
        if pc < DN_QK // LANES:
            yn = yn * (DN_KEY_DIM ** -0.5)
        qkv_ref[0, :, pc * LANES:(pc + 1) * LANES] = yn.astype(BF16)
    qkv_ref[0, :, 2 * DN_QK:] = y[:, 2 * DN_QK:].astype(BF16)

    ba = ba_ref[0]
    nh2 = 2 * DN_HEADS
    beta = _sigmoid(ba[:, :nh2])
    z = ba[:, nh2:2 * nh2] + dtb_ref[...]
    sp = jnp.maximum(z, 0.0) + jnp.log1p(jnp.exp(-jnp.abs(z)))
    g = -jnp.exp(alog_ref[...]) * sp
    g_hi, g_lo = _split_hi_lo(g)
    gg = jnp.concatenate([g_hi, g_lo], axis=1)
    pre = _dot(tri_ref[0], gg)
    suf = _dot(tri_ref[1], gg)
    lane16 = lax.broadcasted_iota(jnp.int32, (tm, nh2), 1)
    gc = jnp.where(lane16 < DN_HEADS, pre[:, :nh2] + pre[:, nh2:], suf[:, :nh2] + suf[:, nh2:])
    b_hi, b_lo = _split_hi_lo(beta)
    c_hi, c_lo = _split_hi_lo(gc)
    c_lo2 = (gc - c_hi.astype(F32) - c_lo.astype(F32)).astype(BF16)
    ex = expand_ref[...]
    betax_ref[0] = _dot(b_hi, ex) + _dot(b_lo, ex)
    gcx_ref[0] = _dot(c_hi, ex) + _dot(c_lo, ex) + _dot(c_lo2, ex)
    gct = jnp.concatenate([gc, jnp.zeros((tm, LANES - nh2), F32)], axis=1).T
    for c in range(tm // DN_CHUNK):
        gct_ref[0, c] = gct[:nh2, c * DN_CHUNK:(c + 1) * DN_CHUNK]


def _dn_prep(dqkv, ba, conv_w, a_log, dt_bias):
    B, S, W = dqkv.shape
    tm = ROW_TILE
    nt = S // tm
    hb = tm // HALO
    r = np.arange(tm)
    same = (r[:, None] // DN_CHUNK) == (r[None, :] // DN_CHUNK)
    tri = np.stack([same & (r[:, None] >= r[None, :]), same & (r[:, None] <= r[None, :])]).astype(np.float32)
    expand = np.repeat(np.eye(2 * DN_HEADS, dtype=np.float32), HALF, axis=1)
    nh2 = 2 * DN_HEADS
    return pl.pallas_call(
        _dn_prep_kernel,
        grid=(B, nt),
        in_specs=[pl.BlockSpec((1, HALO, W), lambda b, i: (b, jnp.maximum(i * hb - 1, 0), 0)),
                  pl.BlockSpec((1, tm, W), lambda b, i: (b, i, 0)),
                  pl.BlockSpec((1, HALO, W), lambda b, i: (b, jnp.minimum((i + 1) * hb, S // HALO - 1), 0)),
                  pl.BlockSpec((1, tm, LANES), lambda b, i: (b, i, 0)),
                  pl.BlockSpec((DN_CONV, W), lambda b, i: (0, 0)),
                  pl.BlockSpec((1, nh2), lambda b, i: (0, 0)),
                  pl.BlockSpec((1, nh2), lambda b, i: (0, 0)),
                  pl.BlockSpec((2, tm, tm), lambda b, i: (0, 0, 0)),
                  pl.BlockSpec((nh2, nh2 * HALF), lambda b, i: (0, 0))],
        out_specs=[pl.BlockSpec((1, tm, W), lambda b, i: (b, i, 0)),
                   pl.BlockSpec((1, tm, nh2 * HALF), lambda b, i: (b, i, 0)),
                   pl.BlockSpec((1, tm, nh2 * HALF), lambda b, i: (b, i, 0)),
                   pl.BlockSpec((1, tm // DN_CHUNK, nh2, DN_CHUNK), lambda b, i: (b, i, 0, 0))],
        out_shape=[jax.ShapeDtypeStruct((B, S, W), BF16),
                   jax.ShapeDtypeStruct((B, S, nh2 * HALF), F32),
                   jax.ShapeDtypeStruct((B, S, nh2 * HALF), F32),
                   jax.ShapeDtypeStruct((B, S // DN_CHUNK, nh2, DN_CHUNK), F32)],
        compiler_params=_cparams(("parallel", "parallel")),
        name="dn_prep",
    )(dqkv, dqkv, dqkv, ba, conv_w.astype(F32), a_log.reshape(1, nh2).astype(F32),
      dt_bias.reshape(1, nh2).astype(F32), jnp.asarray(tri, BF16), jnp.asarray(expand, BF16))


def _block_diag(x2, low):
    zero = jnp.zeros_like(x2)
    return jnp.concatenate([jnp.where(low, x2, zero), jnp.where(low, zero, x2)], axis=0)


DN_PAIRS = DN_HEADS // 2
DN_CHUNKS_PER_STEP = 2
DN_SCAN_CHUNKS = 4


def _dn_chunk_kernel(qkv_ref, bx_ref, gx_ref, gt_ref, u_ref, w_ref, qkd_ref, qin_ref, kupt_ref, egt_ref):
    C = DN_CHUNK
    low = _lane_is_low((C, LANES))
    r = lax.broadcasted_iota(jnp.int32, (C, LANES), 0)
    m = lax.broadcasted_iota(jnp.int32, (C, LANES), 1) % HALF
    eye = jnp.where(r == m, 1.0, 0.0)
    probs = []
    for ch in range(DN_CHUNKS_PER_STEP):
        rows = slice(ch * C, (ch + 1) * C)
        for pr in range(DN_PAIRS):
            q2 = qkv_ref[0, rows, pr * LANES:(pr + 1) * LANES]
            k2 = qkv_ref[0, rows, DN_QK + pr * LANES:DN_QK + (pr + 1) * LANES]
            v2 = qkv_ref[0, rows, 2 * DN_QK + pr * LANES:2 * DN_QK + (pr + 1) * LANES]
            q2f, k2f, v2f = q2.astype(F32), k2.astype(F32), v2.astype(F32)
            qkk = _dot_nt(jnp.concatenate([q2, k2], axis=0), _block_diag(k2, low))
            qk, kk = qkk[:C], qkk[C:]
            for d in range(2):
                col = slice(d * DN_QK + pr * LANES, d * DN_QK + (pr + 1) * LANES)
                beta2 = bx_ref[0, rows, col]
                gcol2 = gx_ref[0, rows, col]
                hrow = d * DN_HEADS + 2 * pr
                grow2 = jnp.concatenate([gt_ref[0, ch, hrow:hrow + 1, :], gt_ref[0, ch, hrow + 1:hrow + 2, :]],
                                        axis=1)
                incl = (r >= m) if d == 0 else (r <= m)
                strict = (r > m) if d == 0 else (r < m)
                decay = jnp.where(incl, jnp.exp(jnp.where(incl, gcol2 - grow2, 0.0)), 0.0)
                gtot2 = gcol2[C - 1:C] if d == 0 else gcol2[0:1]
                e_col = jnp.exp(gcol2)
                p = jnp.where(strict, -(kk * beta2 * decay), 0.0)
                vb = (v2f * beta2).astype(BF16)
                kbe = (k2f * beta2 * e_col).astype(BF16)
                rhs = jnp.concatenate([_block_diag(vb, low), _block_diag(kbe, low)], axis=1)
                qkd_ref[0, rows, col] = (qk * decay).astype(BF16)
                qin_ref[0, rows, col] = (q2f * e_col).astype(BF16)
                kupt_ref[0, ch, d, pr] = (k2f * jnp.exp(gtot2 - gcol2)).T.astype(BF16)
                egt_ref[0, ch, d, pr:pr + 1, :] = jnp.exp(gtot2)
                probs.append([p, eye + p, rhs, rows, col])
    for pb in probs:
        pb[0] = _dot(pb[0].astype(BF16), _block_diag(pb[0].astype(BF16), low))
    for _ in range(4):
        for pb in probs:
            both = _dot(jnp.concatenate([pb[1], pb[0]], axis=0).astype(BF16),
                        _block_diag(pb[0].astype(BF16), low))
            pb[1] = pb[1] + both[:C]
            pb[0] = both[C:]
    for p, t, rhs, rows, col in probs:
        t = t + _dot(t.astype(BF16), _block_diag(p.astype(BF16), low))
        sol = _dot(t.astype(BF16), rhs)
        u_ref[0, rows, col] = sol[:, :LANES].astype(BF16)
        w_ref[0, rows, col] = sol[:, LANES:].astype(BF16)


def _dn_chunk(qkvn, betax, gcx, gct):
    B, S, W = qkvn.shape
    C = DN_CHUNK
    cps = DN_CHUNKS_PER_STEP
    nc = S // C
    wide = 2 * DN_QK
    blk = lambda w_: pl.BlockSpec((1, cps * C, w_), lambda b, i: (b, i, 0))
    return pl.pallas_call(
        _dn_chunk_kernel,
        grid=(B, nc // cps),
        in_specs=[blk(W), blk(wide), blk(wide),
                  pl.BlockSpec((1, cps, 2 * DN_HEADS, C), lambda b, i: (b, i, 0, 0))],
        out_specs=[blk(wide), blk(wide), blk(wide), blk(wide),
                   pl.BlockSpec((1, cps, 2, DN_PAIRS, LANES, C), lambda b, i: (b, i, 0, 0, 0, 0)),
                   pl.BlockSpec((1, cps, 2, DN_PAIRS, LANES), lambda b, i: (b, i, 0, 0, 0))],
        out_shape=[jax.ShapeDtypeStruct((B, S, wide), BF16)] * 4
        + [jax.ShapeDtypeStruct((B, nc, 2, DN_PAIRS, LANES, C), BF16),
           jax.ShapeDtypeStruct((B, nc, 2, DN_PAIRS, LANES), F32)],
        compiler_params=_cparams(("parallel", "parallel")),
        name="dn_chunk",
    )(qkvn, betax, gcx, gct)


def _dn_scan_kernel(uf, wf, qkdf, qinf, kuptf, egtf, ub, wb, qkdb, qinb, kuptb, egtb, of_ref, ob_ref, st_ref):
    @pl.when(pl.program_id(1) == 0)
    def _():
        st_ref[...] = jnp.zeros_like(st_ref)

    C = DN_CHUNK
    low = _lane_is_low((C, LANES))
    rr = lax.broadcasted_iota(jnp.int32, (LANES, LANES), 0) < HALF
    cc = lax.broadcasted_iota(jnp.int32, (LANES, LANES), 1) < HALF
    on_diag = rr == cc
    dirs = ((uf, wf, qkdf, qinf, kuptf, egtf, of_ref), (ub, wb, qkdb, qinb, kuptb, egtb, ob_ref))
    prob_ids = [(d, pr) for d in range(2) for pr in range(DN_PAIRS)]
    state = {dp: st_ref[dp[0], dp[1]] for dp in prob_ids}
    for step in range(DN_SCAN_CHUNKS):
        ws, vb = {}, {}
        for d, pr in prob_ids:
            u_ref, w_ref, qkd_ref, qin_ref, kupt_ref, egt_ref, o_ref = dirs[d]
            ch = step if d == 0 else DN_SCAN_CHUNKS - 1 - step
            rows = slice(ch * C, (ch + 1) * C)
            sl = slice(pr * LANES, (pr + 1) * LANES)
            lhs = jnp.concatenate([w_ref[0, rows, sl], qin_ref[0, rows, sl]], axis=0)
            ws[d, pr] = _dot(lhs, state[d, pr].astype(BF16))
        for d, pr in prob_ids:
            u_ref = dirs[d][0]
            ch = step if d == 0 else DN_SCAN_CHUNKS - 1 - step
            rows = slice(ch * C, (ch + 1) * C)
            sl = slice(pr * LANES, (pr + 1) * LANES)
            vb[d, pr] = (u_ref[0, rows, sl].astype(F32) - ws[d, pr][:C]).astype(BF16)
        for d, pr in prob_ids:
            u_ref, w_ref, qkd_ref, qin_ref, kupt_ref, egt_ref, o_ref = dirs[d]
            ch = step if d == 0 else DN_SCAN_CHUNKS - 1 - step
            rows = slice(ch * C, (ch + 1) * C)
            sl = slice(pr * LANES, (pr + 1) * LANES)
            upd = _dot(kupt_ref[0, ch, 0, pr], vb[d, pr])
            state[d, pr] = state[d, pr] * egt_ref[0, ch, 0, pr:pr + 1, :] + jnp.where(on_diag, upd, 0.0)
            o2 = ws[d, pr][C:] + _dot(qkd_ref[0, rows, sl], _block_diag(vb[d, pr], low))
            o_ref[0, rows, sl] = o2.astype(BF16)
    for d, pr in prob_ids:
        st_ref[d, pr] = state[d, pr]


def _dn_scan(u, w, qkd, qin, kupt, egt):
    B, S, _ = u.shape
    C = DN_CHUNK
    sc = DN_SCAN_CHUNKS
    nb = S // (C * sc)
    fwd3 = lambda b, i: (b, i, 0)
    bwd3 = lambda b, i: (b, nb - 1 - i, 1)
    row_f = pl.BlockSpec((1, sc * C, DN_QK), fwd3)
    row_b = pl.BlockSpec((1, sc * C, DN_QK), bwd3)
    kup_f = pl.BlockSpec((1, sc, 1, DN_PAIRS, LANES, C), lambda b, i: (b, i, 0, 0, 0, 0))
    kup_b = pl.BlockSpec((1, sc, 1, DN_PAIRS, LANES, C), lambda b, i: (b, nb - 1 - i, 1, 0, 0, 0))
    egt_f = pl.BlockSpec((1, sc, 1, DN_PAIRS, LANES), lambda b, i: (b, i, 0, 0, 0))
    egt_b = pl.BlockSpec((1, sc, 1, DN_PAIRS, LANES), lambda b, i: (b, nb - 1 - i, 1, 0, 0))
    return pl.pallas_call(
        _dn_scan_kernel,
        grid=(B, nb),
        in_specs=[row_f, row_f, row_f, row_f, kup_f, egt_f, row_b, row_b, row_b, row_b, kup_b, egt_b],
        out_specs=[pl.BlockSpec((1, sc * C, DN_V), fwd3),
                   pl.BlockSpec((1, sc * C, DN_V), lambda b, i: (b, nb - 1 - i, 0))],
        out_shape=[jax.ShapeDtypeStruct((B, S, DN_V), BF16), jax.ShapeDtypeStruct((B, S, DN_V), BF16)],
        scratch_shapes=[pltpu.VMEM((2, DN_PAIRS, LANES, LANES), F32)],
        compiler_params=_cparams(("parallel", "arbitrary")),
        name="dn_scan",
    )(u, w, qkd, qin, kupt, egt, u, w, qkd, qin, kupt, egt)


def _mem_kv_kernel(m_ref, g_ref, w_ref, kg_ref, k_ref, v_ref):
    x = m_ref[...]
    ms = jnp.mean(x * x, axis=-1, keepdims=True)
    h = ((x * lax.rsqrt(ms + EPS)) * g_ref[...]).astype(BF16)
    kv = _dot(h, w_ref[...])
    for hd in range(MEM_HEADS):
        kh = kv[:, hd * LANES:(hd + 1) * LANES]
        kms = jnp.mean(kh * kh, axis=-1, keepdims=True)
        k_ref[:, hd * LANES:(hd + 1) * LANES] = ((kh * lax.rsqrt(kms + EPS)) * kg_ref[...]).astype(BF16)
    v_ref[...] = kv[:, MEM_Q:].astype(BF16)


def _mem_kv(mem2, g, w_kv, kg):
    n = mem2.shape[0]
    return pl.pallas_call(
        _mem_kv_kernel,
        grid=(n // N_MEM,),
        in_specs=[pl.BlockSpec((N_MEM, D_MODEL), lambda i: (i, 0)),
                  pl.BlockSpec((1, D_MODEL), lambda i: (0, 0)),
                  pl.BlockSpec((D_MODEL, 2 * MEM_Q), lambda i: (0, 0)),
                  pl.BlockSpec((1, MEM_HEAD_DIM), lambda i: (0, 0))],
        out_specs=[pl.BlockSpec((N_MEM, MEM_Q), lambda i: (i, 0)), pl.BlockSpec((N_MEM, MEM_Q), lambda i: (i, 0))],
        out_shape=[jax.ShapeDtypeStruct((n, MEM_Q), BF16), jax.ShapeDtypeStruct((n, MEM_Q), BF16)],
        compiler_params=_cparams(("parallel",)),
        name="mem_kv",
    )(mem2, g, w_kv, kg)


def _mem_attn_kernel(q_ref, k_ref, v_ref, qg_ref, o_ref):
    scale = MEM_HEAD_DIM ** -0.5
    for hd in range(MEM_HEADS):
        sl = slice(hd * LANES, (hd + 1) * LANES)
        q = q_ref[0, :, sl].astype(F32)
        qms = jnp.mean(q * q, axis=-1, keepdims=True)
        qn = ((q * lax.rsqrt(qms + EPS)) * qg_ref[...]).astype(BF16)
        s = _dot_nt(qn, k_ref[0, :, sl]) * scale
        m = jnp.max(s, axis=-1, keepdims=True)
        p = jnp.exp(s - m)
        den = jnp.sum(p, axis=-1, keepdims=True)
        o = _dot(p.astype(BF16), v_ref[0, :, sl]) * (1.0 / den)
        o_ref[0, :, sl] = o.astype(BF16)


def _mem_attn(mq, k, v, qg):
    B, S, _ = mq.shape
    tm = ROW_TILE
    return pl.pallas_call(
        _mem_attn_kernel,
        grid=(B, S // tm),
        in_specs=[pl.BlockSpec((1, tm, MEM_Q), lambda b, i: (b, i, 0)),
                  pl.BlockSpec((1, N_MEM, MEM_Q), lambda b, i: (b, 0, 0)),
                  pl.BlockSpec((1, N_MEM, MEM_Q), lambda b, i: (b, 0, 0)),
                  pl.BlockSpec((1, MEM_HEAD_DIM), lambda b, i: (0, 0))],
        out_specs=pl.BlockSpec((1, tm, MEM_Q), lambda b, i: (b, i, 0)),
        out_shape=jax.ShapeDtypeStruct((B, S, MEM_Q), BF16),
        compiler_params=_cparams(("parallel", "parallel")),
        name="mem_attn",
    )(mq, k, v, qg)


def _merge_kernel(x_ref, ya_ref, of_ref, ob_ref, z_ref, ym_ref, gate_ref, og_ref, wa_ref, wd_ref, wm_ref,
                  wo_ref, o_ref):
    o = of_ref[...].astype(F32) + ob_ref[...].astype(F32)
    z = z_ref[...].astype(F32)
    og = og_ref[...]
    parts = []
    for pc in range(DN_V // LANES):
        os_ = o[:, pc * LANES:(pc + 1) * LANES]
        on = os_ * lax.rsqrt(_half_sums(os_ * os_) * (1.0 / DN_VALUE_DIM) + EPS) * og
        parts.append((on * _silu(z[:, pc * LANES:(pc + 1) * LANES])).astype(BF16))
    y_dn = jnp.concatenate(parts, axis=1)
    g = gate_ref[...].astype(F32)
    merged = (g[:, :D_MODEL] * _dot(ya_ref[...], wa_ref[...])
              + g[:, D_MODEL:2 * D_MODEL] * _dot(y_dn, wd_ref[...])
              + g[:, 2 * D_MODEL:] * _dot(ym_ref[...], wm_ref[...]))
    o_ref[...] = x_ref[...] + _dot(merged.astype(BF16), wo_ref[...])


def _merge(x2, ya, of, ob, z, ym, gates, og, wa, wd, wm, wo):
    n = x2.shape[0]
    tm = ROW_TILE
    row = lambda w_: pl.BlockSpec((tm, w_), lambda i: (i, 0))
    full = lambda a, b: _resident((a, b))
    return pl.pallas_call(
        _merge_kernel,
        grid=(n // tm,),
        in_specs=[row(D_MODEL), row(ATTN_Q), row(DN_V), row(DN_V), row(DN_V), row(MEM_Q), row(N_BRANCH * D_MODEL),
                  full(1, LANES), full(ATTN_Q, D_MODEL), full(DN_V, D_MODEL), full(MEM_Q, D_MODEL),
                  full(D_MODEL, D_MODEL)],
        out_specs=row(D_MODEL),
        out_shape=jax.ShapeDtypeStruct((n, D_MODEL), F32),
        compiler_params=_cparams(("parallel",)),
        name="merge",
    )(x2, ya, of, ob, z, ym, gates, og, wa, wd, wm, wo)


def _ffn_kernel(xp_ref, xc_ref, xn_ref, g_ref, wu_ref, cw_ref, cb_ref, wd_ref, o_ref, act_ref):
    i = pl.program_id(1)
    nt = pl.num_programs(1)
    tm = xc_ref.shape[1]
    xc = xc_ref[0]
    prev = jnp.where(i == 0, 0.0, xp_ref[0])
    nxt = jnp.where(i == nt - 1, 0.0, xn_ref[0])
    xe = jnp.concatenate([prev, xc, nxt], axis=0)
    ms = jnp.mean(xe * xe, axis=-1, keepdims=True)
    h = ((xe * lax.rsqrt(ms + EPS)) * g_ref[...]).astype(BF16)
    rows = tm + 2 * HALO

    def conv(u, c0):
        out = None
        for j in range(FFN_CONV):
            sh = (FFN_CONV // 2 - j) % rows
            us = u if sh == 0 else pltpu.roll(u, sh, axis=0)
            term = us[HALO:HALO + tm] * cw_ref[j:j + 1, c0:c0 + FF_CHUNK]
            out = term if out is None else out + term
        return out + cb_ref[:, c0:c0 + FF_CHUNK]

    for c in range(D_FF // FF_CHUNK):
        c0 = c * FF_CHUNK
        ug = conv(_dot(h, wu_ref[:, c0:c0 + FF_CHUNK]), c0)
        uv = conv(_dot(h, wu_ref[:, D_FF + c0:D_FF + c0 + FF_CHUNK]), D_FF + c0)
        act_ref[:, c0:c0 + FF_CHUNK] = (_silu(ug) * uv).astype(BF16)
    o_ref[0] = xc + _dot(act_ref[...], wd_ref[...])


def _ffn(x1, g, wu, cw, cb, wd):
    B, S, _ = x1.shape
    tm = FFN_ROW_TILE
    nt = S // tm
    hb = tm // HALO
    return pl.pallas_call(
        _ffn_kernel,
        grid=(B, nt),
        in_specs=[pl.BlockSpec((1, HALO, D_MODEL), lambda b, i: (b, jnp.maximum(i * hb - 1, 0), 0)),
                  pl.BlockSpec((1, tm, D_MODEL), lambda b, i: (b, i, 0)),
                  pl.BlockSpec((1, HALO, D_MODEL), lambda b, i: (b, jnp.minimum((i + 1) * hb, S // HALO - 1), 0)),
                  _resident((1, D_MODEL)), _resident((D_MODEL, 2 * D_FF)), _resident((FFN_CONV, 2 * D_FF)),
                  _resident((1, 2 * D_FF)), _resident((D_FF, D_MODEL))],
        out_specs=pl.BlockSpec((1, tm, D_MODEL), lambda b, i: (b, i, 0)),
        out_shape=jax.ShapeDtypeStruct((B, S, D_MODEL), F32),
        scratch_shapes=[pltpu.VMEM((tm, D_FF), BF16)],
        compiler_params=_cparams(("parallel", "parallel")),
        name="ffn",
    )(x1, x1, x1, g, wu, cw, cb, wd)


def _permute_w_in(w):
    idx = np.cumsum((0,) + IN_SPLITS)
    seg = lambda k: w[:, idx[k]:idx[k + 1]]
    aq, ak, av, dq, dk, dv, dz, db, da, mq, gates = (seg(k) for k in range(11))
    pad = jnp.zeros((w.shape[0], LANES - 4 * DN_HEADS), w.dtype)
    return jnp.concatenate([aq, ak, av, dq, dk, dv, dz, mq, gates, db, da, pad], axis=1).astype(BF16)


def _layer(x, mem, rel_bias_table, p):
    B, S, D = x.shape
    n = B * S
    x2 = x.reshape(n, D)
    row = lambda a: a.reshape(1, -1).astype(F32)
    tile2 = lambda a: jnp.tile(a.astype(F32), 2)[None]
    aq, akv, dqkv, dz, mq, gates, ba = _inproj(x2, row(p["norm_mix_g"]), _permute_w_in(p["w_in"]),
                                               tile2(p["attn_q_norm_g"]), tile2(p["attn_k_norm_g"]))
    y_attn = _attn(aq.reshape(B, S, -1), akv.reshape(B, S, -1), rel_bias_table, p["attn_sink"])
    qkvn, betax, gcx, gct = _dn_prep(dqkv.reshape(B, S, -1), ba.reshape(B, S, -1), p["dn_conv_w"],
                                     p["dn_a_log"], p["dn_dt_bias"])
    o_f, o_b = _dn_scan(*_dn_chunk(qkvn, betax, gcx, gct))
    mk, mv = _mem_kv(mem.reshape(B * N_MEM, D), row(p["mem_norm_g"]), p["mem_w_kv"].astype(BF16),
                     row(p["mem_k_norm_g"]))
    y_mem = _mem_attn(mq.reshape(B, S, -1), mk.reshape(B, N_MEM, -1), mv.reshape(B, N_MEM, -1),
                      row(p["mem_q_norm_g"]))
    og2 = jnp.tile(p["dn_out_norm_g"].astype(F32), 2)[None]
    x1 = _merge(x2, y_attn.reshape(n, -1), o_f.reshape(n, -1), o_b.reshape(n, -1), dz, y_mem.reshape(n, -1), gates,
                og2, p["w_br_attn"].astype(BF16), p["w_br_dn"].astype(BF16), p["w_br_mem"].astype(BF16),
                p["w_out"].astype(BF16))
    return _ffn(x1.reshape(B, S, D), row(p["norm_ffn_g"]), p["ffn_w_up"].astype(BF16), p["ffn_conv_w"].astype(F32),
                row(p["ffn_conv_b"]), p["ffn_w_down"].astype(BF16))


_LAYER_PARAMS = ("norm_mix_g", "w_in", "attn_q_norm_g", "attn_k_norm_g", "attn_sink", "dn_conv_w", "dn_a_log",
                 "dn_dt_bias", "dn_out_norm_g", "mem_norm_g", "mem_w_kv", "mem_q_norm_g", "mem_k_norm_g",
                 "w_br_attn", "w_br_dn", "w_br_mem", "w_out", "norm_ffn_g", "ffn_w_up", "ffn_conv_w", "ffn_conv_b",
                 "ffn_w_down")


def kernel(x, mem, rel_bias_table, norm_mix_g, w_in, attn_q_norm_g, attn_k_norm_g, attn_sink, dn_conv_w, dn_a_log,
           dn_dt_bias, dn_out_norm_g, mem_norm_g, mem_w_kv, mem_q_norm_g, mem_k_norm_g, w_br_attn, w_br_dn,
           w_br_mem, w_out, norm_ffn_g, ffn_w_up, ffn_conv_w, ffn_conv_b, ffn_w_down):
    stacked = dict(zip(_LAYER_PARAMS, (norm_mix_g, w_in, attn_q_norm_g, attn_k_norm_g, attn_sink, dn_conv_w,
                                       dn_a_log, dn_dt_bias, dn_out_norm_g, mem_norm_g, mem_w_kv, mem_q_norm_g,
                                       mem_k_norm_g, w_br_attn, w_br_dn, w_br_mem, w_out, norm_ffn_g, ffn_w_up,
                                       ffn_conv_w, ffn_conv_b, ffn_w_down)))
    depth = w_in.shape[0]
    for l in range(depth):
        x = _layer(x, mem, rel_bias_table, {k: v[l] for k, v in stacked.items()})
    return x
```

```python
import functools
import math

import numpy as np
import jax
import jax.numpy as jnp
from jax import lax
from jax.experimental import pallas as pl
from jax.experimental.pallas import tpu as pltpu

F32 = jnp.float32
BF16 = jnp.bfloat16

EPS = 1e-6
D_MODEL = 1024
N_MEM = 256
ATTN_HEADS = 8
ATTN_KV_HEADS = 2
ATTN_HEAD_DIM = 64
WINDOW = 128
ATTN_BLOCK = 128
REL_BUCKETS = 32
REL_MAX_DIST = 128
DN_HEADS = 8
DN_KEY_DIM = 64
DN_VALUE_DIM = 64
DN_CONV = 5
DN_CHUNK = 64
MEM_HEADS = 4
MEM_HEAD_DIM = 128
D_FF = 2816
FFN_CONV = 3
N_BRANCH = 3

ATTN_Q = ATTN_HEADS * ATTN_HEAD_DIM
ATTN_KV = ATTN_KV_HEADS * ATTN_HEAD_DIM
DN_QK = DN_HEADS * DN_KEY_DIM
DN_V = DN_HEADS * DN_VALUE_DIM
MEM_Q = MEM_HEADS * MEM_HEAD_DIM
IN_SPLITS = (ATTN_Q, ATTN_KV, ATTN_KV, DN_QK, DN_QK, DN_V, DN_V, 2 * DN_HEADS, 2 * DN_HEADS, MEM_Q,
             N_BRANCH * D_MODEL)

LANES = 128
HALF = 64
HALO = 8
NEG = -1e30
VMEM_LIMIT = 56 * 1024 * 1024

ROW_TILE = 512
FFN_ROW_TILE = 1024
FF_CHUNK = 256


LOG2E = math.log2(math.e)


def _cparams(sem):
    return pltpu.CompilerParams(dimension_semantics=sem, vmem_limit_bytes=VMEM_LIMIT)


def _resident(shape):
    zeros = (0,) * len(shape)
    return pl.BlockSpec(shape, lambda *_: zeros, pipeline_mode=pl.Buffered(1))


def _dot(a, b):
    return jnp.dot(a, b, preferred_element_type=F32)


def _dot_nt(a, b):
    return lax.dot_general(a, b, (((1,), (1,)), ((), ())), preferred_element_type=F32)


def _dot_tn(a, b):
    return lax.dot_general(a, b, (((0,), (0,)), ((), ())), preferred_element_type=F32)


def _lane_is_low(shape):
    lane = lax.broadcasted_iota(jnp.int32, shape, len(shape) - 1)
    return (lane % LANES) < HALF


def _half_sums(sq):
    low = _lane_is_low(sq.shape)
    s_lo = jnp.sum(jnp.where(low, sq, 0.0), axis=-1, keepdims=True)
    s_hi = jnp.sum(jnp.where(low, 0.0, sq), axis=-1, keepdims=True)
    return jnp.where(low, s_lo, s_hi)


def _silu(x):
    return x * (1.0 / (1.0 + jnp.exp(-x)))


def _sigmoid(x):
    return 1.0 / (1.0 + jnp.exp(-x))


_C_AQ = (0, 512)
_C_AKV = (512, 768)
_C_DQKV = (768, 2304)
_C_DZ = (2304, 2816)
_C_MQ = (2816, 3328)
_C_BA = (3328, 3456)
_N_IN = 3456


def _head_rmsnorm(t, gain2):
    return t * lax.rsqrt(_half_sums(t * t) * (1.0 / ATTN_HEAD_DIM) + EPS) * gain2


def _inproj_kernel(x_ref, g_ref, w_ref, qg_ref, kg_ref, aq_ref, akv_ref, dqkv_ref, dz_ref, mq_ref, ba_ref):
    x = x_ref[...]
    ms = jnp.mean(x * x, axis=-1, keepdims=True)
    h = ((x * lax.rsqrt(ms + EPS)) * g_ref[...]).astype(BF16)

    def proj(c):
        return _dot(h, w_ref[:, c[0]:c[1]])

    aq = proj(_C_AQ)
    q_scale = ATTN_HEAD_DIM ** -0.5 * LOG2E
    for pc in range(ATTN_Q // LANES):
        sl = slice(pc * LANES, (pc + 1) * LANES)
        aq_ref[:, sl] = (_head_rmsnorm(aq[:, sl], qg_ref[...]) * q_scale).astype(BF16)
    akv = proj(_C_AKV)
    kn = _head_rmsnorm(akv[:, :LANES], kg_ref[...])
    av = akv[:, LANES:]
    akv_ref[:, 0 * LANES:1 * LANES] = kn.astype(BF16)
    akv_ref[:, 1 * LANES:2 * LANES] = av.astype(BF16)
    akv_ref[:, 2 * LANES:3 * LANES] = pltpu.roll(kn, HALF, axis=1).astype(BF16)
    akv_ref[:, 3 * LANES:4 * LANES] = pltpu.roll(av, HALF, axis=1).astype(BF16)
    dqkv_ref[...] = proj(_C_DQKV).astype(BF16)
    dz_ref[...] = proj(_C_DZ).astype(BF16)
    mq_ref[...] = proj(_C_MQ).astype(BF16)
    ba_ref[...] = proj(_C_BA)


def _inproj(x2, g, w, qg2, kg2):
    n = x2.shape[0]
    tm = ROW_TILE
    widths = [(512, BF16), (512, BF16), (1536, BF16), (512, BF16), (512, BF16), (128, F32)]
    return pl.pallas_call(
        _inproj_kernel,
        grid=(n // tm,),
        in_specs=[pl.BlockSpec((tm, D_MODEL), lambda i: (i, 0)),
                  _resident((1, D_MODEL)), _resident((D_MODEL, _N_IN)), _resident((1, LANES)), _resident((1, LANES))],
        out_specs=[pl.BlockSpec((tm, w_), lambda i: (i, 0)) for w_, _ in widths],
        out_shape=[jax.ShapeDtypeStruct((n, w_), dt) for w_, dt in widths],
        compiler_params=_cparams(("parallel",)),
        name="inproj",
    )(x2, g, w, qg2, kg2)


def _t5_buckets(rel):
    nb = REL_BUCKETS // 2
    max_exact = nb // 2
    ret = (rel > 0).astype(np.int32) * nb
    n = np.abs(rel)
    large = max_exact + (np.log(np.maximum(n, 1) / max_exact) / np.log(REL_MAX_DIST / max_exact)
                         * (nb - max_exact)).astype(np.int32)
    large = np.minimum(large, nb - 1)
    return (ret + np.where(n < max_exact, n, large)).astype(np.int32)


_ATTN_GROUPS = ((0, 1, True, False), (0, 1, False, True), (2, 3, True, True), (2, 3, False, False))
_ATTN_GROUP_HEADS = ((0, 2), (1, 3), (4, 6), (5, 7))
ATTN_BLOCKS_PER_STEP = 2


def _attn_kernel(q_ref, kp_ref, kc_ref, kn_ref, bias_ref, sink_ref, o_ref):
    T = ATTN_BLOCK
    nq = ATTN_BLOCKS_PER_STEP
    first_blk = pl.program_id(1) * nq
    last_blk = pl.num_programs(1) * nq - 1
    kv_all = jnp.concatenate([kp_ref[0], kc_ref[0], kn_ref[0]], axis=0)
    ones = jnp.ones((3 * T, LANES), BF16)
    low_q = _lane_is_low((T, LANES))
    zero = jnp.zeros((T, LANES), BF16)
    for qb in range(nq):
        rq = slice(qb * T, (qb + 1) * T)
        kv = kv_all[qb * T:(qb + 3) * T]
        k_n = kv[:, 0 * LANES:1 * LANES]
        v_n = jnp.concatenate([kv[:, 1 * LANES:2 * LANES], ones], axis=1)
        k_s = kv[:, 2 * LANES:3 * LANES]
        v_s = jnp.concatenate([kv[:, 3 * LANES:4 * LANES], ones], axis=1)
        blk = first_blk + qb
        edge = jnp.where(blk == 0, 0, jnp.where(blk == last_blk, 2, 1))
        res = []
        for gi, (pa, pb, low, swapped) in enumerate(_ATTN_GROUPS):
            sel = low_q if low else jnp.logical_not(low_q)
            lhs = jnp.concatenate([jnp.where(sel, q_ref[0, rq, pa * LANES:(pa + 1) * LANES], zero),
                                   jnp.where(sel, q_ref[0, rq, pb * LANES:(pb + 1) * LANES], zero)], axis=0)
            s = _dot_nt(lhs, k_s if swapped else k_n) + bias_ref[edge, gi]
            sk = sink_ref[gi]
            m = jnp.maximum(jnp.max(s, axis=-1, keepdims=True), sk)
            p = jnp.exp2(s - m)
            r = _dot(p.astype(BF16), v_s if swapped else v_n)
            den = r[:, LANES:] + jnp.exp2(sk - m)
            res.append(r[:, :LANES] * (1.0 / den))
        for pc, (ge, go) in enumerate(((0, 1), (0, 1), (2, 3), (2, 3))):
            r0 = (pc % 2) * T
            out = jnp.where(low_q, res[ge][r0:r0 + T], res[go][r0:r0 + T])
            o_ref[0, rq, pc * LANES:(pc + 1) * LANES] = out.astype(BF16)


def _attn(aq, akv, rel_table, sink):
    B, S, _ = aq.shape
    T = ATTN_BLOCK
    nb = S // T
    assert nb >= 2
    t_idx = np.arange(T)[:, None]
    j_idx = np.arange(3 * T)[None, :]
    rel = j_idx - T - t_idx
    onehot = jnp.asarray(np.eye(REL_BUCKETS, dtype=np.float32)[_t5_buckets(rel)])
    bias = jnp.einsum("tjr,rh->htj", onehot, rel_table.astype(F32), precision=lax.Precision.HIGHEST) * LOG2E
    in_win = np.abs(rel) <= WINDOW
    edge_ok = np.stack([in_win & (j_idx >= T), in_win, in_win & (j_idx < 2 * T)])
    bias = jnp.where(jnp.asarray(edge_ok)[:, None], bias[None], NEG)
    bias_g = jnp.stack([jnp.concatenate([bias[:, a], bias[:, b]], axis=1) for a, b in _ATTN_GROUP_HEADS], axis=1)
    sk = sink.astype(F32) * LOG2E
    sink_g = jnp.stack([jnp.concatenate([jnp.full((T, 1), 1.0) * sk[a], jnp.full((T, 1), 1.0) * sk[b]], axis=0)
                        for a, b in _ATTN_GROUP_HEADS])
    kv_w = akv.shape[-1]
    nq = ATTN_BLOCKS_PER_STEP
    assert nb % nq == 0
    return pl.pallas_call(
        _attn_kernel,
        grid=(B, nb // nq),
        in_specs=[pl.BlockSpec((1, nq * T, ATTN_Q), lambda b, i: (b, i, 0)),
                  pl.BlockSpec((1, T, kv_w), lambda b, i: (b, jnp.maximum(i * nq - 1, 0), 0)),
                  pl.BlockSpec((1, nq * T, kv_w), lambda b, i: (b, i, 0)),
                  pl.BlockSpec((1, T, kv_w), lambda b, i: (b, jnp.minimum((i + 1) * nq, nb - 1), 0)),
                  _resident((3, 4, 2 * T, 3 * T)), _resident((4, 2 * T, 1))],
        out_specs=pl.BlockSpec((1, nq * T, ATTN_Q), lambda b, i: (b, i, 0)),
        out_shape=jax.ShapeDtypeStruct((B, S, ATTN_Q), BF16),
        compiler_params=_cparams(("parallel", "parallel")),
        name="attn",
    )(aq, akv, akv, akv, bias_g, sink_g)


def _split_hi_lo(x):
    hi = x.astype(BF16)
    lo = (x - hi.astype(F32)).astype(BF16)
    return hi, lo


def _dn_prep_kernel(xp_ref, xc_ref, xn_ref, ba_ref, cw_ref, alog_ref, dtb_ref, tri_ref, expand_ref,
                    qkv_ref, betax_ref, gcx_ref, gct_ref):
    i = pl.program_id(1)
    nt = pl.num_programs(1)
    tm = xc_ref.shape[1]
    prev = jnp.where(i == 0, 0.0, xp_ref[0].astype(F32))
    nxt = jnp.where(i == nt - 1, 0.0, xn_ref[0].astype(F32))
    xe = jnp.concatenate([prev, xc_ref[0].astype(F32), nxt], axis=0)
    rows = tm + 2 * HALO
    acc = None
    for j in range(DN_CONV):
        sh = (DN_CONV // 2 - j) % rows
        xs = xe if sh == 0 else pltpu.roll(xe, sh, axis=0)
        term = xs[HALO:HALO + tm] * cw_ref[j:j + 1, :]
        acc = term if acc is None else acc + term
    y = _silu(acc)
    for pc in range(2 * DN_QK // LANES):
        ys = y[:, pc * LANES:(pc + 1) * LANES]
        yn = ys * lax.rsqrt(_half_sums(ys * ys) + EPS)
        if pc < DN_QK // LANES:
            yn = yn * (DN_KEY_DIM ** -0.5)
        qkv_ref[0, :, pc * LANES:(pc + 1) * LANES] = yn.astype(BF16)
    qkv_ref[0, :, 2 * DN_QK:] = y[:, 2 * DN_QK:].astype(BF16)

    ba = ba_ref[0]
    nh2 = 2 * DN_HEADS
    beta = _sigmoid(ba[:, :nh2])
    z = ba[:, nh2:2 * nh2] + dtb_ref[...]
    sp = jnp.maximum(z, 0.0) + jnp.log1p(jnp.exp(-jnp.abs(z)))
    g = -jnp.exp(alog_ref[...]) * sp
    g_hi, g_lo = _split_hi_lo(g)
    gg = jnp.concatenate([g_hi, g_lo], axis=1)
    pre = _dot(tri_ref[0], gg)
    suf = _dot(tri_ref[1], gg)
    lane16 = lax.broadcasted_iota(jnp.int32, (tm, nh2), 1)
    gc = jnp.where(lane16 < DN_HEADS, pre[:, :nh2] + pre[:, nh2:], suf[:, :nh2] + suf[:, nh2:])
    b_hi, b_lo = _split_hi_lo(beta)
    c_hi, c_lo = _split_hi_lo(gc)
    c_lo2 = (gc - c_hi.astype(F32) - c_lo.astype(F32)).astype(BF16)
    ex = expand_ref[...]
    betax_ref[0] = _dot(b_hi, ex) + _dot(b_lo, ex)
    gcx_ref[0] = _dot(c_hi, ex) + _dot(c_lo, ex) + _dot(c_lo2, ex)
    gct = jnp.concatenate([gc, jnp.zeros((tm, LANES - nh2), F32)], axis=1).T
    for c in range(tm // DN_CHUNK):
        gct_ref[0, c] = gct[:nh2, c * DN_CHUNK:(c + 1) * DN_CHUNK]


def _dn_prep(dqkv, ba, conv_w, a_log, dt_bias):
    B, S, W = dqkv.shape
    tm = ROW_TILE
    nt = S // tm
    hb = tm // HALO
    r = np.arange(tm)
    same = (r[:, None] // DN_CHUNK) == (r[None, :] // DN_CHUNK)
    tri = np.stack([same & (r[:, None] >= r[None, :]), same & (r[:, None] <= r[None, :])]).astype(np.float32)
    expand = np.repeat(np.eye(2 * DN_HEADS, dtype=np.float32), HALF, axis=1)
    nh2 = 2 * DN_HEADS
    return pl.pallas_call(
        _dn_prep_kernel,
        grid=(B, nt),
        in_specs=[pl.BlockSpec((1, HALO, W), lambda b, i: (b, jnp.maximum(i * hb - 1, 0), 0)),
                  pl.BlockSpec((1, tm, W), lambda b, i: (b, i, 0)),
                  pl.BlockSpec((1, HALO, W), lambda b, i: (b, jnp.minimum((i + 1) * hb, S // HALO - 1), 0)),
                  pl.BlockSpec((1, tm, LANES), lambda b, i: (b, i, 0)),
                  pl.BlockSpec((DN_CONV, W), lambda b, i: (0, 0)),
                  pl.BlockSpec((1, nh2), lambda b, i: (0, 0)),
                  pl.BlockSpec((1, nh2), lambda b, i: (0, 0)),
                  pl.BlockSpec((2, tm, tm), lambda b, i: (0, 0, 0)),
                  pl.BlockSpec((nh2, nh2 * HALF), lambda b, i: (0, 0))],
        out_specs=[pl.BlockSpec((1, tm, W), lambda b, i: (b, i, 0)),
                   pl.BlockSpec((1, tm, nh2 * HALF), lambda b, i: (b, i, 0)),
                   pl.BlockSpec((1, tm, nh2 * HALF), lambda b, i: (b, i, 0)),
                   pl.BlockSpec((1, tm // DN_CHUNK, nh2, DN_CHUNK), lambda b, i: (b, i, 0, 0))],
        out_shape=[jax.ShapeDtypeStruct((B, S, W), BF16),
                   jax.ShapeDtypeStruct((B, S, nh2 * HALF), F32),
                   jax.ShapeDtypeStruct((B, S, nh2 * HALF), F32),
                   jax.ShapeDtypeStruct((B, S // DN_CHUNK, nh2, DN_CHUNK), F32)],
        compiler_params=_cparams(("parallel", "parallel")),
        name="dn_prep",
    )(dqkv, dqkv, dqkv, ba, conv_w.astype(F32), a_log.reshape(1, nh2).astype(F32),
      dt_bias.reshape(1, nh2).astype(F32), jnp.asarray(tri, BF16), jnp.asarray(expand, BF16))


def _block_diag(x2, low):
    zero = jnp.zeros_like(x2)
    return jnp.concatenate([jnp.where(low, x2, zero), jnp.where(low, zero, x2)], axis=0)


DN_PAIRS = DN_HEADS // 2
DN_CHUNKS_PER_STEP = 4
DN_SCAN_CHUNKS = 4


def _dn_chunk_kernel(qkv_ref, bx_ref, gx_ref, gt_ref, u_ref, w_ref, qkd_ref, qin_ref, kupt_ref, egt_ref):
    C = DN_CHUNK
    low = _lane_is_low((C, LANES))
    r = lax.broadcasted_iota(jnp.int32, (C, LANES), 0)
    m = lax.broadcasted_iota(jnp.int32, (C, LANES), 1) % HALF
    eye = jnp.where(r == m, 1.0, 0.0)
    probs = []
    for ch in range(DN_CHUNKS_PER_STEP):
        rows = slice(ch * C, (ch + 1) * C)
        for pr in range(DN_PAIRS):
            q2 = qkv_ref[0, rows, pr * LANES:(pr + 1) * LANES]
            k2 = qkv_ref[0, rows, DN_QK + pr * LANES:DN_QK + (pr + 1) * LANES]
            v2 = qkv_ref[0, rows, 2 * DN_QK + pr * LANES:2 * DN_QK + (pr + 1) * LANES]
            q2f, k2f, v2f = q2.astype(F32), k2.astype(F32), v2.astype(F32)
            qkk = _dot_nt(jnp.concatenate([q2, k2], axis=0), _block_diag(k2, low))
            qk, kk = qkk[:C], qkk[C:]
            for d in range(2):
                col = slice(d * DN_QK + pr * LANES, d * DN_QK + (pr + 1) * LANES)
                beta2 = bx_ref[0, rows, col]
                gcol2 = gx_ref[0, rows, col]
                hrow = d * DN_HEADS + 2 * pr
                grow2 = jnp.concatenate([gt_ref[0, ch, hrow:hrow + 1, :], gt_ref[0, ch, hrow + 1:hrow + 2, :]],
                                        axis=1)
                incl = (r >= m) if d == 0 else (r <= m)
                strict = (r > m) if d == 0 else (r < m)
                decay = jnp.where(incl, jnp.exp(jnp.where(incl, gcol2 - grow2, 0.0)), 0.0)
                gtot2 = gcol2[C - 1:C] if d == 0 else gcol2[0:1]
                e_col = jnp.exp(gcol2)
                p = jnp.where(strict, -(kk * beta2 * decay), 0.0)
                vb = (v2f * beta2).astype(BF16)
                kbe = (k2f * beta2 * e_col).astype(BF16)
                rhs = jnp.concatenate([_block_diag(vb, low), _block_diag(kbe, low)], axis=1)
                qkd_ref[0, rows, col] = (qk * decay).astype(BF16)
                qin_ref[0, rows, col] = (q2f * e_col).astype(BF16)
                kupt_ref[0, ch, d, pr] = (k2f * jnp.exp(gtot2 - gcol2)).T.astype(BF16)
                egt_ref[0, ch, d, pr:pr + 1, :] = jnp.exp(gtot2)
                probs.append([p, eye + p, rhs, rows, col])
    for pb in probs:
        pb[0] = _dot(pb[0].astype(BF16), _block_diag(pb[0].astype(BF16), low))
    for _ in range(4):
        for pb in probs:
            both = _dot(jnp.concatenate([pb[1], pb[0]], axis=0).astype(BF16),
                        _block_diag(pb[0].astype(BF16), low))
            pb[1] = pb[1] + both[:C]
            pb[0] = both[C:]
    for p, t, rhs, rows, col in probs:
        t = t + _dot(t.astype(BF16), _block_diag(p.astype(BF16), low))
        sol = _dot(t.astype(BF16), rhs)
        u_ref[0, rows, col] = sol[:, :LANES].astype(BF16)
        w_ref[0, rows, col] = sol[:, LANES:].astype(BF16)


def _dn_chunk(qkvn, betax, gcx, gct):
    B, S, W = qkvn.shape
    C = DN_CHUNK
    cps = DN_CHUNKS_PER_STEP
    nc = S // C
    wide = 2 * DN_QK
    blk = lambda w_: pl.BlockSpec((1, cps * C, w_), lambda b, i: (b, i, 0))
    return pl.pallas_call(
        _dn_chunk_kernel,
        grid=(B, nc // cps),
        in_specs=[blk(W), blk(wide), blk(wide),
                  pl.BlockSpec((1, cps, 2 * DN_HEADS, C), lambda b, i: (b, i, 0, 0))],
        out_specs=[blk(wide), blk(wide), blk(wide), blk(wide),
                   pl.BlockSpec((1, cps, 2, DN_PAIRS, LANES, C), lambda b, i: (b, i, 0, 0, 0, 0)),
                   pl.BlockSpec((1, cps, 2, DN_PAIRS, LANES), lambda b, i: (b, i, 0, 0, 0))],
        out_shape=[jax.ShapeDtypeStruct((B, S, wide), BF16)] * 4
        + [jax.ShapeDtypeStruct((B, nc, 2, DN_PAIRS, LANES, C), BF16),
           jax.ShapeDtypeStruct((B, nc, 2, DN_PAIRS, LANES), F32)],
        compiler_params=_cparams(("parallel", "parallel")),
        name="dn_chunk",
    )(qkvn, betax, gcx, gct)


def _dn_scan_kernel(uf, wf, qkdf, qinf, kuptf, egtf, ub, wb, qkdb, qinb, kuptb, egtb, of_ref, ob_ref, st_ref):
    @pl.when(pl.program_id(1) == 0)
    def _():
        st_ref[...] = jnp.zeros_like(st_ref)

    C = DN_CHUNK
    low = _lane_is_low((C, LANES))
    rr = lax.broadcasted_iota(jnp.int32, (LANES, LANES), 0) < HALF
    cc = lax.broadcasted_iota(jnp.int32, (LANES, LANES), 1) < HALF
    on_diag = rr == cc
    dirs = ((uf, wf, qkdf, qinf, kuptf, egtf, of_ref), (ub, wb, qkdb, qinb, kuptb, egtb, ob_ref))
    prob_ids = [(d, pr) for d in range(2) for pr in range(DN_PAIRS)]
    state = {dp: st_ref[dp[0], dp[1]] for dp in prob_ids}
    for step in range(DN_SCAN_CHUNKS):
        ws, vb = {}, {}
        for d, pr in prob_ids:
            u_ref, w_ref, qkd_ref, qin_ref, kupt_ref, egt_ref, o_ref = dirs[d]
            ch = step if d == 0 else DN_SCAN_CHUNKS - 1 - step
            rows = slice(ch * C, (ch + 1) * C)
            sl = slice(pr * LANES, (pr + 1) * LANES)
            lhs = jnp.concatenate([w_ref[0, rows, sl], qin_ref[0, rows, sl]], axis=0)
            ws[d, pr] = _dot(lhs, state[d, pr].astype(BF16))
        for d, pr in prob_ids:
            u_ref = dirs[d][0]
            ch = step if d == 0 else DN_SCAN_CHUNKS - 1 - step
            rows = slice(ch * C, (ch + 1) * C)
            sl = slice(pr * LANES, (pr + 1) * LANES)
            vb[d, pr] = (u_ref[0, rows, sl].astype(F32) - ws[d, pr][:C]).astype(BF16)
        for d, pr in prob_ids:
            u_ref, w_ref, qkd_ref, qin_ref, kupt_ref, egt_ref, o_ref = dirs[d]
            ch = step if d == 0 else DN_SCAN_CHUNKS - 1 - step
            rows = slice(ch * C, (ch + 1) * C)
            sl = slice(pr * LANES, (pr + 1) * LANES)
            upd = _dot(kupt_ref[0, ch, 0, pr], vb[d, pr])
            state[d, pr] = state[d, pr] * egt_ref[0, ch, 0, pr:pr + 1, :] + jnp.where(on_diag, upd, 0.0)
            o2 = ws[d, pr][C:] + _dot(qkd_ref[0, rows, sl], _block_diag(vb[d, pr], low))
            o_ref[0, rows, sl] = o2.astype(BF16)
    for d, pr in prob_ids:
        st_ref[d, pr] = state[d, pr]


def _dn_scan(u, w, qkd, qin, kupt, egt):
    B, S, _ = u.shape
    C = DN_CHUNK
    sc = DN_SCAN_CHUNKS
    nb = S // (C * sc)
    fwd3 = lambda b, i: (b, i, 0)
    bwd3 = lambda b, i: (b, nb - 1 - i, 1)
    row_f = pl.BlockSpec((1, sc * C, DN_QK), fwd3)
    row_b = pl.BlockSpec((1, sc * C, DN_QK), bwd3)
    kup_f = pl.BlockSpec((1, sc, 1, DN_PAIRS, LANES, C), lambda b, i: (b, i, 0, 0, 0, 0))
    kup_b = pl.BlockSpec((1, sc, 1, DN_PAIRS, LANES, C), lambda b, i: (b, nb - 1 - i, 1, 0, 0, 0))
    egt_f = pl.BlockSpec((1, sc, 1, DN_PAIRS, LANES), lambda b, i: (b, i, 0, 0, 0))
    egt_b = pl.BlockSpec((1, sc, 1, DN_PAIRS, LANES), lambda b, i: (b, nb - 1 - i, 1, 0, 0))
    return pl.pallas_call(
        _dn_scan_kernel,
        grid=(B, nb),
        in_specs=[row_f, row_f, row_f, row_f, kup_f, egt_f, row_b, row_b, row_b, row_b, kup_b, egt_b],
        out_specs=[pl.BlockSpec((1, sc * C, DN_V), fwd3),
                   pl.BlockSpec((1, sc * C, DN_V), lambda b, i: (b, nb - 1 - i, 0))],
        out_shape=[jax.ShapeDtypeStruct((B, S, DN_V), BF16), jax.ShapeDtypeStruct((B, S, DN_V), BF16)],
        scratch_shapes=[pltpu.VMEM((2, DN_PAIRS, LANES, LANES), F32)],
        compiler_params=_cparams(("parallel", "arbitrary")),
        name="dn_scan",
    )(u, w, qkd, qin, kupt, egt, u, w, qkd, qin, kupt, egt)


def _mem_kv_kernel(m_ref, g_ref, w_ref, kg_ref, k_ref, v_ref):
    x = m_ref[...]
    ms = jnp.mean(x * x, axis=-1, keepdims=True)
    h = ((x * lax.rsqrt(ms + EPS)) * g_ref[...]).astype(BF16)
    kv = _dot(h, w_ref[...])
    for hd in range(MEM_HEADS):
        kh = kv[:, hd * LANES:(hd + 1) * LANES]
        kms = jnp.mean(kh * kh, axis=-1, keepdims=True)
        k_ref[:, hd * LANES:(hd + 1) * LANES] = ((kh * lax.rsqrt(kms + EPS)) * kg_ref[...]).astype(BF16)
    v_ref[...] = kv[:, MEM_Q:].astype(BF16)


def _mem_kv(mem2, g, w_kv, kg):
    n = mem2.shape[0]
    return pl.pallas_call(
        _mem_kv_kernel,
        grid=(n // N_MEM,),
        in_specs=[pl.BlockSpec((N_MEM, D_MODEL), lambda i: (i, 0)),
                  pl.BlockSpec((1, D_MODEL), lambda i: (0, 0)),
                  pl.BlockSpec((D_MODEL, 2 * MEM_Q), lambda i: (0, 0)),
                  pl.BlockSpec((1, MEM_HEAD_DIM), lambda i: (0, 0))],
        out_specs=[pl.BlockSpec((N_MEM, MEM_Q), lambda i: (i, 0)), pl.BlockSpec((N_MEM, MEM_Q), lambda i: (i, 0))],
        out_shape=[jax.ShapeDtypeStruct((n, MEM_Q), BF16), jax.ShapeDtypeStruct((n, MEM_Q), BF16)],
        compiler_params=_cparams(("parallel",)),
        name="mem_kv",
    )(mem2, g, w_kv, kg)


def _mem_attn_kernel(q_ref, k_ref, v_ref, qg_ref, o_ref):
    scale = MEM_HEAD_DIM ** -0.5
    for hd in range(MEM_HEADS):
        sl = slice(hd * LANES, (hd + 1) * LANES)
        q = q_ref[0, :, sl].astype(F32)
        qms = jnp.mean(q * q, axis=-1, keepdims=True)
        qn = ((q * lax.rsqrt(qms + EPS)) * qg_ref[...]).astype(BF16)
        s = _dot_nt(qn, k_ref[0, :, sl]) * scale
        m = jnp.max(s, axis=-1, keepdims=True)
        p = jnp.exp(s - m)
        den = jnp.sum(p, axis=-1, keepdims=True)
        o = _dot(p.astype(BF16), v_ref[0, :, sl]) * (1.0 / den)
        o_ref[0, :, sl] = o.astype(BF16)


def _mem_attn(mq, k, v, qg):
    B, S, _ = mq.shape
    tm = ROW_TILE
    return pl.pallas_call(
        _mem_attn_kernel,
        grid=(B, S // tm),
        in_specs=[pl.BlockSpec((1, tm, MEM_Q), lambda b, i: (b, i, 0)),
                  pl.BlockSpec((1, N_MEM, MEM_Q), lambda b, i: (b, 0, 0)),
                  pl.BlockSpec((1, N_MEM, MEM_Q), lambda b, i: (b, 0, 0)),
                  pl.BlockSpec((1, MEM_HEAD_DIM), lambda b, i: (0, 0))],
        out_specs=pl.BlockSpec((1, tm, MEM_Q), lambda b, i: (b, i, 0)),
        out_shape=jax.ShapeDtypeStruct((B, S, MEM_Q), BF16),
        compiler_params=_cparams(("parallel", "parallel")),
        name="mem_attn",
    )(mq, k, v, qg)


def _merge_kernel(x_ref, ya_ref, of_ref, ob_ref, z_ref, ym_ref, ng_ref, wg_ref, og_ref, wa_ref, wd_ref, wm_ref,
                  wo_ref, o_ref):
    x = x_ref[...]
    ms = jnp.mean(x * x, axis=-1, keepdims=True)
    h = ((x * lax.rsqrt(ms + EPS)) * ng_ref[...]).astype(BF16)
    o = of_ref[...].astype(F32) + ob_ref[...].astype(F32)
    z = z_ref[...].astype(F32)
    og = og_ref[...]
    parts = []
    for pc in range(DN_V // LANES):
        os_ = o[:, pc * LANES:(pc + 1) * LANES]
        on = os_ * lax.rsqrt(_half_sums(os_ * os_) * (1.0 / DN_VALUE_DIM) + EPS) * og
        parts.append((on * _silu(z[:, pc * LANES:(pc + 1) * LANES])).astype(BF16))
    y_dn = jnp.concatenate(parts, axis=1)
    def gate(b):
        return _sigmoid(_dot(h, wg_ref[:, b * D_MODEL:(b + 1) * D_MODEL]))

    merged = (gate(0) * _dot(ya_ref[...], wa_ref[...])
              + gate(1) * _dot(y_dn, wd_ref[...])
              + gate(2) * _dot(ym_ref[...], wm_ref[...]))
    o_ref[...] = x + _dot(merged.astype(BF16), wo_ref[...])


def _merge(x2, ya, of, ob, z, ym, ng, wg, og, wa, wd, wm, wo):
    n = x2.shape[0]
    tm = ROW_TILE
    row = lambda w_: pl.BlockSpec((tm, w_), lambda i: (i, 0))
    full = lambda a, b: _resident((a, b))
    return pl.pallas_call(
        _merge_kernel,
        grid=(n // tm,),
        in_specs=[row(D_MODEL), row(ATTN_Q), row(DN_V), row(DN_V), row(DN_V), row(MEM_Q),
                  full(1, D_MODEL), full(D_MODEL, N_BRANCH * D_MODEL),
                  full(1, LANES), full(ATTN_Q, D_MODEL), full(DN_V, D_MODEL), full(MEM_Q, D_MODEL),
                  full(D_MODEL, D_MODEL)],
        out_specs=row(D_MODEL),
        out_shape=jax.ShapeDtypeStruct((n, D_MODEL), F32),
        compiler_params=_cparams(("parallel",)),
        name="merge",
    )(x2, ya, of, ob, z, ym, ng, wg, og, wa, wd, wm, wo)


def _ffn_kernel(xp_ref, xc_ref, xn_ref, g_ref, wu_ref, cw_ref, cb_ref, wd_ref, o_ref, act_ref):
    i = pl.program_id(1)
    nt = pl.num_programs(1)
    tm = xc_ref.shape[1]
    xc = xc_ref[0]
    prev = jnp.where(i == 0, 0.0, xp_ref[0])
    nxt = jnp.where(i == nt - 1, 0.0, xn_ref[0])
    xe = jnp.concatenate([prev, xc, nxt], axis=0)
    ms = jnp.mean(xe * xe, axis=-1, keepdims=True)
    h = ((xe * lax.rsqrt(ms + EPS)) * g_ref[...]).astype(BF16)
    rows = tm + 2 * HALO

    def conv(u, c0):
        out = None
        for j in range(FFN_CONV):
            sh = (FFN_CONV // 2 - j) % rows
            us = u if sh == 0 else pltpu.roll(u, sh, axis=0)
            term = us[HALO:HALO + tm] * cw_ref[j:j + 1, c0:c0 + FF_CHUNK]
            out = term if out is None else out + term
        return out + cb_ref[:, c0:c0 + FF_CHUNK]

    for c in range(D_FF // FF_CHUNK):
        c0 = c * FF_CHUNK
        ug = conv(_dot(h, wu_ref[:, c0:c0 + FF_CHUNK]), c0)
        uv = conv(_dot(h, wu_ref[:, D_FF + c0:D_FF + c0 + FF_CHUNK]), D_FF + c0)
        act_ref[:, c0:c0 + FF_CHUNK] = (_silu(ug) * uv).astype(BF16)
    o_ref[0] = xc + _dot(act_ref[...], wd_ref[...])


def _ffn(x1, g, wu, cw, cb, wd):
    B, S, _ = x1.shape
    tm = FFN_ROW_TILE
    nt = S // tm
    hb = tm // HALO
    return pl.pallas_call(
        _ffn_kernel,
        grid=(B, nt),
        in_specs=[pl.BlockSpec((1, HALO, D_MODEL), lambda b, i: (b, jnp.maximum(i * hb - 1, 0), 0)),
                  pl.BlockSpec((1, tm, D_MODEL), lambda b, i: (b, i, 0)),
                  pl.BlockSpec((1, HALO, D_MODEL), lambda b, i: (b, jnp.minimum((i + 1) * hb, S // HALO - 1), 0)),
                  _resident((1, D_MODEL)), _resident((D_MODEL, 2 * D_FF)), _resident((FFN_CONV, 2 * D_FF)),
                  _resident((1, 2 * D_FF)), _resident((D_FF, D_MODEL))],
        out_specs=pl.BlockSpec((1, tm, D_MODEL), lambda b, i: (b, i, 0)),
        out_shape=jax.ShapeDtypeStruct((B, S, D_MODEL), F32),
        scratch_shapes=[pltpu.VMEM((tm, D_FF), BF16)],
        compiler_params=_cparams(("parallel", "parallel")),
        name="ffn",
    )(x1, x1, x1, g, wu, cw, cb, wd)


def _permute_w_in(w):
    idx = np.cumsum((0,) + IN_SPLITS)
    wb = w.astype(BF16)
    pad = jnp.zeros((w.shape[0], LANES - 4 * DN_HEADS), BF16)
    main = jnp.concatenate([wb[:, :idx[7]], wb[:, idx[9]:idx[10]], wb[:, idx[7]:idx[9]], pad], axis=1)
    return main, wb[:, idx[10]:]


def _layer(x, mem, rel_bias_table, p):
    B, S, D = x.shape
    n = B * S
    x2 = x.reshape(n, D)
    row = lambda a: a.reshape(1, -1).astype(F32)
    tile2 = lambda a: jnp.tile(a.astype(F32), 2)[None]
    w_main, w_gate = _permute_w_in(p["w_in"])
    aq, akv, dqkv, dz, mq, ba = _inproj(x2, row(p["norm_mix_g"]), w_main,
                                        tile2(p["attn_q_norm_g"]), tile2(p["attn_k_norm_g"]))
    y_attn = _attn(aq.reshape(B, S, -1), akv.reshape(B, S, -1), rel_bias_table, p["attn_sink"])
    qkvn, betax, gcx, gct = _dn_prep(dqkv.reshape(B, S, -1), ba.reshape(B, S, -1), p["dn_conv_w"],
                                     p["dn_a_log"], p["dn_dt_bias"])
    o_f, o_b = _dn_scan(*_dn_chunk(qkvn, betax, gcx, gct))
    mk, mv = _mem_kv(mem.reshape(B * N_MEM, D), row(p["mem_norm_g"]), p["mem_w_kv"].astype(BF16),
                     row(p["mem_k_norm_g"]))
    y_mem = _mem_attn(mq.reshape(B, S, -1), mk.reshape(B, N_MEM, -1), mv.reshape(B, N_MEM, -1),
                      row(p["mem_q_norm_g"]))
    og2 = jnp.tile(p["dn_out_norm_g"].astype(F32), 2)[None]
    x1 = _merge(x2, y_attn.reshape(n, -1), o_f.reshape(n, -1), o_b.reshape(n, -1), dz, y_mem.reshape(n, -1),
                row(p["norm_mix_g"]), w_gate, og2, p["w_br_attn"].astype(BF16), p["w_br_dn"].astype(BF16), p["w_br_mem"].astype(BF16),
                p["w_out"].astype(BF16))
    return _ffn(x1.reshape(B, S, D), row(p["norm_ffn_g"]), p["ffn_w_up"].astype(BF16), p["ffn_conv_w"].astype(F32),
                row(p["ffn_conv_b"]), p["ffn_w_down"].astype(BF16))


_LAYER_PARAMS = ("norm_mix_g", "w_in", "attn_q_norm_g", "attn_k_norm_g", "attn_sink", "dn_conv_w", "dn_a_log",
                 "dn_dt_bias", "dn_out_norm_g", "mem_norm_g", "mem_w_kv", "mem_q_norm_g", "mem_k_norm_g",
                 "w_br_attn", "w_br_dn", "w_br_mem", "w_out", "norm_ffn_g", "ffn_w_up", "ffn_conv_w", "ffn_conv_b",
                 "ffn_w_down")


def kernel(x, mem, rel_bias_table, norm_mix_g, w_in, attn_q_norm_g, attn_k_norm_g, attn_sink, dn_conv_w, dn_a_log,
           dn_dt_bias, dn_out_norm_g, mem_norm_g, mem_w_kv, mem_q_norm_g, mem_k_norm_g, w_br_attn, w_br_dn,
           w_br_mem, w_out, norm_ffn_g, ffn_w_up, ffn_conv_w, ffn_conv_b, ffn_w_down):
    stacked = dict(zip(_LAYER_PARAMS, (norm_mix_g, w_in, attn_q_norm_g, attn_k_norm_g, attn_sink, dn_conv_w,
                                       dn_a_log, dn_dt_bias, dn_out_norm_g, mem_norm_g, mem_w_kv, mem_q_norm_g,
                                       mem_k_norm_g, w_br_attn, w_br_dn, w_br_mem, w_out, norm_ffn_g, ffn_w_up,
                                       ffn_conv_w, ffn_conv_b, ffn_w_down)))
    depth = w_in.shape[0]
    for l in range(depth):
        x = _layer(x, mem, rel_bias_table, {k: v[l] for k, v in stacked.items()})
    return x
```

```python
import functools
import math

import numpy as np
import jax
import jax.numpy as jnp
from jax import lax
from jax.experimental import pallas as pl
from jax.experimental.pallas import tpu as pltpu

F32 = jnp.float32
BF16 = jnp.bfloat16

EPS = 1e-6
D_MODEL = 1024
N_MEM = 256
ATTN_HEADS = 8
ATTN_KV_HEADS = 2
ATTN_HEAD_DIM = 64
WINDOW = 128
ATTN_BLOCK = 128
REL_BUCKETS = 32
REL_MAX_DIST = 128
DN_HEADS = 8
DN_KEY_DIM = 64
DN_VALUE_DIM = 64
DN_CONV = 5
DN_CHUNK = 64
MEM_HEADS = 4
MEM_HEAD_DIM = 128
D_FF = 2816
FFN_CONV = 3
N_BRANCH = 3

ATTN_Q = ATTN_HEADS * ATTN_HEAD_DIM
ATTN_KV = ATTN_KV_HEADS * ATTN_HEAD_DIM
DN_QK = DN_HEADS * DN_KEY_DIM
DN_V = DN_HEADS * DN_VALUE_DIM
MEM_Q = MEM_HEADS * MEM_HEAD_DIM
IN_SPLITS = (ATTN_Q, ATTN_KV, ATTN_KV, DN_QK, DN_QK, DN_V, DN_V, 2 * DN_HEADS, 2 * DN_HEADS, MEM_Q,
             N_BRANCH * D_MODEL)

LANES = 128
HALF = 64
HALO = 8
NEG = -1e30
VMEM_LIMIT = 56 * 1024 * 1024

ROW_TILE = 512
FFN_ROW_TILE = 1024
FF_CHUNK = 256


LOG2E = math.log2(math.e)


def _cparams(sem):
    return pltpu.CompilerParams(dimension_semantics=sem, vmem_limit_bytes=VMEM_LIMIT)


def _resident(shape):
    zeros = (0,) * len(shape)
    return pl.BlockSpec(shape, lambda *_: zeros, pipeline_mode=pl.Buffered(1))


def _dot(a, b):
    return jnp.dot(a, b, preferred_element_type=F32)


def _dot_nt(a, b):
    return lax.dot_general(a, b, (((1,), (1,)), ((), ())), preferred_element_type=F32)


def _dot_tn(a, b):
    return lax.dot_general(a, b, (((0,), (0,)), ((), ())), preferred_element_type=F32)


def _lane_is_low(shape):
    lane = lax.broadcasted_iota(jnp.int32, shape, len(shape) - 1)
    return (lane % LANES) < HALF


def _half_sums(sq):
    low = _lane_is_low(sq.shape)
    s_lo = jnp.sum(jnp.where(low, sq, 0.0), axis=-1, keepdims=True)
    s_hi = jnp.sum(jnp.where(low, 0.0, sq), axis=-1, keepdims=True)
    return jnp.where(low, s_lo, s_hi)


def _silu(x):
    return x * (1.0 / (1.0 + jnp.exp(-x)))


def _sigmoid(x):
    return 1.0 / (1.0 + jnp.exp(-x))


_C_AQ = (0, 512)
_C_AKV = (512, 768)
_C_DQKV = (768, 2304)
_C_DZ = (2304, 2816)
_C_MQ = (2816, 3328)
_C_BA = (3328, 3456)
_N_IN = 3456


def _head_rmsnorm(t, gain2):
    return t * lax.rsqrt(_half_sums(t * t) * (1.0 / ATTN_HEAD_DIM) + EPS) * gain2


def _split_hi_lo(x):
    hi = x.astype(BF16)
    lo = (x - hi.astype(F32)).astype(BF16)
    return hi, lo


def _inproj_kernel(xp_ref, xc_ref, xn_ref, g_ref, w_ref, qg_ref, kg_ref, cw_ref, alog_ref, dtb_ref, tri_ref,
                   expand_ref, aq_ref, akv_ref, dz_ref, mq_ref, qkv_ref, betax_ref, gcx_ref, gct_ref):
    i = pl.program_id(1)
    nt = pl.num_programs(1)
    tm = xc_ref.shape[1]
    prev = jnp.where(i == 0, 0.0, xp_ref[0])
    nxt = jnp.where(i == nt - 1, 0.0, xn_ref[0])
    xe = jnp.concatenate([prev, xc_ref[0], nxt], axis=0)
    ms = jnp.mean(xe * xe, axis=-1, keepdims=True)
    he = ((xe * lax.rsqrt(ms + EPS)) * g_ref[...]).astype(BF16)
    h = he[HALO:HALO + tm]

    def proj(c):
        return _dot(h, w_ref[:, c[0]:c[1]])

    ue = _dot(he, w_ref[:, _C_DQKV[0]:_C_DQKV[1]])
    rows = tm + 2 * HALO
    acc = None
    for j in range(DN_CONV):
        sh = (DN_CONV // 2 - j) % rows
        us = ue if sh == 0 else pltpu.roll(ue, sh, axis=0)
        term = us[HALO:HALO + tm] * cw_ref[j:j + 1, :]
        acc = term if acc is None else acc + term
    y = _silu(acc)
    for pc in range(2 * DN_QK // LANES):
        ys = y[:, pc * LANES:(pc + 1) * LANES]
        yn = ys * lax.rsqrt(_half_sums(ys * ys) + EPS)
        if pc < DN_QK // LANES:
            yn = yn * (DN_KEY_DIM ** -0.5)
        qkv_ref[0, :, pc * LANES:(pc + 1) * LANES] = yn.astype(BF16)
    qkv_ref[0, :, 2 * DN_QK:] = y[:, 2 * DN_QK:].astype(BF16)

    ba = proj(_C_BA)
    nh2 = 2 * DN_HEADS
    beta = _sigmoid(ba[:, :nh2])
    z = ba[:, nh2:2 * nh2] + dtb_ref[...]
    sp = jnp.maximum(z, 0.0) + jnp.log1p(jnp.exp(-jnp.abs(z)))
    g = -jnp.exp(alog_ref[...]) * sp
    g_hi, g_lo = _split_hi_lo(g)
    gg = jnp.concatenate([g_hi, g_lo], axis=1)
    pre = _dot(tri_ref[0], gg)
    suf = _dot(tri_ref[1], gg)
    lane16 = lax.broadcasted_iota(jnp.int32, (tm, nh2), 1)
    gc = jnp.where(lane16 < DN_HEADS, pre[:, :nh2] + pre[:, nh2:], suf[:, :nh2] + suf[:, nh2:])
    b_hi, b_lo = _split_hi_lo(beta)
    c_hi, c_lo = _split_hi_lo(gc)
    c_lo2 = (gc - c_hi.astype(F32) - c_lo.astype(F32)).astype(BF16)
    ex = expand_ref[...]
    betax_ref[0] = _dot(b_hi, ex) + _dot(b_lo, ex)
    gcx_ref[0] = _dot(c_hi, ex) + _dot(c_lo, ex) + _dot(c_lo2, ex)
    gct = jnp.concatenate([gc, jnp.zeros((tm, LANES - nh2), F32)], axis=1).T
    for c in range(tm // DN_CHUNK):
        gct_ref[0, c] = gct[:nh2, c * DN_CHUNK:(c + 1) * DN_CHUNK]

    aq = proj(_C_AQ)
    q_scale = ATTN_HEAD_DIM ** -0.5 * LOG2E
    for pc in range(ATTN_Q // LANES):
        sl = slice(pc * LANES, (pc + 1) * LANES)
        aq_ref[0, :, sl] = (_head_rmsnorm(aq[:, sl], qg_ref[...]) * q_scale).astype(BF16)
    akv = proj(_C_AKV)
    kn = _head_rmsnorm(akv[:, :LANES], kg_ref[...])
    av = akv[:, LANES:]
    akv_ref[0, :, 0 * LANES:1 * LANES] = kn.astype(BF16)
    akv_ref[0, :, 1 * LANES:2 * LANES] = av.astype(BF16)
    akv_ref[0, :, 2 * LANES:3 * LANES] = pltpu.roll(kn, HALF, axis=1).astype(BF16)
    akv_ref[0, :, 3 * LANES:4 * LANES] = pltpu.roll(av, HALF, axis=1).astype(BF16)
    dz_ref[0] = proj(_C_DZ).astype(BF16)
    mq_ref[0] = proj(_C_MQ).astype(BF16)


def _inproj(x, g, w, qg2, kg2, conv_w, a_log, dt_bias):
    B, S, D = x.shape
    tm = ROW_TILE
    nt = S // tm
    hb = tm // HALO
    nh2 = 2 * DN_HEADS
    r = np.arange(tm)
    same = (r[:, None] // DN_CHUNK) == (r[None, :] // DN_CHUNK)
    tri = np.stack([same & (r[:, None] >= r[None, :]), same & (r[:, None] <= r[None, :])]).astype(np.float32)
    expand = np.repeat(np.eye(nh2, dtype=np.float32), HALF, axis=1)
    blk = lambda w_: pl.BlockSpec((1, tm, w_), lambda b, i: (b, i, 0))
    outs = [(ATTN_Q, BF16), (4 * LANES, BF16), (DN_V, BF16), (MEM_Q, BF16), (2 * DN_QK + DN_V, BF16),
            (nh2 * HALF, F32), (nh2 * HALF, F32)]
    return pl.pallas_call(
        _inproj_kernel,
        grid=(B, nt),
        in_specs=[pl.BlockSpec((1, HALO, D), lambda b, i: (b, jnp.maximum(i * hb - 1, 0), 0)),
                  blk(D),
                  pl.BlockSpec((1, HALO, D), lambda b, i: (b, jnp.minimum((i + 1) * hb, S // HALO - 1), 0)),
                  _resident((1, D)), _resident((D, _N_IN)), _resident((1, LANES)), _resident((1, LANES)),
                  _resident((DN_CONV, 2 * DN_QK + DN_V)), _resident((1, nh2)), _resident((1, nh2)),
                  _resident((2, tm, tm)), _resident((nh2, nh2 * HALF))],
        out_specs=[blk(w_) for w_, _ in outs]
        + [pl.BlockSpec((1, tm // DN_CHUNK, nh2, DN_CHUNK), lambda b, i: (b, i, 0, 0))],
        out_shape=[jax.ShapeDtypeStruct((B, S, w_), dt) for w_, dt in outs]
        + [jax.ShapeDtypeStruct((B, S // DN_CHUNK, nh2, DN_CHUNK), F32)],
        compiler_params=_cparams(("parallel", "parallel")),
        name="inproj",
    )(x, x, x, g, w, qg2, kg2, conv_w.astype(F32), a_log.reshape(1, nh2).astype(F32),
      dt_bias.reshape(1, nh2).astype(F32), jnp.asarray(tri, BF16), jnp.asarray(expand, BF16))


def _t5_buckets(rel):
    nb = REL_BUCKETS // 2
    max_exact = nb // 2
    ret = (rel > 0).astype(np.int32) * nb
    n = np.abs(rel)
    large = max_exact + (np.log(np.maximum(n, 1) / max_exact) / np.log(REL_MAX_DIST / max_exact)
                         * (nb - max_exact)).astype(np.int32)
    large = np.minimum(large, nb - 1)
    return (ret + np.where(n < max_exact, n, large)).astype(np.int32)


_ATTN_GROUPS = ((0, 1, True, False), (0, 1, False, True), (2, 3, True, True), (2, 3, False, False))
_ATTN_GROUP_HEADS = ((0, 2), (1, 3), (4, 6), (5, 7))
ATTN_BLOCKS_PER_STEP = 2


def _attn_kernel(q_ref, kp_ref, kc_ref, kn_ref, bias_ref, sink_ref, o_ref):
    T = ATTN_BLOCK
    nq = ATTN_BLOCKS_PER_STEP
    first_blk = pl.program_id(1) * nq
    last_blk = pl.num_programs(1) * nq - 1
    kv_all = jnp.concatenate([kp_ref[0], kc_ref[0], kn_ref[0]], axis=0)
    ones = jnp.ones((3 * T, LANES), BF16)
    low_q = _lane_is_low((T, LANES))
    zero = jnp.zeros((T, LANES), BF16)
    for qb in range(nq):
        rq = slice(qb * T, (qb + 1) * T)
        kv = kv_all[qb * T:(qb + 3) * T]
        k_n = kv[:, 0 * LANES:1 * LANES]
        v_n = jnp.concatenate([kv[:, 1 * LANES:2 * LANES], ones], axis=1)
        k_s = kv[:, 2 * LANES:3 * LANES]
        v_s = jnp.concatenate([kv[:, 3 * LANES:4 * LANES], ones], axis=1)
        blk = first_blk + qb
        edge = jnp.where(blk == 0, 0, jnp.where(blk == last_blk, 2, 1))
        res = []
        for gi, (pa, pb, low, swapped) in enumerate(_ATTN_GROUPS):
            sel = low_q if low else jnp.logical_not(low_q)
            lhs = jnp.concatenate([jnp.where(sel, q_ref[0, rq, pa * LANES:(pa + 1) * LANES], zero),
                                   jnp.where(sel, q_ref[0, rq, pb * LANES:(pb + 1) * LANES], zero)], axis=0)
            s = _dot_nt(lhs, k_s if swapped else k_n) + bias_ref[edge, gi]
            sk = sink_ref[gi]
            m = jnp.maximum(jnp.max(s, axis=-1, keepdims=True), sk)
            p = jnp.exp2(s - m)
            r = _dot(p.astype(BF16), v_s if swapped else v_n)
            den = r[:, LANES:] + jnp.exp2(sk - m)
            res.append(r[:, :LANES] * (1.0 / den))
        for pc, (ge, go) in enumerate(((0, 1), (0, 1), (2, 3), (2, 3))):
            r0 = (pc % 2) * T
            out = jnp.where(low_q, res[ge][r0:r0 + T], res[go][r0:r0 + T])
            o_ref[0, rq, pc * LANES:(pc + 1) * LANES] = out.astype(BF16)


def _attn(aq, akv, rel_table, sink):
    B, S, _ = aq.shape
    T = ATTN_BLOCK
    nb = S // T
    assert nb >= 2
    t_idx = np.arange(T)[:, None]
    j_idx = np.arange(3 * T)[None, :]
    rel = j_idx - T - t_idx
    onehot = jnp.asarray(np.eye(REL_BUCKETS, dtype=np.float32)[_t5_buckets(rel)])
    bias = jnp.einsum("tjr,rh->htj", onehot, rel_table.astype(F32), precision=lax.Precision.HIGHEST) * LOG2E
    in_win = np.abs(rel) <= WINDOW
    edge_ok = np.stack([in_win & (j_idx >= T), in_win, in_win & (j_idx < 2 * T)])
    bias = jnp.where(jnp.asarray(edge_ok)[:, None], bias[None], NEG)
    bias_g = jnp.stack([jnp.concatenate([bias[:, a], bias[:, b]], axis=1) for a, b in _ATTN_GROUP_HEADS], axis=1)
    sk = sink.astype(F32) * LOG2E
    sink_g = jnp.stack([jnp.concatenate([jnp.full((T, 1), 1.0) * sk[a], jnp.full((T, 1), 1.0) * sk[b]], axis=0)
                        for a, b in _ATTN_GROUP_HEADS])
    kv_w = akv.shape[-1]
    nq = ATTN_BLOCKS_PER_STEP
    assert nb % nq == 0
    return pl.pallas_call(
        _attn_kernel,
        grid=(B, nb // nq),
        in_specs=[pl.BlockSpec((1, nq * T, ATTN_Q), lambda b, i: (b, i, 0)),
                  pl.BlockSpec((1, T, kv_w), lambda b, i: (b, jnp.maximum(i * nq - 1, 0), 0)),
                  pl.BlockSpec((1, nq * T, kv_w), lambda b, i: (b, i, 0)),
                  pl.BlockSpec((1, T, kv_w), lambda b, i: (b, jnp.minimum((i + 1) * nq, nb - 1), 0)),
                  _resident((3, 4, 2 * T, 3 * T)), _resident((4, 2 * T, 1))],
        out_specs=pl.BlockSpec((1, nq * T, ATTN_Q), lambda b, i: (b, i, 0)),
        out_shape=jax.ShapeDtypeStruct((B, S, ATTN_Q), BF16),
        compiler_params=_cparams(("parallel", "parallel")),
        name="attn",
    )(aq, akv, akv, akv, bias_g, sink_g)


def _block_diag(x2, low):
    zero = jnp.zeros_like(x2)
    return jnp.concatenate([jnp.where(low, x2, zero), jnp.where(low, zero, x2)], axis=0)


DN_PAIRS = DN_HEADS // 2
DN_CHUNKS_PER_STEP = 4
DN_SCAN_CHUNKS = 8


def _dn_chunk_kernel(qkv_ref, bx_ref, gx_ref, gt_ref, u_ref, w_ref, qkd_ref, qin_ref, kupt_ref, egt_ref):
    C = DN_CHUNK
    low = _lane_is_low((C, LANES))
    r = lax.broadcasted_iota(jnp.int32, (C, LANES), 0)
    m = lax.broadcasted_iota(jnp.int32, (C, LANES), 1) % HALF
    eye = jnp.where(r == m, 1.0, 0.0)
    probs = []
    for ch in range(DN_CHUNKS_PER_STEP):
        rows = slice(ch * C, (ch + 1) * C)
        for pr in range(DN_PAIRS):
            q2 = qkv_ref[0, rows, pr * LANES:(pr + 1) * LANES]
            k2 = qkv_ref[0, rows, DN_QK + pr * LANES:DN_QK + (pr + 1) * LANES]
            v2 = qkv_ref[0, rows, 2 * DN_QK + pr * LANES:2 * DN_QK + (pr + 1) * LANES]
            q2f, k2f, v2f = q2.astype(F32), k2.astype(F32), v2.astype(F32)
            qkk = _dot_nt(jnp.concatenate([q2, k2], axis=0), _block_diag(k2, low))
            qk, kk = qkk[:C], qkk[C:]
            for d in range(2):
                col = slice(d * DN_QK + pr * LANES, d * DN_QK + (pr + 1) * LANES)
                beta2 = bx_ref[0, rows, col]
                gcol2 = gx_ref[0, rows, col]
                hrow = d * DN_HEADS + 2 * pr
                grow2 = jnp.concatenate([gt_ref[0, ch, hrow:hrow + 1, :], gt_ref[0, ch, hrow + 1:hrow + 2, :]],
                                        axis=1)
                incl = (r >= m) if d == 0 else (r <= m)
                strict = (r > m) if d == 0 else (r < m)
                decay = jnp.where(incl, jnp.exp(jnp.where(incl, gcol2 - grow2, 0.0)), 0.0)
                gtot2 = gcol2[C - 1:C] if d == 0 else gcol2[0:1]
                e_col = jnp.exp(gcol2)
                p = jnp.where(strict, -(kk * beta2 * decay), 0.0)
                vb = (v2f * beta2).astype(BF16)
                kbe = (k2f * beta2 * e_col).astype(BF16)
                rhs = jnp.concatenate([_block_diag(vb, low), _block_diag(kbe, low)], axis=1)
                qkd_ref[0, rows, col] = (qk * decay).astype(BF16)
                qin_ref[0, rows, col] = (q2f * e_col).astype(BF16)
                kupt_ref[0, ch, d, pr] = (k2f * jnp.exp(gtot2 - gcol2)).T.astype(BF16)
                egt_ref[0, ch, d, pr:pr + 1, :] = jnp.exp(gtot2)
                probs.append([p, eye + p, rhs, rows, col])
    for pb in probs:
        pb[0] = _dot(pb[0].astype(BF16), _block_diag(pb[0].astype(BF16), low))
    for _ in range(4):
        for pb in probs:
            both = _dot(jnp.concatenate([pb[1], pb[0]], axis=0).astype(BF16),
                        _block_diag(pb[0].astype(BF16), low))
            pb[1] = pb[1] + both[:C]
            pb[0] = both[C:]
    for p, t, rhs, rows, col in probs:
        t = t + _dot(t.astype(BF16), _block_diag(p.astype(BF16), low))
        sol = _dot(t.astype(BF16), rhs)
        u_ref[0, rows, col] = sol[:, :LANES].astype(BF16)
        w_ref[0, rows, col] = sol[:, LANES:].astype(BF16)


def _dn_chunk(qkvn, betax, gcx, gct):
    B, S, W = qkvn.shape
    C = DN_CHUNK
    cps = DN_CHUNKS_PER_STEP
    nc = S // C
    wide = 2 * DN_QK
    blk = lambda w_: pl.BlockSpec((1, cps * C, w_), lambda b, i: (b, i, 0))
    return pl.pallas_call(
        _dn_chunk_kernel,
        grid=(B, nc // cps),
        in_specs=[blk(W), blk(wide), blk(wide),
                  pl.BlockSpec((1, cps, 2 * DN_HEADS, C), lambda b, i: (b, i, 0, 0))],
        out_specs=[blk(wide), blk(wide), blk(wide), blk(wide),
                   pl.BlockSpec((1, cps, 2, DN_PAIRS, LANES, C), lambda b, i: (b, i, 0, 0, 0, 0)),
                   pl.BlockSpec((1, cps, 2, DN_PAIRS, LANES), lambda b, i: (b, i, 0, 0, 0))],
        out_shape=[jax.ShapeDtypeStruct((B, S, wide), BF16)] * 4
        + [jax.ShapeDtypeStruct((B, nc, 2, DN_PAIRS, LANES, C), BF16),
           jax.ShapeDtypeStruct((B, nc, 2, DN_PAIRS, LANES), F32)],
        compiler_params=_cparams(("parallel", "parallel")),
        name="dn_chunk",
    )(qkvn, betax, gcx, gct)


def _dn_scan_kernel(uf, wf, qkdf, qinf, kuptf, egtf, ub, wb, qkdb, qinb, kuptb, egtb, of_ref, ob_ref, st_ref):
    @pl.when(pl.program_id(1) == 0)
    def _():
        st_ref[...] = jnp.zeros_like(st_ref)

    C = DN_CHUNK
    low = _lane_is_low((C, LANES))
    rr = lax.broadcasted_iota(jnp.int32, (LANES, LANES), 0) < HALF
    cc = lax.broadcasted_iota(jnp.int32, (LANES, LANES), 1) < HALF
    on_diag = rr == cc
    dirs = ((uf, wf, qkdf, qinf, kuptf, egtf, of_ref), (ub, wb, qkdb, qinb, kuptb, egtb, ob_ref))
    prob_ids = [(d, pr) for d in range(2) for pr in range(DN_PAIRS)]
    state = {dp: st_ref[dp[0], dp[1]] for dp in prob_ids}
    for step in range(DN_SCAN_CHUNKS):
        ws, vb = {}, {}
        for d, pr in prob_ids:
            u_ref, w_ref, qkd_ref, qin_ref, kupt_ref, egt_ref, o_ref = dirs[d]
            ch = step if d == 0 else DN_SCAN_CHUNKS - 1 - step
            rows = slice(ch * C, (ch + 1) * C)
            sl = slice(pr * LANES, (pr + 1) * LANES)
            lhs = jnp.concatenate([w_ref[0, rows, sl], qin_ref[0, rows, sl]], axis=0)
            ws[d, pr] = _dot(lhs, state[d, pr].astype(BF16))
        for d, pr in prob_ids:
            u_ref = dirs[d][0]
            ch = step if d == 0 else DN_SCAN_CHUNKS - 1 - step
            rows = slice(ch * C, (ch + 1) * C)
            sl = slice(pr * LANES, (pr + 1) * LANES)
            vb[d, pr] = (u_ref[0, rows, sl].astype(F32) - ws[d, pr][:C]).astype(BF16)
        for d, pr in prob_ids:
            u_ref, w_ref, qkd_ref, qin_ref, kupt_ref, egt_ref, o_ref = dirs[d]
            ch = step if d == 0 else DN_SCAN_CHUNKS - 1 - step
            rows = slice(ch * C, (ch + 1) * C)
            sl = slice(pr * LANES, (pr + 1) * LANES)
            upd = _dot(kupt_ref[0, ch, 0, pr], vb[d, pr])
            state[d, pr] = state[d, pr] * egt_ref[0, ch, 0, pr:pr + 1, :] + jnp.where(on_diag, upd, 0.0)
            o2 = ws[d, pr][C:] + _dot(qkd_ref[0, rows, sl], _block_diag(vb[d, pr], low))
            o_ref[0, rows, sl] = o2.astype(BF16)
    for d, pr in prob_ids:
        st_ref[d, pr] = state[d, pr]


def _dn_scan(u, w, qkd, qin, kupt, egt):
    B, S, _ = u.shape
    C = DN_CHUNK
    sc = DN_SCAN_CHUNKS
    nb = S // (C * sc)
    fwd3 = lambda b, i: (b, i, 0)
    bwd3 = lambda b, i: (b, nb - 1 - i, 1)
    row_f = pl.BlockSpec((1, sc * C, DN_QK), fwd3)
    row_b = pl.BlockSpec((1, sc * C, DN_QK), bwd3)
    kup_f = pl.BlockSpec((1, sc, 1, DN_PAIRS, LANES, C), lambda b, i: (b, i, 0, 0, 0, 0))
    kup_b = pl.BlockSpec((1, sc, 1, DN_PAIRS, LANES, C), lambda b, i: (b, nb - 1 - i, 1, 0, 0, 0))
    egt_f = pl.BlockSpec((1, sc, 1, DN_PAIRS, LANES), lambda b, i: (b, i, 0, 0, 0))
    egt_b = pl.BlockSpec((1, sc, 1, DN_PAIRS, LANES), lambda b, i: (b, nb - 1 - i, 1, 0, 0))
    return pl.pallas_call(
        _dn_scan_kernel,
        grid=(B, nb),
        in_specs=[row_f, row_f, row_f, row_f, kup_f, egt_f, row_b, row_b, row_b, row_b, kup_b, egt_b],
        out_specs=[pl.BlockSpec((1, sc * C, DN_V), fwd3),
                   pl.BlockSpec((1, sc * C, DN_V), lambda b, i: (b, nb - 1 - i, 0))],
        out_shape=[jax.ShapeDtypeStruct((B, S, DN_V), BF16), jax.ShapeDtypeStruct((B, S, DN_V), BF16)],
        scratch_shapes=[pltpu.VMEM((2, DN_PAIRS, LANES, LANES), F32)],
        compiler_params=_cparams(("parallel", "arbitrary")),
        name="dn_scan",
    )(u, w, qkd, qin, kupt, egt, u, w, qkd, qin, kupt, egt)


def _mem_kv_kernel(m_ref, g_ref, w_ref, kg_ref, k_ref, v_ref):
    x = m_ref[...]
    ms = jnp.mean(x * x, axis=-1, keepdims=True)
    h = ((x * lax.rsqrt(ms + EPS)) * g_ref[...]).astype(BF16)
    kv = _dot(h, w_ref[...])
    for hd in range(MEM_HEADS):
        kh = kv[:, hd * LANES:(hd + 1) * LANES]
        kms = jnp.mean(kh * kh, axis=-1, keepdims=True)
        k_ref[:, hd * LANES:(hd + 1) * LANES] = ((kh * lax.rsqrt(kms + EPS)) * kg_ref[...]).astype(BF16)
    v_ref[...] = kv[:, MEM_Q:].astype(BF16)


def _mem_kv(mem2, g, w_kv, kg):
    n = mem2.shape[0]
    return pl.pallas_call(
        _mem_kv_kernel,
        grid=(n // N_MEM,),
        in_specs=[pl.BlockSpec((N_MEM, D_MODEL), lambda i: (i, 0)),
                  pl.BlockSpec((1, D_MODEL), lambda i: (0, 0)),
                  pl.BlockSpec((D_MODEL, 2 * MEM_Q), lambda i: (0, 0)),
                  pl.BlockSpec((1, MEM_HEAD_DIM), lambda i: (0, 0))],
        out_specs=[pl.BlockSpec((N_MEM, MEM_Q), lambda i: (i, 0)), pl.BlockSpec((N_MEM, MEM_Q), lambda i: (i, 0))],
        out_shape=[jax.ShapeDtypeStruct((n, MEM_Q), BF16), jax.ShapeDtypeStruct((n, MEM_Q), BF16)],
        compiler_params=_cparams(("parallel",)),
        name="mem_kv",
    )(mem2, g, w_kv, kg)


def _mem_attend(q_ref, k_ref, v_ref, qg_ref):
    scale = MEM_HEAD_DIM ** -0.5 * LOG2E
    ones = jnp.ones((N_MEM, LANES), BF16)
    heads = []
    for hd in range(MEM_HEADS):
        sl = slice(hd * LANES, (hd + 1) * LANES)
        q = q_ref[:, sl].astype(F32)
        qms = jnp.mean(q * q, axis=-1, keepdims=True)
        qn = ((q * lax.rsqrt(qms + EPS)) * qg_ref[...] * scale).astype(BF16)
        s = _dot_nt(qn, k_ref[0, :, sl])
        p = jnp.exp2(s - jnp.max(s, axis=-1, keepdims=True))
        r = _dot(p.astype(BF16), jnp.concatenate([v_ref[0, :, sl], ones], axis=1))
        heads.append((r[:, :LANES] * (1.0 / r[:, LANES:])).astype(BF16))
    return jnp.concatenate(heads, axis=1)


def _merge_kernel(x_ref, ya_ref, of_ref, ob_ref, z_ref, mq_ref, mk_ref, mv_ref, mqg_ref, ng_ref, wg_ref, og_ref,
                  wa_ref, wd_ref, wm_ref, wo_ref, o_ref):
    y_mem = _mem_attend(mq_ref, mk_ref, mv_ref, mqg_ref)
    x = x_ref[...]
    ms = jnp.mean(x * x, axis=-1, keepdims=True)
    h = ((x * lax.rsqrt(ms + EPS)) * ng_ref[...]).astype(BF16)
    o = of_ref[...].astype(F32) + ob_ref[...].astype(F32)
    z = z_ref[...].astype(F32)
    og = og_ref[...]
    parts = []
    for pc in range(DN_V // LANES):
        os_ = o[:, pc * LANES:(pc + 1) * LANES]
        on = os_ * lax.rsqrt(_half_sums(os_ * os_) * (1.0 / DN_VALUE_DIM) + EPS) * og
        parts.append((on * _silu(z[:, pc * LANES:(pc + 1) * LANES])).astype(BF16))
    y_dn = jnp.concatenate(parts, axis=1)
    def gate(b):
        return _sigmoid(_dot(h, wg_ref[:, b * D_MODEL:(b + 1) * D_MODEL]))

    merged = (gate(0) * _dot(ya_ref[...], wa_ref[...])
              + gate(1) * _dot(y_dn, wd_ref[...])
              + gate(2) * _dot(y_mem, wm_ref[...]))
    o_ref[...] = x + _dot(merged.astype(BF16), wo_ref[...])


def _merge(x2, ya, of, ob, z, mq, mk, mv, mqg, ng, wg, og, wa, wd, wm, wo):
    n = x2.shape[0]
    tm = ROW_TILE
    tiles_per_seq = n // mk.shape[0] // tm
    row = lambda w_: pl.BlockSpec((tm, w_), lambda i: (i, 0))
    mem = pl.BlockSpec((1, N_MEM, MEM_Q), lambda i: (i // tiles_per_seq, 0, 0))
    full = lambda a, b: _resident((a, b))
    return pl.pallas_call(
        _merge_kernel,
        grid=(n // tm,),
        in_specs=[row(D_MODEL), row(ATTN_Q), row(DN_V), row(DN_V), row(DN_V), row(MEM_Q), mem, mem,
                  full(1, MEM_HEAD_DIM), full(1, D_MODEL), full(D_MODEL, N_BRANCH * D_MODEL),
                  full(1, LANES), full(ATTN_Q, D_MODEL), full(DN_V, D_MODEL), full(MEM_Q, D_MODEL),
                  full(D_MODEL, D_MODEL)],
        out_specs=row(D_MODEL),
        out_shape=jax.ShapeDtypeStruct((n, D_MODEL), F32),
        compiler_params=_cparams(("parallel",)),
        name="merge",
    )(x2, ya, of, ob, z, mq, mk, mv, mqg, ng, wg, og, wa, wd, wm, wo)


def _ffn_kernel(xp_ref, xc_ref, xn_ref, g_ref, wu_ref, cw_ref, cb_ref, wd_ref, o_ref, act_ref):
    i = pl.program_id(1)
    nt = pl.num_programs(1)
    tm = xc_ref.shape[1]
    xc = xc_ref[0]
    prev = jnp.where(i == 0, 0.0, xp_ref[0])
    nxt = jnp.where(i == nt - 1, 0.0, xn_ref[0])
    xe = jnp.concatenate([prev, xc, nxt], axis=0)
    ms = jnp.mean(xe * xe, axis=-1, keepdims=True)
    h = ((xe * lax.rsqrt(ms + EPS)) * g_ref[...]).astype(BF16)
    rows = tm + 2 * HALO

    def conv(u, c0):
        out = None
        for j in range(FFN_CONV):
            sh = (FFN_CONV // 2 - j) % rows
            us = u if sh == 0 else pltpu.roll(u, sh, axis=0)
            term = us[HALO:HALO + tm] * cw_ref[j:j + 1, c0:c0 + FF_CHUNK]
            out = term if out is None else out + term
        return out + cb_ref[:, c0:c0 + FF_CHUNK]

    for c in range(D_FF // FF_CHUNK):
        c0 = c * FF_CHUNK
        ug = conv(_dot(h, wu_ref[:, c0:c0 + FF_CHUNK]), c0)
        uv = conv(_dot(h, wu_ref[:, D_FF + c0:D_FF + c0 + FF_CHUNK]), D_FF + c0)
        act_ref[:, c0:c0 + FF_CHUNK] = (_silu(ug) * uv).astype(BF16)
    o_ref[0] = xc + _dot(act_ref[...], wd_ref[...])


def _ffn(x1, g, wu, cw, cb, wd):
    B, S, _ = x1.shape
    tm = FFN_ROW_TILE
    nt = S // tm
    hb = tm // HALO
    return pl.pallas_call(
        _ffn_kernel,
        grid=(B, nt),
        in_specs=[pl.BlockSpec((1, HALO, D_MODEL), lambda b, i: (b, jnp.maximum(i * hb - 1, 0), 0)),
                  pl.BlockSpec((1, tm, D_MODEL), lambda b, i: (b, i, 0)),
                  pl.BlockSpec((1, HALO, D_MODEL), lambda b, i: (b, jnp.minimum((i + 1) * hb, S // HALO - 1), 0)),
                  _resident((1, D_MODEL)), _resident((D_MODEL, 2 * D_FF)), _resident((FFN_CONV, 2 * D_FF)),
                  _resident((1, 2 * D_FF)), _resident((D_FF, D_MODEL))],
        out_specs=pl.BlockSpec((1, tm, D_MODEL), lambda b, i: (b, i, 0)),
        out_shape=jax.ShapeDtypeStruct((B, S, D_MODEL), F32),
        scratch_shapes=[pltpu.VMEM((tm, D_FF), BF16)],
        compiler_params=_cparams(("parallel", "parallel")),
        name="ffn",
    )(x1, x1, x1, g, wu, cw, cb, wd)


def _permute_w_in(w):
    idx = np.cumsum((0,) + IN_SPLITS)
    wb = w.astype(BF16)
    pad = jnp.zeros((w.shape[0], LANES - 4 * DN_HEADS), BF16)
    main = jnp.concatenate([wb[:, :idx[7]], wb[:, idx[9]:idx[10]], wb[:, idx[7]:idx[9]], pad], axis=1)
    return main, wb[:, idx[10]:]


def _layer(x, mem, rel_bias_table, p):
    B, S, D = x.shape
    n = B * S
    x2 = x.reshape(n, D)
    row = lambda a: a.reshape(1, -1).astype(F32)
    tile2 = lambda a: jnp.tile(a.astype(F32), 2)[None]
    w_main, w_gate = _permute_w_in(p["w_in"])
    aq, akv, dz, mq, qkvn, betax, gcx, gct = _inproj(
        x, row(p["norm_mix_g"]), w_main, tile2(p["attn_q_norm_g"]), tile2(p["attn_k_norm_g"]),
        p["dn_conv_w"], p["dn_a_log"], p["dn_dt_bias"])
    y_attn = _attn(aq, akv, rel_bias_table, p["attn_sink"])
    o_f, o_b = _dn_scan(*_dn_chunk(qkvn, betax, gcx, gct))
    mk, mv = _mem_kv(mem.reshape(B * N_MEM, D), row(p["mem_norm_g"]), p["mem_w_kv"].astype(BF16),
                     row(p["mem_k_norm_g"]))
    og2 = jnp.tile(p["dn_out_norm_g"].astype(F32), 2)[None]
    x1 = _merge(x2, y_attn.reshape(n, -1), o_f.reshape(n, -1), o_b.reshape(n, -1), dz.reshape(n, -1),
                mq.reshape(n, -1), mk.reshape(B, N_MEM, -1), mv.reshape(B, N_MEM, -1), row(p["mem_q_norm_g"]),
                row(p["norm_mix_g"]), w_gate, og2, p["w_br_attn"].astype(BF16), p["w_br_dn"].astype(BF16),
                p["w_br_mem"].astype(BF16), p["w_out"].astype(BF16))
    return _ffn(x1.reshape(B, S, D), row(p["norm_ffn_g"]), p["ffn_w_up"].astype(BF16), p["ffn_conv_w"].astype(F32),
                row(p["ffn_conv_b"]), p["ffn_w_down"].astype(BF16))


_LAYER_PARAMS = ("norm_mix_g", "w_in", "attn_q_norm_g", "attn_k_norm_g", "attn_sink", "dn_conv_w", "dn_a_log",
                 "dn_dt_bias", "dn_out_norm_g", "mem_norm_g", "mem_w_kv", "mem_q_norm_g", "mem_k_norm_g",
                 "w_br_attn", "w_br_dn", "w_br_mem", "w_out", "norm_ffn_g", "ffn_w_up", "ffn_conv_w", "ffn_conv_b",
                 "ffn_w_down")


def kernel(x, mem, rel_bias_table, norm_mix_g, w_in, attn_q_norm_g, attn_k_norm_g, attn_sink, dn_conv_w, dn_a_log,
           dn_dt_bias, dn_out_norm_g, mem_norm_g, mem_w_kv, mem_q_norm_g, mem_k_norm_g, w_br_attn, w_br_dn,
           w_br_mem, w_out, norm_ffn_g, ffn_w_up, ffn_conv_w, ffn_conv_b, ffn_w_down):
    stacked = dict(zip(_LAYER_PARAMS, (norm_mix_g, w_in, attn_q_norm_g, attn_k_norm_g, attn_sink, dn_conv_w,
                                       dn_a_log, dn_dt_bias, dn_out_norm_g, mem_norm_g, mem_w_kv, mem_q_norm_g,
                                       mem_k_norm_g, w_br_attn, w_br_dn, w_br_mem, w_out, norm_ffn_g, ffn_w_up,
                                       ffn_conv_w, ffn_conv_b, ffn_w_down)))
    depth = w_in.shape[0]
    for l in range(depth):
        x = _layer(x, mem, rel_bias_table, {k: v[l] for k, v in stacked.items()})
    return x
```

```python
import functools
import math

import numpy as np
import jax
import jax.numpy as jnp
from jax import lax
from jax.experimental import pallas as pl
from jax.experimental.pallas import tpu as pltpu

F32 = jnp.float32
BF16 = jnp.bfloat16

EPS = 1e-6
D_MODEL = 1024
N_MEM = 256
ATTN_HEADS = 8
ATTN_KV_HEADS = 2
ATTN_HEAD_DIM = 64
WINDOW = 128
ATTN_BLOCK = 128
REL_BUCKETS = 32
REL_MAX_DIST = 128
DN_HEADS = 8
DN_KEY_DIM = 64
DN_VALUE_DIM = 64
DN_CONV = 5
DN_CHUNK = 64
MEM_HEADS = 4
MEM_HEAD_DIM = 128
D_FF = 2816
FFN_CONV = 3
N_BRANCH = 3

ATTN_Q = ATTN_HEADS * ATTN_HEAD_DIM
ATTN_KV = ATTN_KV_HEADS * ATTN_HEAD_DIM
DN_QK = DN_HEADS * DN_KEY_DIM
DN_V = DN_HEADS * DN_VALUE_DIM
MEM_Q = MEM_HEADS * MEM_HEAD_DIM
IN_SPLITS = (ATTN_Q, ATTN_KV, ATTN_KV, DN_QK, DN_QK, DN_V, DN_V, 2 * DN_HEADS, 2 * DN_HEADS, MEM_Q,
             N_BRANCH * D_MODEL)

LANES = 128
HALF = 64
HALO = 8
NEG = -1e30
VMEM_LIMIT = 56 * 1024 * 1024

ROW_TILE = 1024
FFN_ROW_TILE = 1024
FF_CHUNK = 256


LOG2E = math.log2(math.e)


def _cparams(sem):
    return pltpu.CompilerParams(dimension_semantics=sem, vmem_limit_bytes=VMEM_LIMIT)


def _resident(shape):
    zeros = (0,) * len(shape)
    return pl.BlockSpec(shape, lambda *_: zeros, pipeline_mode=pl.Buffered(1))


def _dot(a, b):
    return jnp.dot(a, b, preferred_element_type=F32)


def _dot_nt(a, b):
    return lax.dot_general(a, b, (((1,), (1,)), ((), ())), preferred_element_type=F32)


def _dot_tn(a, b):
    return lax.dot_general(a, b, (((0,), (0,)), ((), ())), preferred_element_type=F32)


def _lane_is_low(shape):
    lane = lax.broadcasted_iota(jnp.int32, shape, len(shape) - 1)
    return (lane % LANES) < HALF


def _half_sums(sq):
    low = _lane_is_low(sq.shape)
    s_lo = jnp.sum(jnp.where(low, sq, 0.0), axis=-1, keepdims=True)
    s_hi = jnp.sum(jnp.where(low, 0.0, sq), axis=-1, keepdims=True)
    return jnp.where(low, s_lo, s_hi)


SUBLANES = 8


def _shift_rows(x, s):
    if s == 0:
        return x
    rr, cc = x.shape
    x3 = x.reshape(rr // SUBLANES, SUBLANES, cc)
    rot = pltpu.roll(x3, s % SUBLANES, axis=1)
    sub = lax.broadcasted_iota(jnp.int32, x3.shape, 1)
    if s > 0:
        other = jnp.concatenate([rot[-1:], rot[:-1]], axis=0)
        y3 = jnp.where(sub < s, other, rot)
    else:
        other = jnp.concatenate([rot[1:], rot[:1]], axis=0)
        y3 = jnp.where(sub >= SUBLANES + s, other, rot)
    return y3.reshape(rr, cc)


def _silu(x):
    return x * (1.0 / (1.0 + jnp.exp(-x)))


def _sigmoid(x):
    return 1.0 / (1.0 + jnp.exp(-x))


_C_AQ = (0, 512)
_C_AKV = (512, 768)
_C_DQKV = (768, 2304)
_C_DZ = (2304, 2816)
_C_MQ = (2816, 3328)
_C_BA = (3328, 3456)
_N_IN = 3456


def _head_rmsnorm(t, gain2):
    return t * lax.rsqrt(_half_sums(t * t) * (1.0 / ATTN_HEAD_DIM) + EPS) * gain2


def _split_hi_lo(x):
    hi = x.astype(BF16)
    lo = (x - hi.astype(F32)).astype(BF16)
    return hi, lo


def _inproj_kernel(xp_ref, xc_ref, xn_ref, g_ref, w_ref, qg_ref, kg_ref, cw_ref, alog_ref, dtb_ref, tri_ref,
                   expand_ref, aq_ref, akv_ref, dz_ref, mq_ref, qkv_ref, betax_ref, gcx_ref, gct_ref):
    i = pl.program_id(1)
    nt = pl.num_programs(1)
    tm = xc_ref.shape[1]
    prev = jnp.where(i == 0, 0.0, xp_ref[0])
    nxt = jnp.where(i == nt - 1, 0.0, xn_ref[0])
    xe = jnp.concatenate([prev, xc_ref[0], nxt], axis=0)
    ms = jnp.mean(xe * xe, axis=-1, keepdims=True)
    he = ((xe * lax.rsqrt(ms + EPS)) * g_ref[...]).astype(BF16)
    h = he[HALO:HALO + tm]

    def proj(c):
        return _dot(h, w_ref[:, c[0]:c[1]])

    rows = tm + 2 * HALO
    cw_chunk = 2 * LANES
    for c0 in range(0, 2 * DN_QK + DN_V, cw_chunk):
        ue = _dot(he, w_ref[:, _C_DQKV[0] + c0:_C_DQKV[0] + c0 + cw_chunk])
        acc = None
        for j in range(DN_CONV):
            term = _shift_rows(ue, DN_CONV // 2 - j)[HALO:HALO + tm] * cw_ref[j:j + 1, c0:c0 + cw_chunk]
            acc = term if acc is None else acc + term
        y = _silu(acc)
        for l0 in range(0, cw_chunk, LANES):
            ys = y[:, l0:l0 + LANES]
            if c0 < 2 * DN_QK:
                ys = ys * lax.rsqrt(_half_sums(ys * ys) + EPS)
            if c0 < DN_QK:
                ys = ys * (DN_KEY_DIM ** -0.5)
            qkv_ref[0, :, c0 + l0:c0 + l0 + LANES] = ys.astype(BF16)

    ba = proj(_C_BA)
    nh2 = 2 * DN_HEADS
    beta = _sigmoid(ba[:, :nh2])
    z = ba[:, nh2:2 * nh2] + dtb_ref[...]
    sp = jnp.maximum(z, 0.0) + jnp.log1p(jnp.exp(-jnp.abs(z)))
    g = -jnp.exp(alog_ref[...]) * sp
    g_hi, g_lo = _split_hi_lo(g)
    gg = jnp.concatenate([g_hi, g_lo], axis=1)
    tb = tri_ref.shape[1]
    pre = jnp.concatenate([_dot(tri_ref[0], gg[r0:r0 + tb]) for r0 in range(0, tm, tb)], axis=0)
    suf = jnp.concatenate([_dot(tri_ref[1], gg[r0:r0 + tb]) for r0 in range(0, tm, tb)], axis=0)
    lane16 = lax.broadcasted_iota(jnp.int32, (tm, nh2), 1)
    gc = jnp.where(lane16 < DN_HEADS, pre[:, :nh2] + pre[:, nh2:], suf[:, :nh2] + suf[:, nh2:])
    b_hi, b_lo = _split_hi_lo(beta)
    c_hi, c_lo = _split_hi_lo(gc)
    c_lo2 = (gc - c_hi.astype(F32) - c_lo.astype(F32)).astype(BF16)
    bx = _dot(jnp.concatenate([b_hi, b_lo, c_hi, c_lo, c_lo2], axis=1), expand_ref[...])
    betax_ref[0] = bx[:, :nh2 * HALF]
    gcx_ref[0] = bx[:, nh2 * HALF:]
    gct = jnp.concatenate([gc, jnp.zeros((tm, LANES - nh2), F32)], axis=1).T
    for c in range(tm // DN_CHUNK):
        gct_ref[0, c] = gct[:nh2, c * DN_CHUNK:(c + 1) * DN_CHUNK]

    aq = proj(_C_AQ)
    q_scale = ATTN_HEAD_DIM ** -0.5 * LOG2E
    for pc in range(ATTN_Q // LANES):
        sl = slice(pc * LANES, (pc + 1) * LANES)
        aq_ref[0, :, sl] = (_head_rmsnorm(aq[:, sl], qg_ref[...]) * q_scale).astype(BF16)
    akv = proj(_C_AKV)
    kn = _head_rmsnorm(akv[:, :LANES], kg_ref[...])
    av = akv[:, LANES:]
    akv_ref[0, :, 0 * LANES:1 * LANES] = kn.astype(BF16)
    akv_ref[0, :, 1 * LANES:2 * LANES] = av.astype(BF16)
    akv_ref[0, :, 2 * LANES:3 * LANES] = pltpu.roll(kn, HALF, axis=1).astype(BF16)
    akv_ref[0, :, 3 * LANES:4 * LANES] = pltpu.roll(av, HALF, axis=1).astype(BF16)
    dz_ref[0] = proj(_C_DZ).astype(BF16)
    mq_ref[0] = proj(_C_MQ).astype(BF16)


def _inproj(x, g, w, qg2, kg2, conv_w, a_log, dt_bias):
    B, S, D = x.shape
    tm = ROW_TILE
    nt = S // tm
    hb = tm // HALO
    nh2 = 2 * DN_HEADS
    tb = 2 * LANES
    r = np.arange(tb)
    same = (r[:, None] // DN_CHUNK) == (r[None, :] // DN_CHUNK)
    tri = np.stack([same & (r[:, None] >= r[None, :]), same & (r[:, None] <= r[None, :])]).astype(np.float32)
    rep = np.repeat(np.eye(nh2, dtype=np.float32), HALF, axis=1)
    zero = np.zeros_like(rep)
    expand = np.block([[rep, zero]] * 2 + [[zero, rep]] * 3)
    blk = lambda w_: pl.BlockSpec((1, tm, w_), lambda b, i: (b, i, 0))
    outs = [(ATTN_Q, BF16), (4 * LANES, BF16), (DN_V, BF16), (MEM_Q, BF16), (2 * DN_QK + DN_V, BF16),
            (nh2 * HALF, F32), (nh2 * HALF, F32)]
    return pl.pallas_call(
        _inproj_kernel,
        grid=(B, nt),
        in_specs=[pl.BlockSpec((1, HALO, D), lambda b, i: (b, jnp.maximum(i * hb - 1, 0), 0)),
                  blk(D),
                  pl.BlockSpec((1, HALO, D), lambda b, i: (b, jnp.minimum((i + 1) * hb, S // HALO - 1), 0)),
                  _resident((1, D)), _resident((D, _N_IN)), _resident((1, LANES)), _resident((1, LANES)),
                  _resident((DN_CONV, 2 * DN_QK + DN_V)), _resident((1, nh2)), _resident((1, nh2)),
                  _resident((2, tb, tb)), _resident((5 * nh2, 2 * nh2 * HALF))],
        out_specs=[blk(w_) for w_, _ in outs]
        + [pl.BlockSpec((1, tm // DN_CHUNK, nh2, DN_CHUNK), lambda b, i: (b, i, 0, 0))],
        out_shape=[jax.ShapeDtypeStruct((B, S, w_), dt) for w_, dt in outs]
        + [jax.ShapeDtypeStruct((B, S // DN_CHUNK, nh2, DN_CHUNK), F32)],
        compiler_params=_cparams(("parallel", "parallel")),
        name="inproj",
    )(x, x, x, g, w, qg2, kg2, conv_w.astype(F32), a_log.reshape(1, nh2).astype(F32),
      dt_bias.reshape(1, nh2).astype(F32), jnp.asarray(tri, BF16), jnp.asarray(expand, BF16))


def _t5_buckets(rel):
    nb = REL_BUCKETS // 2
    max_exact = nb // 2
    ret = (rel > 0).astype(np.int32) * nb
    n = np.abs(rel)
    large = max_exact + (np.log(np.maximum(n, 1) / max_exact) / np.log(REL_MAX_DIST / max_exact)
                         * (nb - max_exact)).astype(np.int32)
    large = np.minimum(large, nb - 1)
    return (ret + np.where(n < max_exact, n, large)).astype(np.int32)


_ATTN_GROUPS = ((0, 1, True, False), (0, 1, False, True), (2, 3, True, True), (2, 3, False, False))
_ATTN_GROUP_HEADS = ((0, 2), (1, 3), (4, 6), (5, 7))
ATTN_BLOCKS_PER_STEP = 2


def _attn_kernel(q_ref, kp_ref, kc_ref, kn_ref, bias_ref, sink_ref, o_ref):
    T = ATTN_BLOCK
    nq = ATTN_BLOCKS_PER_STEP
    first_blk = pl.program_id(1) * nq
    last_blk = pl.num_programs(1) * nq - 1
    kv_all = jnp.concatenate([kp_ref[0], kc_ref[0], kn_ref[0]], axis=0)
    ones = jnp.ones((3 * T, LANES), BF16)
    low_q = _lane_is_low((T, LANES))
    zero = jnp.zeros((T, LANES), BF16)
    probs = [(qb, gi) for qb in range(nq) for gi in range(len(_ATTN_GROUPS))]
    kvs, scores, maxes, res = {}, {}, {}, {}
    for qb in range(nq):
        kv = kv_all[qb * T:(qb + 3) * T]
        kvs[qb, False] = (kv[:, 0 * LANES:1 * LANES], jnp.concatenate([kv[:, 1 * LANES:2 * LANES], ones], axis=1))
        kvs[qb, True] = (kv[:, 2 * LANES:3 * LANES], jnp.concatenate([kv[:, 3 * LANES:4 * LANES], ones], axis=1))
    for qb, gi in probs:
        pa, pb, low, swapped = _ATTN_GROUPS[gi]
        rq = slice(qb * T, (qb + 1) * T)
        blk = first_blk + qb
        edge = jnp.where(blk == 0, 0, jnp.where(blk == last_blk, 2, 1))
        sel = low_q if low else jnp.logical_not(low_q)
        lhs = jnp.concatenate([jnp.where(sel, q_ref[0, rq, pa * LANES:(pa + 1) * LANES], zero),
                               jnp.where(sel, q_ref[0, rq, pb * LANES:(pb + 1) * LANES], zero)], axis=0)
        scores[qb, gi] = _dot_nt(lhs, kvs[qb, swapped][0]) + bias_ref[edge, gi]
    for qb, gi in probs:
        s = scores[qb, gi]
        m = jnp.maximum(jnp.max(s, axis=-1, keepdims=True), sink_ref[gi])
        maxes[qb, gi] = m
        scores[qb, gi] = jnp.exp2(s - m).astype(BF16)
    for qb, gi in probs:
        res[qb, gi] = _dot(scores[qb, gi], kvs[qb, _ATTN_GROUPS[gi][3]][1])
    for qb, gi in probs:
        r = res[qb, gi]
        den = r[:, LANES:] + jnp.exp2(sink_ref[gi] - maxes[qb, gi])
        res[qb, gi] = r[:, :LANES] * (1.0 / den)
    for qb in range(nq):
        rq = slice(qb * T, (qb + 1) * T)
        for pc, (ge, go) in enumerate(((0, 1), (0, 1), (2, 3), (2, 3))):
            r0 = (pc % 2) * T
            out = jnp.where(low_q, res[qb, ge][r0:r0 + T], res[qb, go][r0:r0 + T])
            o_ref[0, rq, pc * LANES:(pc + 1) * LANES] = out.astype(BF16)


def _attn(aq, akv, rel_table, sink):
    B, S, _ = aq.shape
    T = ATTN_BLOCK
    nb = S // T
    assert nb >= 2
    t_idx = np.arange(T)[:, None]
    j_idx = np.arange(3 * T)[None, :]
    rel = j_idx - T - t_idx
    onehot = jnp.asarray(np.eye(REL_BUCKETS, dtype=np.float32)[_t5_buckets(rel)])
    bias = jnp.einsum("tjr,rh->htj", onehot, rel_table.astype(F32), precision=lax.Precision.HIGHEST) * LOG2E
    in_win = np.abs(rel) <= WINDOW
    edge_ok = np.stack([in_win & (j_idx >= T), in_win, in_win & (j_idx < 2 * T)])
    bias = jnp.where(jnp.asarray(edge_ok)[:, None], bias[None], NEG)
    bias_g = jnp.stack([jnp.concatenate([bias[:, a], bias[:, b]], axis=1) for a, b in _ATTN_GROUP_HEADS], axis=1)
    sk = sink.astype(F32) * LOG2E
    sink_g = jnp.stack([jnp.concatenate([jnp.full((T, 1), 1.0) * sk[a], jnp.full((T, 1), 1.0) * sk[b]], axis=0)
                        for a, b in _ATTN_GROUP_HEADS])
    kv_w = akv.shape[-1]
    nq = ATTN_BLOCKS_PER_STEP
    assert nb % nq == 0
    return pl.pallas_call(
        _attn_kernel,
        grid=(B, nb // nq),
        in_specs=[pl.BlockSpec((1, nq * T, ATTN_Q), lambda b, i: (b, i, 0)),
                  pl.BlockSpec((1, T, kv_w), lambda b, i: (b, jnp.maximum(i * nq - 1, 0), 0)),
                  pl.BlockSpec((1, nq * T, kv_w), lambda b, i: (b, i, 0)),
                  pl.BlockSpec((1, T, kv_w), lambda b, i: (b, jnp.minimum((i + 1) * nq, nb - 1), 0)),
                  _resident((3, 4, 2 * T, 3 * T)), _resident((4, 2 * T, 1))],
        out_specs=pl.BlockSpec((1, nq * T, ATTN_Q), lambda b, i: (b, i, 0)),
        out_shape=jax.ShapeDtypeStruct((B, S, ATTN_Q), BF16),
        compiler_params=_cparams(("parallel", "parallel")),
        name="attn",
    )(aq, akv, akv, akv, bias_g, sink_g)


DN_TILE_HEADS = 2
DN_TILE = DN_TILE_HEADS * HALF
DN_TILES = DN_HEADS // DN_TILE_HEADS
DN_PAIRS = DN_HEADS // 2
DN_CHUNKS_PER_STEP = 4
DN_SCAN_CHUNKS = 16


def _head_of_lane(shape):
    return lax.broadcasted_iota(jnp.int32, shape, len(shape) - 1) // HALF


def _block_diag(x, head):
    zero = jnp.zeros_like(x)
    return jnp.concatenate([jnp.where(head == h, x, zero) for h in range(DN_TILE_HEADS)], axis=0)


def _dn_chunk_kernel(qkv_ref, bx_ref, gx_ref, gt_ref, u_ref, w_ref, qkd_ref, qin_ref, kupt_ref, egt_ref):
    C = DN_CHUNK
    TW = DN_TILE
    head = _head_of_lane((C, TW))
    r = lax.broadcasted_iota(jnp.int32, (C, TW), 0)
    m = lax.broadcasted_iota(jnp.int32, (C, TW), 1) % HALF
    eye = jnp.where(r == m, 1.0, 0.0)
    probs = []
    for ch in range(DN_CHUNKS_PER_STEP):
        rows = slice(ch * C, (ch + 1) * C)
        for tl in range(DN_TILES):
            q4 = qkv_ref[0, rows, tl * TW:(tl + 1) * TW]
            k4 = qkv_ref[0, rows, DN_QK + tl * TW:DN_QK + (tl + 1) * TW]
            v4 = qkv_ref[0, rows, 2 * DN_QK + tl * TW:2 * DN_QK + (tl + 1) * TW]
            q4f, k4f, v4f = q4.astype(F32), k4.astype(F32), v4.astype(F32)
            qkk = _dot_nt(jnp.concatenate([q4, k4], axis=0), _block_diag(k4, head))
            qk, kk = qkk[:C], qkk[C:]
            for d in range(2):
                col = slice(d * DN_QK + tl * TW, d * DN_QK + (tl + 1) * TW)
                beta4 = bx_ref[0, rows, col]
                gcol4 = gx_ref[0, rows, col]
                h0 = d * DN_HEADS + tl * DN_TILE_HEADS
                grow4 = jnp.concatenate([gt_ref[0, ch, h0 + h:h0 + h + 1, :] for h in range(DN_TILE_HEADS)],
                                        axis=1)
                incl = (r >= m) if d == 0 else (r <= m)
                strict = (r > m) if d == 0 else (r < m)
                decay = jnp.where(incl, jnp.exp(jnp.where(incl, gcol4 - grow4, 0.0)), 0.0)
                gtot4 = gcol4[C - 1:C] if d == 0 else gcol4[0:1]
                e_col = jnp.exp(gcol4)
                p = jnp.where(strict, -(kk * beta4 * decay), 0.0)
                vb = (v4f * beta4).astype(BF16)
                kbe = (k4f * beta4 * e_col).astype(BF16)
                rhs = jnp.concatenate([_block_diag(vb, head), _block_diag(kbe, head)], axis=1)
                qkd_ref[0, rows, col] = (qk * decay).astype(BF16)
                qin_ref[0, rows, col] = (q4f * e_col).astype(BF16)
                kupt_ref[0, ch, d, tl] = (k4f * jnp.exp(gtot4 - gcol4)).T.astype(BF16)
                egt_ref[0, ch, d, tl:tl + 1, :] = jnp.exp(gtot4)
                probs.append([p, eye + p, rhs, rows, col])
    for pb in probs:
        pb[0] = _dot(pb[0].astype(BF16), _block_diag(pb[0].astype(BF16), head))
    for _ in range(4):
        for pb in probs:
            both = _dot(jnp.concatenate([pb[1], pb[0]], axis=0).astype(BF16),
                        _block_diag(pb[0].astype(BF16), head))
            pb[1] = pb[1] + both[:C]
            pb[0] = both[C:]
    for p, t, rhs, rows, col in probs:
        t = t + _dot(t.astype(BF16), _block_diag(p.astype(BF16), head))
        sol = _dot(t.astype(BF16), rhs)
        u_ref[0, rows, col] = sol[:, :TW].astype(BF16)
        w_ref[0, rows, col] = sol[:, TW:].astype(BF16)


def _dn_chunk(qkvn, betax, gcx, gct):
    B, S, W = qkvn.shape
    C = DN_CHUNK
    cps = DN_CHUNKS_PER_STEP
    nc = S // C
    wide = 2 * DN_QK
    blk = lambda w_: pl.BlockSpec((1, cps * C, w_), lambda b, i: (b, i, 0))
    return pl.pallas_call(
        _dn_chunk_kernel,
        grid=(B, nc // cps),
        in_specs=[blk(W), blk(wide), blk(wide),
                  pl.BlockSpec((1, cps, 2 * DN_HEADS, C), lambda b, i: (b, i, 0, 0))],
        out_specs=[blk(wide), blk(wide), blk(wide), blk(wide),
                   pl.BlockSpec((1, cps, 2, DN_TILES, DN_TILE, C), lambda b, i: (b, i, 0, 0, 0, 0)),
                   pl.BlockSpec((1, cps, 2, DN_TILES, DN_TILE), lambda b, i: (b, i, 0, 0, 0))],
        out_shape=[jax.ShapeDtypeStruct((B, S, wide), BF16)] * 4
        + [jax.ShapeDtypeStruct((B, nc, 2, DN_TILES, DN_TILE, C), BF16),
           jax.ShapeDtypeStruct((B, nc, 2, DN_TILES, DN_TILE), F32)],
        compiler_params=_cparams(("parallel", "parallel")),
        name="dn_chunk",
    )(qkvn, betax, gcx, gct)


def _dn_scan_kernel(uf, wf, qkdf, qinf, kuptf, egtf, ub, wb, qkdb, qinb, kuptb, egtb, of_ref, ob_ref, st_ref):
    @pl.when(pl.program_id(1) == 0)
    def _():
        st_ref[...] = jnp.zeros_like(st_ref)

    C = DN_CHUNK
    TW = DN_TILE
    head = _head_of_lane((C, TW))
    rr = lax.broadcasted_iota(jnp.int32, (LANES, LANES), 0) < HALF
    cc = lax.broadcasted_iota(jnp.int32, (LANES, LANES), 1) < HALF
    on_diag = rr == cc
    zpad = jnp.zeros((LANES, LANES), BF16)
    ppt = TW // LANES
    dirs = ((uf, wf, qkdf, qinf, kuptf, egtf, of_ref), (ub, wb, qkdb, qinb, kuptb, egtb, ob_ref))
    tile_ids = [(d, tl) for d in range(2) for tl in range(DN_TILES)]
    state = {(d, pr): st_ref[d, pr] for d in range(2) for pr in range(DN_PAIRS)}
    for step in range(DN_SCAN_CHUNKS):
        ws, vb = {}, {}
        for d, tl in tile_ids:
            u_ref, w_ref, qkd_ref, qin_ref, kupt_ref, egt_ref, o_ref = dirs[d]
            ch = step if d == 0 else DN_SCAN_CHUNKS - 1 - step
            rows = slice(ch * C, (ch + 1) * C)
            sl = slice(tl * TW, (tl + 1) * TW)
            lhs = jnp.concatenate([w_ref[0, rows, sl], qin_ref[0, rows, sl]], axis=0)
            blocks = [state[d, ppt * tl + i].astype(BF16) for i in range(ppt)]
            s_tile = jnp.concatenate(
                [jnp.concatenate([blocks[i] if j == i else zpad for j in range(ppt)], axis=1) for i in range(ppt)],
                axis=0) if ppt > 1 else blocks[0]
            ws[d, tl] = _dot(lhs, s_tile)
        for d, tl in tile_ids:
            u_ref = dirs[d][0]
            ch = step if d == 0 else DN_SCAN_CHUNKS - 1 - step
            rows = slice(ch * C, (ch + 1) * C)
            sl = slice(tl * TW, (tl + 1) * TW)
            vb[d, tl] = (u_ref[0, rows, sl].astype(F32) - ws[d, tl][:C]).astype(BF16)
        for d, tl in tile_ids:
            u_ref, w_ref, qkd_ref, qin_ref, kupt_ref, egt_ref, o_ref = dirs[d]
            ch = step if d == 0 else DN_SCAN_CHUNKS - 1 - step
            rows = slice(ch * C, (ch + 1) * C)
            sl = slice(tl * TW, (tl + 1) * TW)
            for part in range(ppt):
                ls = slice(part * LANES, (part + 1) * LANES)
                upd = _dot(kupt_ref[0, ch, 0, tl, ls, :], vb[d, tl][:, ls])
                pr = ppt * tl + part
                state[d, pr] = state[d, pr] * egt_ref[0, ch, 0, tl:tl + 1, ls] + jnp.where(on_diag, upd, 0.0)
            o4 = ws[d, tl][C:] + _dot(qkd_ref[0, rows, sl], _block_diag(vb[d, tl], head))
            o_ref[0, rows, sl] = o4.astype(BF16)
    for dp, s in state.items():
        st_ref[dp[0], dp[1]] = s


def _dn_scan(u, w, qkd, qin, kupt, egt):
    B, S, _ = u.shape
    C = DN_CHUNK
    sc = DN_SCAN_CHUNKS
    nb = S // (C * sc)
    fwd3 = lambda b, i: (b, i, 0)
    bwd3 = lambda b, i: (b, nb - 1 - i, 1)
    row_f = pl.BlockSpec((1, sc * C, DN_QK), fwd3)
    row_b = pl.BlockSpec((1, sc * C, DN_QK), bwd3)
    kup_f = pl.BlockSpec((1, sc, 1, DN_TILES, DN_TILE, C), lambda b, i: (b, i, 0, 0, 0, 0))
    kup_b = pl.BlockSpec((1, sc, 1, DN_TILES, DN_TILE, C), lambda b, i: (b, nb - 1 - i, 1, 0, 0, 0))
    egt_f = pl.BlockSpec((1, sc, 1, DN_TILES, DN_TILE), lambda b, i: (b, i, 0, 0, 0))
    egt_b = pl.BlockSpec((1, sc, 1, DN_TILES, DN_TILE), lambda b, i: (b, nb - 1 - i, 1, 0, 0))
    return pl.pallas_call(
        _dn_scan_kernel,
        grid=(B, nb),
        in_specs=[row_f, row_f, row_f, row_f, kup_f, egt_f, row_b, row_b, row_b, row_b, kup_b, egt_b],
        out_specs=[pl.BlockSpec((1, sc * C, DN_V), fwd3),
                   pl.BlockSpec((1, sc * C, DN_V), lambda b, i: (b, nb - 1 - i, 0))],
        out_shape=[jax.ShapeDtypeStruct((B, S, DN_V), BF16), jax.ShapeDtypeStruct((B, S, DN_V), BF16)],
        scratch_shapes=[pltpu.VMEM((2, DN_PAIRS, LANES, LANES), F32)],
        compiler_params=_cparams(("parallel", "arbitrary")),
        name="dn_scan",
    )(u, w, qkd, qin, kupt, egt, u, w, qkd, qin, kupt, egt)


def _mem_kv_kernel(m_ref, g_ref, w_ref, kg_ref, k_ref, v_ref):
    x = m_ref[...]
    ms = jnp.mean(x * x, axis=-1, keepdims=True)
    h = ((x * lax.rsqrt(ms + EPS)) * g_ref[...]).astype(BF16)
    kv = _dot(h, w_ref[...])
    for hd in range(MEM_HEADS):
        kh = kv[:, hd * LANES:(hd + 1) * LANES]
        kms = jnp.mean(kh * kh, axis=-1, keepdims=True)
        k_ref[:, hd * LANES:(hd + 1) * LANES] = ((kh * lax.rsqrt(kms + EPS)) * kg_ref[...]).astype(BF16)
    v_ref[...] = kv[:, MEM_Q:].astype(BF16)


def _mem_kv(mem2, g, w_kv, kg):
    n = mem2.shape[0]
    return pl.pallas_call(
        _mem_kv_kernel,
        grid=(n // N_MEM,),
        in_specs=[pl.BlockSpec((N_MEM, D_MODEL), lambda i: (i, 0)),
                  pl.BlockSpec((1, D_MODEL), lambda i: (0, 0)),
                  pl.BlockSpec((D_MODEL, 2 * MEM_Q), lambda i: (0, 0)),
                  pl.BlockSpec((1, MEM_HEAD_DIM), lambda i: (0, 0))],
        out_specs=[pl.BlockSpec((N_MEM, MEM_Q), lambda i: (i, 0)), pl.BlockSpec((N_MEM, MEM_Q), lambda i: (i, 0))],
        out_shape=[jax.ShapeDtypeStruct((n, MEM_Q), BF16), jax.ShapeDtypeStruct((n, MEM_Q), BF16)],
        compiler_params=_cparams(("parallel",)),
        name="mem_kv",
    )(mem2, g, w_kv, kg)


def _mem_attend(q_ref, k_ref, v_ref, qg_ref):
    scale = MEM_HEAD_DIM ** -0.5 * LOG2E
    ones = jnp.ones((N_MEM, LANES), BF16)
    heads = []
    for hd in range(MEM_HEADS):
        sl = slice(hd * LANES, (hd + 1) * LANES)
        q = q_ref[:, sl].astype(F32)
        qms = jnp.mean(q * q, axis=-1, keepdims=True)
        qn = ((q * lax.rsqrt(qms + EPS)) * qg_ref[...] * scale).astype(BF16)
        s = _dot_nt(qn, k_ref[0, :, sl])
        p = jnp.exp2(s - jnp.max(s, axis=-1, keepdims=True))
        r = _dot(p.astype(BF16), jnp.concatenate([v_ref[0, :, sl], ones], axis=1))
        heads.append((r[:, :LANES] * (1.0 / r[:, LANES:])).astype(BF16))
    return jnp.concatenate(heads, axis=1)


def _merge_kernel(x_ref, ya_ref, of_ref, ob_ref, z_ref, mq_ref, mk_ref, mv_ref, mqg_ref, ng_ref, wg_ref, og_ref,
                  wa_ref, wd_ref, wm_ref, wo_ref, o_ref):
    y_mem = _mem_attend(mq_ref, mk_ref, mv_ref, mqg_ref)
    x = x_ref[...]
    ms = jnp.mean(x * x, axis=-1, keepdims=True)
    h = ((x * lax.rsqrt(ms + EPS)) * ng_ref[...]).astype(BF16)
    o = of_ref[...].astype(F32) + ob_ref[...].astype(F32)
    z = z_ref[...].astype(F32)
    og = og_ref[...]
    parts = []
    for pc in range(DN_V // LANES):
        os_ = o[:, pc * LANES:(pc + 1) * LANES]
        on = os_ * lax.rsqrt(_half_sums(os_ * os_) * (1.0 / DN_VALUE_DIM) + EPS) * og
        parts.append((on * _silu(z[:, pc * LANES:(pc + 1) * LANES])).astype(BF16))
    y_dn = jnp.concatenate(parts, axis=1)
    def gate(b):
        return _sigmoid(_dot(h, wg_ref[:, b * D_MODEL:(b + 1) * D_MODEL]))

    merged = (gate(0) * _dot(ya_ref[...], wa_ref[...])
              + gate(1) * _dot(y_dn, wd_ref[...])
              + gate(2) * _dot(y_mem, wm_ref[...]))
    o_ref[...] = x + _dot(merged.astype(BF16), wo_ref[...])


def _merge(x2, ya, of, ob, z, mq, mk, mv, mqg, ng, wg, og, wa, wd, wm, wo):
    n = x2.shape[0]
    tm = ROW_TILE
    tiles_per_seq = n // mk.shape[0] // tm
    row = lambda w_: pl.BlockSpec((tm, w_), lambda i: (i, 0))
    mem = pl.BlockSpec((1, N_MEM, MEM_Q), lambda i: (i // tiles_per_seq, 0, 0))
    full = lambda a, b: _resident((a, b))
    return pl.pallas_call(
        _merge_kernel,
        grid=(n // tm,),
        in_specs=[row(D_MODEL), row(ATTN_Q), row(DN_V), row(DN_V), row(DN_V), row(MEM_Q), mem, mem,
                  full(1, MEM_HEAD_DIM), full(1, D_MODEL), full(D_MODEL, N_BRANCH * D_MODEL),
                  full(1, LANES), full(ATTN_Q, D_MODEL), full(DN_V, D_MODEL), full(MEM_Q, D_MODEL),
                  full(D_MODEL, D_MODEL)],
        out_specs=row(D_MODEL),
        out_shape=jax.ShapeDtypeStruct((n, D_MODEL), F32),
        compiler_params=_cparams(("parallel",)),
        name="merge",
    )(x2, ya, of, ob, z, mq, mk, mv, mqg, ng, wg, og, wa, wd, wm, wo)


def _ffn_kernel(xp_ref, xc_ref, xn_ref, g_ref, wu_ref, cw_ref, cb_ref, wd_ref, o_ref, act_ref):
    i = pl.program_id(1)
    nt = pl.num_programs(1)
    tm = xc_ref.shape[1]
    xc = xc_ref[0]
    prev = jnp.where(i == 0, 0.0, xp_ref[0])
    nxt = jnp.where(i == nt - 1, 0.0, xn_ref[0])
    xe = jnp.concatenate([prev, xc, nxt], axis=0)
    ms = jnp.mean(xe * xe, axis=-1, keepdims=True)
    h = ((xe * lax.rsqrt(ms + EPS)) * g_ref[...]).astype(BF16)

    def both(ref, rows_, c0):
        return jnp.concatenate([ref[rows_, c0:c0 + FF_CHUNK], ref[rows_, D_FF + c0:D_FF + c0 + FF_CHUNK]], axis=1)

    for c in range(D_FF // FF_CHUNK):
        c0 = c * FF_CHUNK
        u = _dot(h, both(wu_ref, slice(None), c0))
        y = None
        for j in range(FFN_CONV):
            term = _shift_rows(u, FFN_CONV // 2 - j)[HALO:HALO + tm] * both(cw_ref, slice(j, j + 1), c0)
            y = term if y is None else y + term
        y = y + both(cb_ref, slice(None), c0)
        act_ref[:, c0:c0 + FF_CHUNK] = (_silu(y[:, :FF_CHUNK]) * y[:, FF_CHUNK:]).astype(BF16)
    o_ref[0] = xc + _dot(act_ref[...], wd_ref[...])


def _ffn(x1, g, wu, cw, cb, wd):
    B, S, _ = x1.shape
    tm = FFN_ROW_TILE
    nt = S // tm
    hb = tm // HALO
    return pl.pallas_call(
        _ffn_kernel,
        grid=(B, nt),
        in_specs=[pl.BlockSpec((1, HALO, D_MODEL), lambda b, i: (b, jnp.maximum(i * hb - 1, 0), 0)),
                  pl.BlockSpec((1, tm, D_MODEL), lambda b, i: (b, i, 0)),
                  pl.BlockSpec((1, HALO, D_MODEL), lambda b, i: (b, jnp.minimum((i + 1) * hb, S // HALO - 1), 0)),
                  _resident((1, D_MODEL)), _resident((D_MODEL, 2 * D_FF)), _resident((FFN_CONV, 2 * D_FF)),
                  _resident((1, 2 * D_FF)), _resident((D_FF, D_MODEL))],
        out_specs=pl.BlockSpec((1, tm, D_MODEL), lambda b, i: (b, i, 0)),
        out_shape=jax.ShapeDtypeStruct((B, S, D_MODEL), F32),
        scratch_shapes=[pltpu.VMEM((tm, D_FF), BF16)],
        compiler_params=_cparams(("parallel", "parallel")),
        name="ffn",
    )(x1, x1, x1, g, wu, cw, cb, wd)


def _permute_w_in(w):
    idx = np.cumsum((0,) + IN_SPLITS)
    wb = w.astype(BF16)
    pad = jnp.zeros((w.shape[0], LANES - 4 * DN_HEADS), BF16)
    main = jnp.concatenate([wb[:, :idx[7]], wb[:, idx[9]:idx[10]], wb[:, idx[7]:idx[9]], pad], axis=1)
    return main, wb[:, idx[10]:]


def _layer(x, mem, rel_bias_table, p):
    B, S, D = x.shape
    n = B * S
    x2 = x.reshape(n, D)
    row = lambda a: a.reshape(1, -1).astype(F32)
    tile2 = lambda a: jnp.tile(a.astype(F32), 2)[None]
    w_main, w_gate = _permute_w_in(p["w_in"])
    aq, akv, dz, mq, qkvn, betax, gcx, gct = _inproj(
        x, row(p["norm_mix_g"]), w_main, tile2(p["attn_q_norm_g"]), tile2(p["attn_k_norm_g"]),
        p["dn_conv_w"], p["dn_a_log"], p["dn_dt_bias"])
    y_attn = _attn(aq, akv, rel_bias_table, p["attn_sink"])
    o_f, o_b = _dn_scan(*_dn_chunk(qkvn, betax, gcx, gct))
    mk, mv = _mem_kv(mem.reshape(B * N_MEM, D), row(p["mem_norm_g"]), p["mem_w_kv"].astype(BF16),
                     row(p["mem_k_norm_g"]))
    og2 = jnp.tile(p["dn_out_norm_g"].astype(F32), 2)[None]
    x1 = _merge(x2, y_attn.reshape(n, -1), o_f.reshape(n, -1), o_b.reshape(n, -1), dz.reshape(n, -1),
                mq.reshape(n, -1), mk.reshape(B, N_MEM, -1), mv.reshape(B, N_MEM, -1), row(p["mem_q_norm_g"]),
                row(p["norm_mix_g"]), w_gate, og2, p["w_br_attn"].astype(BF16), p["w_br_dn"].astype(BF16),
                p["w_br_mem"].astype(BF16), p["w_out"].astype(BF16))
    return _ffn(x1.reshape(B, S, D), row(p["norm_ffn_g"]), p["ffn_w_up"].astype(BF16), p["ffn_conv_w"].astype(F32),
                row(p["ffn_conv_b"]), p["ffn_w_down"].astype(BF16))


_LAYER_PARAMS = ("norm_mix_g", "w_in", "attn_q_norm_g", "attn_k_norm_g", "attn_sink", "dn_conv_w", "dn_a_log",
                 "dn_dt_bias", "dn_out_norm_g", "mem_norm_g", "mem_w_kv", "mem_q_norm_g", "mem_k_norm_g",
                 "w_br_attn", "w_br_dn", "w_br_mem", "w_out", "norm_ffn_g", "ffn_w_up", "ffn_conv_w", "ffn_conv_b",
                 "ffn_w_down")


def kernel(x, mem, rel_bias_table, norm_mix_g, w_in, attn_q_norm_g, attn_k_norm_g, attn_sink, dn_conv_w, dn_a_log,
           dn_dt_bias, dn_out_norm_g, mem_norm_g, mem_w_kv, mem_q_norm_g, mem_k_norm_g, w_br_attn, w_br_dn,
           w_br_mem, w_out, norm_ffn_g, ffn_w_up, ffn_conv_w, ffn_conv_b, ffn_w_down):
    stacked = dict(zip(_LAYER_PARAMS, (norm_mix_g, w_in, attn_q_norm_g, attn_k_norm_g, attn_sink, dn_conv_w,
                                       dn_a_log, dn_dt_bias, dn_out_norm_g, mem_norm_g, mem_w_kv, mem_q_norm_g,
                                       mem_k_norm_g, w_br_attn, w_br_dn, w_br_mem, w_out, norm_ffn_g, ffn_w_up,
                                       ffn_conv_w, ffn_conv_b, ffn_w_down)))
    depth = w_in.shape[0]
    for l in range(depth):
        x = _layer(x, mem, rel_bias_table, {k: v[l] for k, v in stacked.items()})
    return x
```

```python
import functools
import math

import numpy as np
import jax
import jax.numpy as jnp
from jax import lax
from jax.experimental import pallas as pl
from jax.experimental.pallas import tpu as pltpu

F32 = jnp.float32
BF16 = jnp.bfloat16

EPS = 1e-6
D_MODEL = 1024
N_MEM = 256
ATTN_HEADS = 8
ATTN_KV_HEADS = 2
ATTN_HEAD_DIM = 64
WINDOW = 128
ATTN_BLOCK = 128
REL_BUCKETS = 32
REL_MAX_DIST = 128
DN_HEADS = 8
DN_KEY_DIM = 64
DN_VALUE_DIM = 64
DN_CONV = 5
DN_CHUNK = 64
MEM_HEADS = 4
MEM_HEAD_DIM = 128
D_FF = 2816
FFN_CONV = 3
N_BRANCH = 3

ATTN_Q = ATTN_HEADS * ATTN_HEAD_DIM
ATTN_KV = ATTN_KV_HEADS * ATTN_HEAD_DIM
DN_QK = DN_HEADS * DN_KEY_DIM
DN_V = DN_HEADS * DN_VALUE_DIM
MEM_Q = MEM_HEADS * MEM_HEAD_DIM
IN_SPLITS = (ATTN_Q, ATTN_KV, ATTN_KV, DN_QK, DN_QK, DN_V, DN_V, 2 * DN_HEADS, 2 * DN_HEADS, MEM_Q,
             N_BRANCH * D_MODEL)

LANES = 128
HALF = 64
HALO = 8
NEG = -1e30
VMEM_LIMIT = 56 * 1024 * 1024

ROW_TILE = 1024
INPROJ_ROW_TILE = 1024
FFN_ROW_TILE = 1024
FF_CHUNK = 256


LOG2E = math.log2(math.e)


def _cparams(sem):
    return pltpu.CompilerParams(dimension_semantics=sem, vmem_limit_bytes=VMEM_LIMIT)


def _resident(shape):
    zeros = (0,) * len(shape)
    return pl.BlockSpec(shape, lambda *_: zeros, pipeline_mode=pl.Buffered(1))


def _dot(a, b):
    return jnp.dot(a, b, preferred_element_type=F32)


def _dot_nt(a, b):
    return lax.dot_general(a, b, (((1,), (1,)), ((), ())), preferred_element_type=F32)


def _dot_tn(a, b):
    return lax.dot_general(a, b, (((0,), (0,)), ((), ())), preferred_element_type=F32)


def _lane_is_low(shape):
    lane = lax.broadcasted_iota(jnp.int32, shape, len(shape) - 1)
    return (lane % LANES) < HALF


def _half_sums(sq):
    low = _lane_is_low(sq.shape)
    s_lo = jnp.sum(jnp.where(low, sq, 0.0), axis=-1, keepdims=True)
    s_hi = jnp.sum(jnp.where(low, 0.0, sq), axis=-1, keepdims=True)
    return jnp.where(low, s_lo, s_hi)


SUBLANES = 8


def _shift_rows(x, s):
    if s == 0:
        return x
    rr, cc = x.shape
    x3 = x.reshape(rr // SUBLANES, SUBLANES, cc)
    rot = pltpu.roll(x3, s % SUBLANES, axis=1)
    sub = lax.broadcasted_iota(jnp.int32, x3.shape, 1)
    if s > 0:
        other = jnp.concatenate([rot[-1:], rot[:-1]], axis=0)
        y3 = jnp.where(sub < s, other, rot)
    else:
        other = jnp.concatenate([rot[1:], rot[:1]], axis=0)
        y3 = jnp.where(sub >= SUBLANES + s, other, rot)
    return y3.reshape(rr, cc)


def _silu(x):
    h = 0.5 * x
    return h * jnp.tanh(h) + h


def _sigmoid(x):
    return 0.5 * jnp.tanh(0.5 * x) + 0.5


_C_AQ = (0, 512)
_C_AKV = (512, 768)
_C_DQKV = (768, 2304)
_C_DZ = (2304, 2816)
_C_MQ = (2816, 3328)
_C_BA = (3328, 3456)
_N_IN = 3456


def _head_rmsnorm(t, gain2):
    return t * lax.rsqrt(_half_sums(t * t) * (1.0 / ATTN_HEAD_DIM) + EPS) * gain2


def _split_hi_lo(x):
    hi = x.astype(BF16)
    lo = (x - hi.astype(F32)).astype(BF16)
    return hi, lo


def _inproj_kernel(xp_ref, xc_ref, xn_ref, g_ref, w_ref, qg_ref, kg_ref, cw_ref, alog_ref, dtb_ref, tri_ref,
                   expand_ref, aq_ref, akv_ref, dz_ref, mq_ref, qkv_ref, betax_ref, gcx_ref, gct_ref):
    i = pl.program_id(1)
    nt = pl.num_programs(1)
    tm = xc_ref.shape[1]
    prev = jnp.where(i == 0, 0.0, xp_ref[0])
    nxt = jnp.where(i == nt - 1, 0.0, xn_ref[0])
    xe = jnp.concatenate([prev, xc_ref[0], nxt], axis=0)
    ms = jnp.mean(xe * xe, axis=-1, keepdims=True)
    he = ((xe * lax.rsqrt(ms + EPS)) * g_ref[...]).astype(BF16)
    h = he[HALO:HALO + tm]

    def proj(c):
        return _dot(h, w_ref[:, c[0]:c[1]])

    cw_chunk = 2 * LANES

    def dn_qkv_chunk(c0):
        ue = _dot(he, w_ref[:, _C_DQKV[0] + c0:_C_DQKV[0] + c0 + cw_chunk])
        acc = None
        for j in range(DN_CONV):
            term = _shift_rows(ue, DN_CONV // 2 - j)[HALO:HALO + tm] * cw_ref[j:j + 1, c0:c0 + cw_chunk]
            acc = term if acc is None else acc + term
        y = _silu(acc)
        for l0 in range(0, cw_chunk, LANES):
            ys = y[:, l0:l0 + LANES]
            if c0 < 2 * DN_QK:
                ys = ys * lax.rsqrt(_half_sums(ys * ys) + EPS)
            if c0 < DN_QK:
                ys = ys * (DN_KEY_DIM ** -0.5)
            qkv_ref[0, :, c0 + l0:c0 + l0 + LANES] = ys.astype(BF16)

    def dn_beta_decay():
        ba = proj(_C_BA)
        nh2 = 2 * DN_HEADS
        beta = _sigmoid(ba[:, :nh2])
        z = ba[:, nh2:2 * nh2] + dtb_ref[...]
        sp = jnp.maximum(z, 0.0) + jnp.log1p(jnp.exp(-jnp.abs(z)))
        g = -jnp.exp(alog_ref[...]) * sp
        g_hi, g_lo = _split_hi_lo(g)
        gg = jnp.concatenate([g_hi, g_lo], axis=1)
        tb = tri_ref.shape[1]
        pre = jnp.concatenate([_dot(tri_ref[0], gg[r0:r0 + tb]) for r0 in range(0, tm, tb)], axis=0)
        suf = jnp.concatenate([_dot(tri_ref[1], gg[r0:r0 + tb]) for r0 in range(0, tm, tb)], axis=0)
        lane16 = lax.broadcasted_iota(jnp.int32, (tm, nh2), 1)
        gc = jnp.where(lane16 < DN_HEADS, pre[:, :nh2] + pre[:, nh2:], suf[:, :nh2] + suf[:, nh2:])
        b_hi, b_lo = _split_hi_lo(beta)
        c_hi, c_lo = _split_hi_lo(gc)
        c_lo2 = (gc - c_hi.astype(F32) - c_lo.astype(F32)).astype(BF16)
        bx = _dot(jnp.concatenate([b_hi, b_lo, c_hi, c_lo, c_lo2], axis=1), expand_ref[...])
        betax_ref[0] = bx[:, :nh2 * HALF]
        gcx_ref[0] = bx[:, nh2 * HALF:]
        gct = jnp.concatenate([gc, jnp.zeros((tm, LANES - nh2), F32)], axis=1).T
        for c in range(tm // DN_CHUNK):
            gct_ref[0, c] = gct[:nh2, c * DN_CHUNK:(c + 1) * DN_CHUNK]

    def attn_q():
        aq = proj(_C_AQ)
        q_scale = ATTN_HEAD_DIM ** -0.5 * LOG2E
        for pc in range(ATTN_Q // LANES):
            sl = slice(pc * LANES, (pc + 1) * LANES)
            aq_ref[0, :, sl] = (_head_rmsnorm(aq[:, sl], qg_ref[...]) * q_scale).astype(BF16)

    def attn_kv():
        akv = proj(_C_AKV)
        kn = _head_rmsnorm(akv[:, :LANES], kg_ref[...])
        av = akv[:, LANES:]
        akv_ref[0, :, 0 * LANES:1 * LANES] = kn.astype(BF16)
        akv_ref[0, :, 1 * LANES:2 * LANES] = av.astype(BF16)
        akv_ref[0, :, 2 * LANES:3 * LANES] = pltpu.roll(kn, HALF, axis=1).astype(BF16)
        akv_ref[0, :, 3 * LANES:4 * LANES] = pltpu.roll(av, HALF, axis=1).astype(BF16)

    def dn_z():
        dz_ref[0] = proj(_C_DZ).astype(BF16)

    def mem_q():
        mq_ref[0] = proj(_C_MQ).astype(BF16)

    for c0 in range(0, 2 * DN_QK + DN_V, cw_chunk):
        dn_qkv_chunk(c0)
    for part in (dn_beta_decay, attn_q, attn_kv, dn_z, mem_q):
        part()


def _inproj(x, g, w, qg2, kg2, conv_w, a_log, dt_bias):
    B, S, D = x.shape
    tm = INPROJ_ROW_TILE
    nt = S // tm
    hb = tm // HALO
    nh2 = 2 * DN_HEADS
    tb = 2 * LANES
    r = np.arange(tb)
    same = (r[:, None] // DN_CHUNK) == (r[None, :] // DN_CHUNK)
    tri = np.stack([same & (r[:, None] >= r[None, :]), same & (r[:, None] <= r[None, :])]).astype(np.float32)
    rep = np.repeat(np.eye(nh2, dtype=np.float32), HALF, axis=1)
    zero = np.zeros_like(rep)
    expand = np.block([[rep, zero]] * 2 + [[zero, rep]] * 3)
    blk = lambda w_: pl.BlockSpec((1, tm, w_), lambda b, i: (b, i, 0))
    outs = [(ATTN_Q, BF16), (4 * LANES, BF16), (DN_V, BF16), (MEM_Q, BF16), (2 * DN_QK + DN_V, BF16),
            (nh2 * HALF, F32), (nh2 * HALF, F32)]
    return pl.pallas_call(
        _inproj_kernel,
        grid=(B, nt),
        in_specs=[pl.BlockSpec((1, HALO, D), lambda b, i: (b, jnp.maximum(i * hb - 1, 0), 0)),
                  blk(D),
                  pl.BlockSpec((1, HALO, D), lambda b, i: (b, jnp.minimum((i + 1) * hb, S // HALO - 1), 0)),
                  _resident((1, D)), _resident((D, _N_IN)), _resident((1, LANES)), _resident((1, LANES)),
                  _resident((DN_CONV, 2 * DN_QK + DN_V)), _resident((1, nh2)), _resident((1, nh2)),
                  _resident((2, tb, tb)), _resident((5 * nh2, 2 * nh2 * HALF))],
        out_specs=[blk(w_) for w_, _ in outs]
        + [pl.BlockSpec((1, tm // DN_CHUNK, nh2, DN_CHUNK), lambda b, i: (b, i, 0, 0))],
        out_shape=[jax.ShapeDtypeStruct((B, S, w_), dt) for w_, dt in outs]
        + [jax.ShapeDtypeStruct((B, S // DN_CHUNK, nh2, DN_CHUNK), F32)],
        compiler_params=_cparams(("parallel", "parallel")),
        name="inproj",
    )(x, x, x, g, w, qg2, kg2, conv_w.astype(F32), a_log.reshape(1, nh2).astype(F32),
      dt_bias.reshape(1, nh2).astype(F32), jnp.asarray(tri, BF16), jnp.asarray(expand, BF16))


def _t5_buckets(rel):
    nb = REL_BUCKETS // 2
    max_exact = nb // 2
    ret = (rel > 0).astype(np.int32) * nb
    n = np.abs(rel)
    large = max_exact + (np.log(np.maximum(n, 1) / max_exact) / np.log(REL_MAX_DIST / max_exact)
                         * (nb - max_exact)).astype(np.int32)
    large = np.minimum(large, nb - 1)
    return (ret + np.where(n < max_exact, n, large)).astype(np.int32)


_ATTN_GROUPS = ((0, 1, True, False), (0, 1, False, True), (2, 3, True, True), (2, 3, False, False))
_ATTN_GROUP_HEADS = ((0, 2), (1, 3), (4, 6), (5, 7))
ATTN_BLOCKS_PER_STEP = 8
ATTN_WAVE_BLOCKS = 2


def _attn_kernel(q_ref, kp_ref, kc_ref, kn_ref, bias_ref, sink_ref, o_ref):
    T = ATTN_BLOCK
    nq = ATTN_BLOCKS_PER_STEP
    first_blk = pl.program_id(1) * nq
    last_blk = pl.num_programs(1) * nq - 1
    kv_all = jnp.concatenate([kp_ref[0], kc_ref[0], kn_ref[0]], axis=0)
    ones = jnp.ones((3 * T, LANES), BF16)
    low_q = _lane_is_low((T, LANES))
    zero = jnp.zeros((T, LANES), BF16)
    groups = range(len(_ATTN_GROUPS))
    kvs, scores, maxes, res = {}, {}, {}, {}

    def score_stage(blocks):
        for qb in blocks:
            kv = kv_all[qb * T:(qb + 3) * T]
            kvs[qb, False] = (kv[:, 0 * LANES:1 * LANES],
                              jnp.concatenate([kv[:, 1 * LANES:2 * LANES], ones], axis=1))
            kvs[qb, True] = (kv[:, 2 * LANES:3 * LANES],
                             jnp.concatenate([kv[:, 3 * LANES:4 * LANES], ones], axis=1))
            rq = slice(qb * T, (qb + 1) * T)
            blk = first_blk + qb
            edge = jnp.where(blk == 0, 0, jnp.where(blk == last_blk, 2, 1))
            for gi in groups:
                pa, pb, low, swapped = _ATTN_GROUPS[gi]
                sel = low_q if low else jnp.logical_not(low_q)
                lhs = jnp.concatenate([jnp.where(sel, q_ref[0, rq, pa * LANES:(pa + 1) * LANES], zero),
                                       jnp.where(sel, q_ref[0, rq, pb * LANES:(pb + 1) * LANES], zero)], axis=0)
                scores[qb, gi] = _dot_nt(lhs, kvs[qb, swapped][0]) + bias_ref[edge, gi]

    def softmax_stages(blocks):
        probs = [(qb, gi) for qb in blocks for gi in groups]
        for qb, gi in probs:
            s = scores.pop((qb, gi))
            m = jnp.maximum(jnp.max(s, axis=-1, keepdims=True), sink_ref[gi])
            maxes[qb, gi] = m
            scores[qb, gi] = jnp.exp2(s - m).astype(BF16)
        for qb, gi in probs:
            res[qb, gi] = _dot(scores.pop((qb, gi)), kvs[qb, _ATTN_GROUPS[gi][3]][1])
        for qb, gi in probs:
            r = res[qb, gi]
            den = r[:, LANES:] + jnp.exp2(sink_ref[gi] - maxes[qb, gi])
            res[qb, gi] = r[:, :LANES] * (1.0 / den)
        for qb in blocks:
            rq = slice(qb * T, (qb + 1) * T)
            for pc, (ge, go) in enumerate(((0, 1), (0, 1), (2, 3), (2, 3))):
                r0 = (pc % 2) * T
                out = jnp.where(low_q, res[qb, ge][r0:r0 + T], res[qb, go][r0:r0 + T])
                o_ref[0, rq, pc * LANES:(pc + 1) * LANES] = out.astype(BF16)

    waves = [range(w0, min(w0 + ATTN_WAVE_BLOCKS, nq)) for w0 in range(0, nq, ATTN_WAVE_BLOCKS)]
    score_stage(waves[0])
    for w, blocks in enumerate(waves):
        if w + 1 < len(waves):
            score_stage(waves[w + 1])
        softmax_stages(blocks)


def _attn(aq, akv, rel_table, sink):
    B, S, _ = aq.shape
    T = ATTN_BLOCK
    nb = S // T
    assert nb >= 2
    t_idx = np.arange(T)[:, None]
    j_idx = np.arange(3 * T)[None, :]
    rel = j_idx - T - t_idx
    onehot = jnp.asarray(np.eye(REL_BUCKETS, dtype=np.float32)[_t5_buckets(rel)])
    bias = jnp.einsum("tjr,rh->htj", onehot, rel_table.astype(F32), precision=lax.Precision.HIGHEST) * LOG2E
    in_win = np.abs(rel) <= WINDOW
    edge_ok = np.stack([in_win & (j_idx >= T), in_win, in_win & (j_idx < 2 * T)])
    bias = jnp.where(jnp.asarray(edge_ok)[:, None], bias[None], NEG)
    bias_g = jnp.stack([jnp.concatenate([bias[:, a], bias[:, b]], axis=1) for a, b in _ATTN_GROUP_HEADS], axis=1)
    sk = sink.astype(F32) * LOG2E
    sink_g = jnp.stack([jnp.concatenate([jnp.full((T, 1), 1.0) * sk[a], jnp.full((T, 1), 1.0) * sk[b]], axis=0)
                        for a, b in _ATTN_GROUP_HEADS])
    kv_w = akv.shape[-1]
    nq = ATTN_BLOCKS_PER_STEP
    assert nb % nq == 0
    return pl.pallas_call(
        _attn_kernel,
        grid=(B, nb // nq),
        in_specs=[pl.BlockSpec((1, nq * T, ATTN_Q), lambda b, i: (b, i, 0)),
                  pl.BlockSpec((1, T, kv_w), lambda b, i: (b, jnp.maximum(i * nq - 1, 0), 0)),
                  pl.BlockSpec((1, nq * T, kv_w), lambda b, i: (b, i, 0)),
                  pl.BlockSpec((1, T, kv_w), lambda b, i: (b, jnp.minimum((i + 1) * nq, nb - 1), 0)),
                  _resident((3, 4, 2 * T, 3 * T)), _resident((4, 2 * T, 1))],
        out_specs=pl.BlockSpec((1, nq * T, ATTN_Q), lambda b, i: (b, i, 0)),
        out_shape=jax.ShapeDtypeStruct((B, S, ATTN_Q), BF16),
        compiler_params=_cparams(("parallel", "parallel")),
        name="attn",
    )(aq, akv, akv, akv, bias_g, sink_g)


DN_TILE_HEADS = 2
DN_TILE = DN_TILE_HEADS * HALF
DN_TILES = DN_HEADS // DN_TILE_HEADS
DN_PAIRS = DN_HEADS // 2
DN_CHUNKS_PER_STEP = 4
DN_SCAN_CHUNKS = 16


def _head_of_lane(shape):
    return lax.broadcasted_iota(jnp.int32, shape, len(shape) - 1) // HALF


def _block_diag(x, head):
    zero = jnp.zeros_like(x)
    return jnp.concatenate([jnp.where(head == h, x, zero) for h in range(DN_TILE_HEADS)], axis=0)


def _dn_chunk_kernel(qkv_ref, bx_ref, gx_ref, gt_ref, u_ref, w_ref, qkd_ref, qin_ref, kupt_ref, egt_ref):
    C = DN_CHUNK
    TW = DN_TILE
    head = _head_of_lane((C, TW))
    r = lax.broadcasted_iota(jnp.int32, (C, TW), 0)
    m = lax.broadcasted_iota(jnp.int32, (C, TW), 1) % HALF
    eye = jnp.where(r == m, 1.0, 0.0)
    probs = []
    for ch in range(DN_CHUNKS_PER_STEP):
        rows = slice(ch * C, (ch + 1) * C)
        for tl in range(DN_TILES):
            q4 = qkv_ref[0, rows, tl * TW:(tl + 1) * TW]
            k4 = qkv_ref[0, rows, DN_QK + tl * TW:DN_QK + (tl + 1) * TW]
            v4 = qkv_ref[0, rows, 2 * DN_QK + tl * TW:2 * DN_QK + (tl + 1) * TW]
            q4f, k4f, v4f = q4.astype(F32), k4.astype(F32), v4.astype(F32)
            qkk = _dot_nt(jnp.concatenate([q4, k4], axis=0), _block_diag(k4, head))
            qk, kk = qkk[:C], qkk[C:]
            for d in range(2):
                col = slice(d * DN_QK + tl * TW, d * DN_QK + (tl + 1) * TW)
                beta4 = bx_ref[0, rows, col]
                gcol4 = gx_ref[0, rows, col]
                h0 = d * DN_HEADS + tl * DN_TILE_HEADS
                grow4 = jnp.concatenate([gt_ref[0, ch, h0 + h:h0 + h + 1, :] for h in range(DN_TILE_HEADS)],
                                        axis=1)
                incl = (r >= m) if d == 0 else (r <= m)
                strict = (r > m) if d == 0 else (r < m)
                decay = jnp.where(incl, jnp.exp(jnp.where(incl, gcol4 - grow4, 0.0)), 0.0)
                gtot4 = gcol4[C - 1:C] if d == 0 else gcol4[0:1]
                e_col = jnp.exp(gcol4)
                p = jnp.where(strict, -(kk * beta4 * decay), 0.0)
                vb = (v4f * beta4).astype(BF16)
                kbe = (k4f * beta4 * e_col).astype(BF16)
                rhs = jnp.concatenate([_block_diag(vb, head), _block_diag(kbe, head)], axis=1)
                qkd_ref[0, rows, col] = (qk * decay).astype(BF16)
                qin_ref[0, rows, col] = (q4f * e_col).astype(BF16)
                kupt_ref[0, ch, d, tl] = (k4f * jnp.exp(gtot4 - gcol4)).T.astype(BF16)
                egt_ref[0, ch, d, tl:tl + 1, :] = jnp.exp(gtot4)
                probs.append([p, eye + p, rhs, rows, col])
    for pb in probs:
        pb[0] = _dot(pb[0].astype(BF16), _block_diag(pb[0].astype(BF16), head))
    for _ in range(4):
        for pb in probs:
            both = _dot(jnp.concatenate([pb[1], pb[0]], axis=0).astype(BF16),
                        _block_diag(pb[0].astype(BF16), head))
            pb[1] = pb[1] + both[:C]
            pb[0] = both[C:]
    for p, t, rhs, rows, col in probs:
        t = t + _dot(t.astype(BF16), _block_diag(p.astype(BF16), head))
        sol = _dot(t.astype(BF16), rhs)
        u_ref[0, rows, col] = sol[:, :TW].astype(BF16)
        w_ref[0, rows, col] = sol[:, TW:].astype(BF16)


def _dn_chunk(qkvn, betax, gcx, gct):
    B, S, W = qkvn.shape
    C = DN_CHUNK
    cps = DN_CHUNKS_PER_STEP
    nc = S // C
    wide = 2 * DN_QK
    blk = lambda w_: pl.BlockSpec((1, cps * C, w_), lambda b, i: (b, i, 0))
    return pl.pallas_call(
        _dn_chunk_kernel,
        grid=(B, nc // cps),
        in_specs=[blk(W), blk(wide), blk(wide),
                  pl.BlockSpec((1, cps, 2 * DN_HEADS, C), lambda b, i: (b, i, 0, 0))],
        out_specs=[blk(wide), blk(wide), blk(wide), blk(wide),
                   pl.BlockSpec((1, cps, 2, DN_TILES, DN_TILE, C), lambda b, i: (b, i, 0, 0, 0, 0)),
                   pl.BlockSpec((1, cps, 2, DN_TILES, DN_TILE), lambda b, i: (b, i, 0, 0, 0))],
        out_shape=[jax.ShapeDtypeStruct((B, S, wide), BF16)] * 4
        + [jax.ShapeDtypeStruct((B, nc, 2, DN_TILES, DN_TILE, C), BF16),
           jax.ShapeDtypeStruct((B, nc, 2, DN_TILES, DN_TILE), F32)],
        compiler_params=_cparams(("parallel", "parallel")),
        name="dn_chunk",
    )(qkvn, betax, gcx, gct)


def _dn_scan_kernel(uf, wf, qkdf, qinf, kuptf, egtf, ub, wb, qkdb, qinb, kuptb, egtb, of_ref, ob_ref, st_ref):
    @pl.when(pl.program_id(1) == 0)
    def _():
        st_ref[...] = jnp.zeros_like(st_ref)

    C = DN_CHUNK
    TW = DN_TILE
    head = _head_of_lane((C, TW))
    rr = lax.broadcasted_iota(jnp.int32, (LANES, LANES), 0) < HALF
    cc = lax.broadcasted_iota(jnp.int32, (LANES, LANES), 1) < HALF
    on_diag = rr == cc
    zpad = jnp.zeros((LANES, LANES), BF16)
    ppt = TW // LANES
    dirs = ((uf, wf, qkdf, qinf, kuptf, egtf, of_ref), (ub, wb, qkdb, qinb, kuptb, egtb, ob_ref))
    tile_ids = [(d, tl) for d in range(2) for tl in range(DN_TILES)]
    state = {(d, pr): st_ref[d, pr] for d in range(2) for pr in range(DN_PAIRS)}
    for step in range(DN_SCAN_CHUNKS):
        ws, vb = {}, {}
        for d, tl in tile_ids:
            u_ref, w_ref, qkd_ref, qin_ref, kupt_ref, egt_ref, o_ref = dirs[d]
            ch = step if d == 0 else DN_SCAN_CHUNKS - 1 - step
            rows = slice(ch * C, (ch + 1) * C)
            sl = slice(tl * TW, (tl + 1) * TW)
            lhs = jnp.concatenate([w_ref[0, rows, sl], qin_ref[0, rows, sl]], axis=0)
            blocks = [state[d, ppt * tl + i].astype(BF16) for i in range(ppt)]
            s_tile = jnp.concatenate(
                [jnp.concatenate([blocks[i] if j == i else zpad for j in range(ppt)], axis=1) for i in range(ppt)],
                axis=0) if ppt > 1 else blocks[0]
            ws[d, tl] = _dot(lhs, s_tile)
        for d, tl in tile_ids:
            u_ref = dirs[d][0]
            ch = step if d == 0 else DN_SCAN_CHUNKS - 1 - step
            rows = slice(ch * C, (ch + 1) * C)
            sl = slice(tl * TW, (tl + 1) * TW)
            vb[d, tl] = (u_ref[0, rows, sl].astype(F32) - ws[d, tl][:C]).astype(BF16)
        for d, tl in tile_ids:
            u_ref, w_ref, qkd_ref, qin_ref, kupt_ref, egt_ref, o_ref = dirs[d]
            ch = step if d == 0 else DN_SCAN_CHUNKS - 1 - step
            rows = slice(ch * C, (ch + 1) * C)
            sl = slice(tl * TW, (tl + 1) * TW)
            for part in range(ppt):
                ls = slice(part * LANES, (part + 1) * LANES)
                upd = _dot(kupt_ref[0, ch, 0, tl, ls, :], vb[d, tl][:, ls])
                pr = ppt * tl + part
                state[d, pr] = state[d, pr] * egt_ref[0, ch, 0, tl:tl + 1, ls] + jnp.where(on_diag, upd, 0.0)
            o4 = ws[d, tl][C:] + _dot(qkd_ref[0, rows, sl], _block_diag(vb[d, tl], head))
            o_ref[0, rows, sl] = o4.astype(BF16)
    for dp, s in state.items():
        st_ref[dp[0], dp[1]] = s


def _dn_scan(u, w, qkd, qin, kupt, egt):
    B, S, _ = u.shape
    C = DN_CHUNK
    sc = DN_SCAN_CHUNKS
    nb = S // (C * sc)
    fwd3 = lambda b, i: (b, i, 0)
    bwd3 = lambda b, i: (b, nb - 1 - i, 1)
    row_f = pl.BlockSpec((1, sc * C, DN_QK), fwd3)
    row_b = pl.BlockSpec((1, sc * C, DN_QK), bwd3)
    kup_f = pl.BlockSpec((1, sc, 1, DN_TILES, DN_TILE, C), lambda b, i: (b, i, 0, 0, 0, 0))
    kup_b = pl.BlockSpec((1, sc, 1, DN_TILES, DN_TILE, C), lambda b, i: (b, nb - 1 - i, 1, 0, 0, 0))
    egt_f = pl.BlockSpec((1, sc, 1, DN_TILES, DN_TILE), lambda b, i: (b, i, 0, 0, 0))
    egt_b = pl.BlockSpec((1, sc, 1, DN_TILES, DN_TILE), lambda b, i: (b, nb - 1 - i, 1, 0, 0))
    return pl.pallas_call(
        _dn_scan_kernel,
        grid=(B, nb),
        in_specs=[row_f, row_f, row_f, row_f, kup_f, egt_f, row_b, row_b, row_b, row_b, kup_b, egt_b],
        out_specs=[pl.BlockSpec((1, sc * C, DN_V), fwd3),
                   pl.BlockSpec((1, sc * C, DN_V), lambda b, i: (b, nb - 1 - i, 0))],
        out_shape=[jax.ShapeDtypeStruct((B, S, DN_V), BF16), jax.ShapeDtypeStruct((B, S, DN_V), BF16)],
        scratch_shapes=[pltpu.VMEM((2, DN_PAIRS, LANES, LANES), F32)],
        compiler_params=_cparams(("parallel", "arbitrary")),
        name="dn_scan",
    )(u, w, qkd, qin, kupt, egt, u, w, qkd, qin, kupt, egt)


def _mem_kv_kernel(m_ref, g_ref, w_ref, kg_ref, k_ref, v_ref):
    x = m_ref[...]
    ms = jnp.mean(x * x, axis=-1, keepdims=True)
    h = ((x * lax.rsqrt(ms + EPS)) * g_ref[...]).astype(BF16)
    kv = _dot(h, w_ref[...])
    for hd in range(MEM_HEADS):
        kh = kv[:, hd * LANES:(hd + 1) * LANES]
        kms = jnp.mean(kh * kh, axis=-1, keepdims=True)
        k_ref[:, hd * LANES:(hd + 1) * LANES] = ((kh * lax.rsqrt(kms + EPS)) * kg_ref[...]).astype(BF16)
    v_ref[...] = kv[:, MEM_Q:].astype(BF16)


def _mem_kv(mem2, g, w_kv, kg):
    n = mem2.shape[0]
    return pl.pallas_call(
        _mem_kv_kernel,
        grid=(n // N_MEM,),
        in_specs=[pl.BlockSpec((N_MEM, D_MODEL), lambda i: (i, 0)),
                  pl.BlockSpec((1, D_MODEL), lambda i: (0, 0)),
                  pl.BlockSpec((D_MODEL, 2 * MEM_Q), lambda i: (0, 0)),
                  pl.BlockSpec((1, MEM_HEAD_DIM), lambda i: (0, 0))],
        out_specs=[pl.BlockSpec((N_MEM, MEM_Q), lambda i: (i, 0)), pl.BlockSpec((N_MEM, MEM_Q), lambda i: (i, 0))],
        out_shape=[jax.ShapeDtypeStruct((n, MEM_Q), BF16), jax.ShapeDtypeStruct((n, MEM_Q), BF16)],
        compiler_params=_cparams(("parallel",)),
        name="mem_kv",
    )(mem2, g, w_kv, kg)


def _mem_attend(q_ref, k_ref, v_ref, qg_ref):
    scale = MEM_HEAD_DIM ** -0.5 * LOG2E
    ones = jnp.ones((N_MEM, LANES), BF16)
    heads = []
    for hd in range(MEM_HEADS):
        sl = slice(hd * LANES, (hd + 1) * LANES)
        q = q_ref[:, sl].astype(F32)
        qms = jnp.mean(q * q, axis=-1, keepdims=True)
        qn = ((q * lax.rsqrt(qms + EPS)) * qg_ref[...] * scale).astype(BF16)
        s = _dot_nt(qn, k_ref[0, :, sl])
        p = jnp.exp2(s - jnp.max(s, axis=-1, keepdims=True))
        r = _dot(p.astype(BF16), jnp.concatenate([v_ref[0, :, sl], ones], axis=1))
        heads.append((r[:, :LANES] * (1.0 / r[:, LANES:])).astype(BF16))
    return jnp.concatenate(heads, axis=1)


def _merge_kernel(x_ref, ya_ref, of_ref, ob_ref, z_ref, mq_ref, mk_ref, mv_ref, mqg_ref, ng_ref, wg_ref, og_ref,
                  wa_ref, wd_ref, wm_ref, wo_ref, o_ref):
    y_mem = _mem_attend(mq_ref, mk_ref, mv_ref, mqg_ref)
    x = x_ref[...]
    ms = jnp.mean(x * x, axis=-1, keepdims=True)
    h = ((x * lax.rsqrt(ms + EPS)) * ng_ref[...]).astype(BF16)
    o = of_ref[...].astype(F32) + ob_ref[...].astype(F32)
    z = z_ref[...].astype(F32)
    og = og_ref[...]
    parts = []
    for pc in range(DN_V // LANES):
        os_ = o[:, pc * LANES:(pc + 1) * LANES]
        on = os_ * lax.rsqrt(_half_sums(os_ * os_) * (1.0 / DN_VALUE_DIM) + EPS) * og
        parts.append((on * _silu(z[:, pc * LANES:(pc + 1) * LANES])).astype(BF16))
    y_dn = jnp.concatenate(parts, axis=1)
    def gate(b):
        return _sigmoid(_dot(h, wg_ref[:, b * D_MODEL:(b + 1) * D_MODEL]))

    merged = (gate(0) * _dot(ya_ref[...], wa_ref[...])
              + gate(1) * _dot(y_dn, wd_ref[...])
              + gate(2) * _dot(y_mem, wm_ref[...]))
    o_ref[...] = x + _dot(merged.astype(BF16), wo_ref[...])


def _merge(x2, ya, of, ob, z, mq, mk, mv, mqg, ng, wg, og, wa, wd, wm, wo):
    n = x2.shape[0]
    tm = ROW_TILE
    tiles_per_seq = n // mk.shape[0] // tm
    row = lambda w_: pl.BlockSpec((tm, w_), lambda i: (i, 0))
    mem = pl.BlockSpec((1, N_MEM, MEM_Q), lambda i: (i // tiles_per_seq, 0, 0))
    full = lambda a, b: _resident((a, b))
    return pl.pallas_call(
        _merge_kernel,
        grid=(n // tm,),
        in_specs=[row(D_MODEL), row(ATTN_Q), row(DN_V), row(DN_V), row(DN_V), row(MEM_Q), mem, mem,
                  full(1, MEM_HEAD_DIM), full(1, D_MODEL), full(D_MODEL, N_BRANCH * D_MODEL),
                  full(1, LANES), full(ATTN_Q, D_MODEL), full(DN_V, D_MODEL), full(MEM_Q, D_MODEL),
                  full(D_MODEL, D_MODEL)],
        out_specs=row(D_MODEL),
        out_shape=jax.ShapeDtypeStruct((n, D_MODEL), F32),
        compiler_params=_cparams(("parallel",)),
        name="merge",
    )(x2, ya, of, ob, z, mq, mk, mv, mqg, ng, wg, og, wa, wd, wm, wo)


def _ffn_kernel(xp_ref, xc_ref, xn_ref, g_ref, wu_ref, cw_ref, cb_ref, wd_ref, o_ref, act_ref):
    i = pl.program_id(1)
    nt = pl.num_programs(1)
    tm = xc_ref.shape[1]
    xc = xc_ref[0]
    prev = jnp.where(i == 0, 0.0, xp_ref[0])
    nxt = jnp.where(i == nt - 1, 0.0, xn_ref[0])
    xe = jnp.concatenate([prev, xc, nxt], axis=0)
    ms = jnp.mean(xe * xe, axis=-1, keepdims=True)
    h = ((xe * lax.rsqrt(ms + EPS)) * g_ref[...]).astype(BF16)

    def both(ref, rows_, c0):
        return jnp.concatenate([ref[rows_, c0:c0 + FF_CHUNK], ref[rows_, D_FF + c0:D_FF + c0 + FF_CHUNK]], axis=1)

    for c in range(D_FF // FF_CHUNK):
        c0 = c * FF_CHUNK
        u = _dot(h, both(wu_ref, slice(None), c0))
        y = None
        for j in range(FFN_CONV):
            term = _shift_rows(u, FFN_CONV // 2 - j)[HALO:HALO + tm] * both(cw_ref, slice(j, j + 1), c0)
            y = term if y is None else y + term
        y = y + both(cb_ref, slice(None), c0)
        act_ref[:, c0:c0 + FF_CHUNK] = (_silu(y[:, :FF_CHUNK]) * y[:, FF_CHUNK:]).astype(BF16)
    o_ref[0] = xc + _dot(act_ref[...], wd_ref[...])


def _ffn(x1, g, wu, cw, cb, wd):
    B, S, _ = x1.shape
    tm = FFN_ROW_TILE
    nt = S // tm
    hb = tm // HALO
    return pl.pallas_call(
        _ffn_kernel,
        grid=(B, nt),
        in_specs=[pl.BlockSpec((1, HALO, D_MODEL), lambda b, i: (b, jnp.maximum(i * hb - 1, 0), 0)),
                  pl.BlockSpec((1, tm, D_MODEL), lambda b, i: (b, i, 0)),
                  pl.BlockSpec((1, HALO, D_MODEL), lambda b, i: (b, jnp.minimum((i + 1) * hb, S // HALO - 1), 0)),
                  _resident((1, D_MODEL)), _resident((D_MODEL, 2 * D_FF)), _resident((FFN_CONV, 2 * D_FF)),
                  _resident((1, 2 * D_FF)), _resident((D_FF, D_MODEL))],
        out_specs=pl.BlockSpec((1, tm, D_MODEL), lambda b, i: (b, i, 0)),
        out_shape=jax.ShapeDtypeStruct((B, S, D_MODEL), F32),
        scratch_shapes=[pltpu.VMEM((tm, D_FF), BF16)],
        compiler_params=_cparams(("parallel", "parallel")),
        name="ffn",
    )(x1, x1, x1, g, wu, cw, cb, wd)


def _permute_w_in(w):
    idx = np.cumsum((0,) + IN_SPLITS)
    wb = w.astype(BF16)
    pad = jnp.zeros((w.shape[0], LANES - 4 * DN_HEADS), BF16)
    main = jnp.concatenate([wb[:, :idx[7]], wb[:, idx[9]:idx[10]], wb[:, idx[7]:idx[9]], pad], axis=1)
    return main, wb[:, idx[10]:]


def _layer(x, mem, rel_bias_table, p):
    B, S, D = x.shape
    n = B * S
    x2 = x.reshape(n, D)
    row = lambda a: a.reshape(1, -1).astype(F32)
    tile2 = lambda a: jnp.tile(a.astype(F32), 2)[None]
    w_main, w_gate = _permute_w_in(p["w_in"])
    aq, akv, dz, mq, qkvn, betax, gcx, gct = _inproj(
        x, row(p["norm_mix_g"]), w_main, tile2(p["attn_q_norm_g"]), tile2(p["attn_k_norm_g"]),
        p["dn_conv_w"], p["dn_a_log"], p["dn_dt_bias"])
    y_attn = _attn(aq, akv, rel_bias_table, p["attn_sink"])
    o_f, o_b = _dn_scan(*_dn_chunk(qkvn, betax, gcx, gct))
    mk, mv = _mem_kv(mem.reshape(B * N_MEM, D), row(p["mem_norm_g"]), p["mem_w_kv"].astype(BF16),
                     row(p["mem_k_norm_g"]))
    og2 = jnp.tile(p["dn_out_norm_g"].astype(F32), 2)[None]
    x1 = _merge(x2, y_attn.reshape(n, -1), o_f.reshape(n, -1), o_b.reshape(n, -1), dz.reshape(n, -1),
                mq.reshape(n, -1), mk.reshape(B, N_MEM, -1), mv.reshape(B, N_MEM, -1), row(p["mem_q_norm_g"]),
                row(p["norm_mix_g"]), w_gate, og2, p["w_br_attn"].astype(BF16), p["w_br_dn"].astype(BF16),
                p["w_br_mem"].astype(BF16), p["w_out"].astype(BF16))
    return _ffn(x1.reshape(B, S, D), row(p["norm_ffn_g"]), p["ffn_w_up"].astype(BF16), p["ffn_conv_w"].astype(F32),
                row(p["ffn_conv_b"]), p["ffn_w_down"].astype(BF16))


_LAYER_PARAMS = ("norm_mix_g", "w_in", "attn_q_norm_g", "attn_k_norm_g", "attn_sink", "dn_conv_w", "dn_a_log",
                 "dn_dt_bias", "dn_out_norm_g", "mem_norm_g", "mem_w_kv", "mem_q_norm_g", "mem_k_norm_g",
                 "w_br_attn", "w_br_dn", "w_br_mem", "w_out", "norm_ffn_g", "ffn_w_up", "ffn_conv_w", "ffn_conv_b",
                 "ffn_w_down")


def kernel(x, mem, rel_bias_table, norm_mix_g, w_in, attn_q_norm_g, attn_k_norm_g, attn_sink, dn_conv_w, dn_a_log,
           dn_dt_bias, dn_out_norm_g, mem_norm_g, mem_w_kv, mem_q_norm_g, mem_k_norm_g, w_br_attn, w_br_dn,
           w_br_mem, w_out, norm_ffn_g, ffn_w_up, ffn_conv_w, ffn_conv_b, ffn_w_down):
    stacked = dict(zip(_LAYER_PARAMS, (norm_mix_g, w_in, attn_q_norm_g, attn_k_norm_g, attn_sink, dn_conv_w,
                                       dn_a_log, dn_dt_bias, dn_out_norm_g, mem_norm_g, mem_w_kv, mem_q_norm_g,
                                       mem_k_norm_g, w_br_attn, w_br_dn, w_br_mem, w_out, norm_ffn_g, ffn_w_up,
                                       ffn_conv_w, ffn_conv_b, ffn_w_down)))
    depth = w_in.shape[0]
    for l in range(depth):
        x = _layer(x, mem, rel_bias_table, {k: v[l] for k, v in stacked.items()})
    return x
```

```python
import functools
import math

import numpy as np
import jax
import jax.numpy as jnp
from jax import lax
from jax.experimental import pallas as pl
from jax.experimental.pallas import tpu as pltpu

F32 = jnp.float32
BF16 = jnp.bfloat16

EPS = 1e-6
D_MODEL = 1024
N_MEM = 256
ATTN_HEADS = 8
ATTN_KV_HEADS = 2
ATTN_HEAD_DIM = 64
WINDOW = 128
ATTN_BLOCK = 128
REL_BUCKETS = 32
REL_MAX_DIST = 128
DN_HEADS = 8
DN_KEY_DIM = 64
DN_VALUE_DIM = 64
DN_CONV = 5
DN_CHUNK = 64
MEM_HEADS = 4
MEM_HEAD_DIM = 128
D_FF = 2816
FFN_CONV = 3
N_BRANCH = 3

ATTN_Q = ATTN_HEADS * ATTN_HEAD_DIM
ATTN_KV = ATTN_KV_HEADS * ATTN_HEAD_DIM
DN_QK = DN_HEADS * DN_KEY_DIM
DN_V = DN_HEADS * DN_VALUE_DIM
MEM_Q = MEM_HEADS * MEM_HEAD_DIM
IN_SPLITS = (ATTN_Q, ATTN_KV, ATTN_KV, DN_QK, DN_QK, DN_V, DN_V, 2 * DN_HEADS, 2 * DN_HEADS, MEM_Q,
             N_BRANCH * D_MODEL)

LANES = 128
HALF = 64
HALO = 8
NEG = -1e30
VMEM_LIMIT = 56 * 1024 * 1024

ROW_TILE = 1024
INPROJ_ROW_TILE = 1024
FFN_ROW_TILE = 1024
FF_CHUNK = 256


LOG2E = math.log2(math.e)


def _cparams(sem):
    return pltpu.CompilerParams(dimension_semantics=sem, vmem_limit_bytes=VMEM_LIMIT)


def _resident(shape):
    zeros = (0,) * len(shape)
    return pl.BlockSpec(shape, lambda *_: zeros, pipeline_mode=pl.Buffered(1))


def _dot(a, b):
    return jnp.dot(a, b, preferred_element_type=F32)


def _dot_nt(a, b):
    return lax.dot_general(a, b, (((1,), (1,)), ((), ())), preferred_element_type=F32)


def _dot_tn(a, b):
    return lax.dot_general(a, b, (((0,), (0,)), ((), ())), preferred_element_type=F32)


def _lane_is_low(shape):
    lane = lax.broadcasted_iota(jnp.int32, shape, len(shape) - 1)
    return (lane % LANES) < HALF


def _half_sums(sq):
    low = _lane_is_low(sq.shape)
    s_lo = jnp.sum(jnp.where(low, sq, 0.0), axis=-1, keepdims=True)
    s_hi = jnp.sum(jnp.where(low, 0.0, sq), axis=-1, keepdims=True)
    return jnp.where(low, s_lo, s_hi)


SUBLANES = 8


def _shift_rows(x, s):
    if s == 0:
        return x
    rr, cc = x.shape
    x3 = x.reshape(rr // SUBLANES, SUBLANES, cc)
    rot = pltpu.roll(x3, s % SUBLANES, axis=1)
    sub = lax.broadcasted_iota(jnp.int32, x3.shape, 1)
    if s > 0:
        other = jnp.concatenate([rot[-1:], rot[:-1]], axis=0)
        y3 = jnp.where(sub < s, other, rot)
    else:
        other = jnp.concatenate([rot[1:], rot[:1]], axis=0)
        y3 = jnp.where(sub >= SUBLANES + s, other, rot)
    return y3.reshape(rr, cc)


def _silu(x):
    h = 0.5 * x
    return h * jnp.tanh(h) + h


def _sigmoid(x):
    return 0.5 * jnp.tanh(0.5 * x) + 0.5


_C_AQ = (0, 512)
_C_AKV = (512, 768)
_C_DQKV = (768, 2304)
_C_DZ = (2304, 2816)
_N_IN = 2816


def _head_rmsnorm(t, gain2):
    return t * lax.rsqrt(_half_sums(t * t) * (1.0 / ATTN_HEAD_DIM) + EPS) * gain2


def _split_hi_lo(x):
    hi = x.astype(BF16)
    lo = (x - hi.astype(F32)).astype(BF16)
    return hi, lo


def _inproj_kernel(xp_ref, xc_ref, xn_ref, g_ref, w_ref, wmq_ref, wba_ref, qg_ref, kg_ref, cw_ref, alog_ref,
                   dtb_ref, tri_ref, expand_ref, aq_ref, akv_ref, dz_ref, mq_ref, qkv_ref, betax_ref, gcx_ref, gct_ref):
    i = pl.program_id(1)
    nt = pl.num_programs(1)
    tm = xc_ref.shape[1]
    prev = jnp.where(i == 0, 0.0, xp_ref[0])
    nxt = jnp.where(i == nt - 1, 0.0, xn_ref[0])
    xe = jnp.concatenate([prev, xc_ref[0], nxt], axis=0)
    ms = jnp.mean(xe * xe, axis=-1, keepdims=True)
    he = ((xe * lax.rsqrt(ms + EPS)) * g_ref[...]).astype(BF16)
    h = he[HALO:HALO + tm]

    def proj(c):
        return _dot(h, w_ref[:, c[0]:c[1]])

    cw_chunk = 2 * LANES

    def dn_qkv_chunk(c0):
        ue = _dot(he, w_ref[:, _C_DQKV[0] + c0:_C_DQKV[0] + c0 + cw_chunk])
        acc = None
        for j in range(DN_CONV):
            term = _shift_rows(ue, DN_CONV // 2 - j)[HALO:HALO + tm] * cw_ref[j:j + 1, c0:c0 + cw_chunk]
            acc = term if acc is None else acc + term
        y = _silu(acc)
        for l0 in range(0, cw_chunk, LANES):
            ys = y[:, l0:l0 + LANES]
            if c0 < 2 * DN_QK:
                ys = ys * lax.rsqrt(_half_sums(ys * ys) + EPS)
            if c0 < DN_QK:
                ys = ys * (DN_KEY_DIM ** -0.5)
            qkv_ref[0, :, c0 + l0:c0 + l0 + LANES] = ys.astype(BF16)

    def dn_beta_decay():
        ba = _dot(h, wba_ref[...])
        nh2 = 2 * DN_HEADS
        beta = _sigmoid(ba[:, :nh2])
        z = ba[:, nh2:2 * nh2] + dtb_ref[...]
        sp = jnp.maximum(z, 0.0) + jnp.log1p(jnp.exp(-jnp.abs(z)))
        g = -jnp.exp(alog_ref[...]) * sp
        g_hi, g_lo = _split_hi_lo(g)
        gg = jnp.concatenate([g_hi, g_lo], axis=1)
        tb = tri_ref.shape[1]
        pre = jnp.concatenate([_dot(tri_ref[0], gg[r0:r0 + tb]) for r0 in range(0, tm, tb)], axis=0)
        suf = jnp.concatenate([_dot(tri_ref[1], gg[r0:r0 + tb]) for r0 in range(0, tm, tb)], axis=0)
        lane16 = lax.broadcasted_iota(jnp.int32, (tm, nh2), 1)
        gc = jnp.where(lane16 < DN_HEADS, pre[:, :nh2] + pre[:, nh2:], suf[:, :nh2] + suf[:, nh2:])
        b_hi, b_lo = _split_hi_lo(beta)
        c_hi, c_lo = _split_hi_lo(gc)
        c_lo2 = (gc - c_hi.astype(F32) - c_lo.astype(F32)).astype(BF16)
        bx = _dot(jnp.concatenate([b_hi, b_lo, c_hi, c_lo, c_lo2], axis=1), expand_ref[...])
        betax_ref[0] = bx[:, :nh2 * HALF]
        gcx_ref[0] = bx[:, nh2 * HALF:]
        gct = jnp.concatenate([gc, jnp.zeros((tm, LANES - nh2), F32)], axis=1).T
        for c in range(tm // DN_CHUNK):
            gct_ref[0, c] = gct[:nh2, c * DN_CHUNK:(c + 1) * DN_CHUNK]

    def attn_q():
        aq = proj(_C_AQ)
        q_scale = ATTN_HEAD_DIM ** -0.5 * LOG2E
        for pc in range(ATTN_Q // LANES):
            sl = slice(pc * LANES, (pc + 1) * LANES)
            aq_ref[0, :, sl] = (_head_rmsnorm(aq[:, sl], qg_ref[...]) * q_scale).astype(BF16)

    def attn_kv():
        akv = proj(_C_AKV)
        kn = _head_rmsnorm(akv[:, :LANES], kg_ref[...])
        av = akv[:, LANES:]
        akv_ref[0, :, 0 * LANES:1 * LANES] = kn.astype(BF16)
        akv_ref[0, :, 1 * LANES:2 * LANES] = av.astype(BF16)
        akv_ref[0, :, 2 * LANES:3 * LANES] = pltpu.roll(kn, HALF, axis=1).astype(BF16)
        akv_ref[0, :, 3 * LANES:4 * LANES] = pltpu.roll(av, HALF, axis=1).astype(BF16)

    def dn_z():
        dz_ref[0] = proj(_C_DZ).astype(BF16)

    def mem_q():
        mq_ref[0] = _dot(h, wmq_ref[...]).astype(BF16)

    for c0 in range(0, 2 * DN_QK + DN_V, cw_chunk):
        dn_qkv_chunk(c0)
    for part in (dn_beta_decay, attn_q, attn_kv, dn_z, mem_q):
        part()


def _inproj(x, g, w, w_mq, w_ba, qg2, kg2, conv_w, a_log, dt_bias):
    B, S, D = x.shape
    tm = INPROJ_ROW_TILE
    nt = S // tm
    hb = tm // HALO
    nh2 = 2 * DN_HEADS
    tb = 2 * LANES
    r = np.arange(tb)
    same = (r[:, None] // DN_CHUNK) == (r[None, :] // DN_CHUNK)
    tri = np.stack([same & (r[:, None] >= r[None, :]), same & (r[:, None] <= r[None, :])]).astype(np.float32)
    rep = np.repeat(np.eye(nh2, dtype=np.float32), HALF, axis=1)
    zero = np.zeros_like(rep)
    expand = np.block([[rep, zero]] * 2 + [[zero, rep]] * 3)
    blk = lambda w_: pl.BlockSpec((1, tm, w_), lambda b, i: (b, i, 0))
    outs = [(ATTN_Q, BF16), (4 * LANES, BF16), (DN_V, BF16), (MEM_Q, BF16), (2 * DN_QK + DN_V, BF16),
            (nh2 * HALF, F32), (nh2 * HALF, F32)]
    return pl.pallas_call(
        _inproj_kernel,
        grid=(B, nt),
        in_specs=[pl.BlockSpec((1, HALO, D), lambda b, i: (b, jnp.maximum(i * hb - 1, 0), 0)),
                  blk(D),
                  pl.BlockSpec((1, HALO, D), lambda b, i: (b, jnp.minimum((i + 1) * hb, S // HALO - 1), 0)),
                  _resident((1, D)), _resident((D, _N_IN)), _resident((D, MEM_Q)), _resident((D, 2 * nh2)),
                  _resident((1, LANES)), _resident((1, LANES)),
                  _resident((DN_CONV, 2 * DN_QK + DN_V)), _resident((1, nh2)), _resident((1, nh2)),
                  _resident((2, tb, tb)), _resident((5 * nh2, 2 * nh2 * HALF))],
        out_specs=[blk(w_) for w_, _ in outs]
        + [pl.BlockSpec((1, tm // DN_CHUNK, nh2, DN_CHUNK), lambda b, i: (b, i, 0, 0))],
        out_shape=[jax.ShapeDtypeStruct((B, S, w_), dt) for w_, dt in outs]
        + [jax.ShapeDtypeStruct((B, S // DN_CHUNK, nh2, DN_CHUNK), F32)],
        compiler_params=_cparams(("parallel", "parallel")),
        name="inproj",
    )(x, x, x, g, w, w_mq, w_ba, qg2, kg2, conv_w.astype(F32), a_log.reshape(1, nh2).astype(F32),
      dt_bias.reshape(1, nh2).astype(F32), jnp.asarray(tri, BF16), jnp.asarray(expand, BF16))


def _t5_buckets(rel):
    nb = REL_BUCKETS // 2
    max_exact = nb // 2
    ret = (rel > 0).astype(np.int32) * nb
    n = np.abs(rel)
    large = max_exact + (np.log(np.maximum(n, 1) / max_exact) / np.log(REL_MAX_DIST / max_exact)
                         * (nb - max_exact)).astype(np.int32)
    large = np.minimum(large, nb - 1)
    return (ret + np.where(n < max_exact, n, large)).astype(np.int32)


_ATTN_GROUPS = ((0, 1, True, False), (0, 1, False, True), (2, 3, True, True), (2, 3, False, False))
_ATTN_GROUP_HEADS = ((0, 2), (1, 3), (4, 6), (5, 7))
ATTN_BLOCKS_PER_STEP = 8
ATTN_WAVE_BLOCKS = 2


def _attn_kernel(q_ref, kp_ref, kc_ref, kn_ref, bias_ref, sink_ref, o_ref):
    T = ATTN_BLOCK
    nq = ATTN_BLOCKS_PER_STEP
    first_blk = pl.program_id(1) * nq
    last_blk = pl.num_programs(1) * nq - 1
    kv_all = jnp.concatenate([kp_ref[0], kc_ref[0], kn_ref[0]], axis=0)
    ones = jnp.ones((3 * T, LANES), BF16)
    low_q = _lane_is_low((T, LANES))
    zero = jnp.zeros((T, LANES), BF16)
    groups = range(len(_ATTN_GROUPS))
    kvs, scores, maxes, res = {}, {}, {}, {}

    def score_stage(blocks):
        for qb in blocks:
            kv = kv_all[qb * T:(qb + 3) * T]
            kvs[qb, False] = (kv[:, 0 * LANES:1 * LANES],
                              jnp.concatenate([kv[:, 1 * LANES:2 * LANES], ones], axis=1))
            kvs[qb, True] = (kv[:, 2 * LANES:3 * LANES],
                             jnp.concatenate([kv[:, 3 * LANES:4 * LANES], ones], axis=1))
            rq = slice(qb * T, (qb + 1) * T)
            blk = first_blk + qb
            edge = jnp.where(blk == 0, 0, jnp.where(blk == last_blk, 2, 1))
            for gi in groups:
                pa, pb, low, swapped = _ATTN_GROUPS[gi]
                sel = low_q if low else jnp.logical_not(low_q)
                lhs = jnp.concatenate([jnp.where(sel, q_ref[0, rq, pa * LANES:(pa + 1) * LANES], zero),
                                       jnp.where(sel, q_ref[0, rq, pb * LANES:(pb + 1) * LANES], zero)], axis=0)
                scores[qb, gi] = _dot_nt(lhs, kvs[qb, swapped][0]) + bias_ref[edge, gi]

    def softmax_stages(blocks):
        probs = [(qb, gi) for qb in blocks for gi in groups]
        for qb, gi in probs:
            s = scores.pop((qb, gi))
            m = jnp.maximum(jnp.max(s, axis=-1, keepdims=True), sink_ref[gi])
            maxes[qb, gi] = m
            scores[qb, gi] = jnp.exp2(s - m).astype(BF16)
        for qb, gi in probs:
            res[qb, gi] = _dot(scores.pop((qb, gi)), kvs[qb, _ATTN_GROUPS[gi][3]][1])
        for qb, gi in probs:
            r = res[qb, gi]
            den = r[:, LANES:] + jnp.exp2(sink_ref[gi] - maxes[qb, gi])
            res[qb, gi] = r[:, :LANES] * (1.0 / den)
        for qb in blocks:
            rq = slice(qb * T, (qb + 1) * T)
            for pc, (ge, go) in enumerate(((0, 1), (0, 1), (2, 3), (2, 3))):
                r0 = (pc % 2) * T
                out = jnp.where(low_q, res[qb, ge][r0:r0 + T], res[qb, go][r0:r0 + T])
                o_ref[0, rq, pc * LANES:(pc + 1) * LANES] = out.astype(BF16)

    waves = [range(w0, min(w0 + ATTN_WAVE_BLOCKS, nq)) for w0 in range(0, nq, ATTN_WAVE_BLOCKS)]
    score_stage(waves[0])
    for w, blocks in enumerate(waves):
        if w + 1 < len(waves):
            score_stage(waves[w + 1])
        softmax_stages(blocks)


def _attn(aq, akv, rel_table, sink):
    B, S, _ = aq.shape
    T = ATTN_BLOCK
    nb = S // T
    assert nb >= 2
    t_idx = np.arange(T)[:, None]
    j_idx = np.arange(3 * T)[None, :]
    rel = j_idx - T - t_idx
    onehot = jnp.asarray(np.eye(REL_BUCKETS, dtype=np.float32)[_t5_buckets(rel)])
    bias = jnp.einsum("tjr,rh->htj", onehot, rel_table.astype(F32), precision=lax.Precision.HIGHEST) * LOG2E
    in_win = np.abs(rel) <= WINDOW
    edge_ok = np.stack([in_win & (j_idx >= T), in_win, in_win & (j_idx < 2 * T)])
    bias = jnp.where(jnp.asarray(edge_ok)[:, None], bias[None], NEG)
    bias_g = jnp.stack([jnp.concatenate([bias[:, a], bias[:, b]], axis=1) for a, b in _ATTN_GROUP_HEADS], axis=1)
    sk = sink.astype(F32) * LOG2E
    sink_g = jnp.stack([jnp.concatenate([jnp.full((T, 1), 1.0) * sk[a], jnp.full((T, 1), 1.0) * sk[b]], axis=0)
                        for a, b in _ATTN_GROUP_HEADS])
    kv_w = akv.shape[-1]
    nq = ATTN_BLOCKS_PER_STEP
    assert nb % nq == 0
    return pl.pallas_call(
        _attn_kernel,
        grid=(B, nb // nq),
        in_specs=[pl.BlockSpec((1, nq * T, ATTN_Q), lambda b, i: (b, i, 0)),
                  pl.BlockSpec((1, T, kv_w), lambda b, i: (b, jnp.maximum(i * nq - 1, 0), 0)),
                  pl.BlockSpec((1, nq * T, kv_w), lambda b, i: (b, i, 0)),
                  pl.BlockSpec((1, T, kv_w), lambda b, i: (b, jnp.minimum((i + 1) * nq, nb - 1), 0)),
                  _resident((3, 4, 2 * T, 3 * T)), _resident((4, 2 * T, 1))],
        out_specs=pl.BlockSpec((1, nq * T, ATTN_Q), lambda b, i: (b, i, 0)),
        out_shape=jax.ShapeDtypeStruct((B, S, ATTN_Q), BF16),
        compiler_params=_cparams(("parallel", "parallel")),
        name="attn",
    )(aq, akv, akv, akv, bias_g, sink_g)


DN_TILE_HEADS = 2
DN_TILE = DN_TILE_HEADS * HALF
DN_TILES = DN_HEADS // DN_TILE_HEADS
DN_PAIRS = DN_HEADS // 2
DN_CHUNKS_PER_STEP = 8
DN_WAVE_CHUNKS = 2
DN_SCAN_CHUNKS = 16


def _head_of_lane(shape):
    return lax.broadcasted_iota(jnp.int32, shape, len(shape) - 1) // HALF


def _block_diag(x, head):
    zero = jnp.zeros_like(x)
    return jnp.concatenate([jnp.where(head == h, x, zero) for h in range(DN_TILE_HEADS)], axis=0)


def _dn_chunk_kernel(qkv_ref, bx_ref, gx_ref, gt_ref, u_ref, w_ref, qkd_ref, qin_ref, kupt_ref, egt_ref):
    C = DN_CHUNK
    TW = DN_TILE
    head = _head_of_lane((C, TW))
    r = lax.broadcasted_iota(jnp.int32, (C, TW), 0)
    m = lax.broadcasted_iota(jnp.int32, (C, TW), 1) % HALF
    eye = jnp.where(r == m, 1.0, 0.0)

    def prepare(ch):
        probs = []
        rows = slice(ch * C, (ch + 1) * C)
        for tl in range(DN_TILES):
            q4 = qkv_ref[0, rows, tl * TW:(tl + 1) * TW]
            k4 = qkv_ref[0, rows, DN_QK + tl * TW:DN_QK + (tl + 1) * TW]
            v4 = qkv_ref[0, rows, 2 * DN_QK + tl * TW:2 * DN_QK + (tl + 1) * TW]
            q4f, k4f, v4f = q4.astype(F32), k4.astype(F32), v4.astype(F32)
            qkk = _dot_nt(jnp.concatenate([q4, k4], axis=0), _block_diag(k4, head))
            qk, kk = qkk[:C], qkk[C:]
            for d in range(2):
                col = slice(d * DN_QK + tl * TW, d * DN_QK + (tl + 1) * TW)
                beta4 = bx_ref[0, rows, col]
                gcol4 = gx_ref[0, rows, col]
                h0 = d * DN_HEADS + tl * DN_TILE_HEADS
                grow4 = jnp.concatenate([gt_ref[0, ch, h0 + h:h0 + h + 1, :] for h in range(DN_TILE_HEADS)],
                                        axis=1)
                incl = (r >= m) if d == 0 else (r <= m)
                strict = (r > m) if d == 0 else (r < m)
                decay = jnp.where(incl, jnp.exp(jnp.where(incl, gcol4 - grow4, 0.0)), 0.0)
                gtot4 = gcol4[C - 1:C] if d == 0 else gcol4[0:1]
                e_col = jnp.exp(gcol4)
                p = jnp.where(strict, -(kk * beta4 * decay), 0.0)
                vb = (v4f * beta4).astype(BF16)
                kbe = (k4f * beta4 * e_col).astype(BF16)
                rhs = jnp.concatenate([_block_diag(vb, head), _block_diag(kbe, head)], axis=1)
                qkd_ref[0, rows, col] = (qk * decay).astype(BF16)
                qin_ref[0, rows, col] = (q4f * e_col).astype(BF16)
                kupt_ref[0, ch, d, tl] = (k4f * jnp.exp(gtot4 - gcol4)).T.astype(BF16)
                egt_ref[0, ch, d, tl:tl + 1, :] = jnp.exp(gtot4)
                probs.append([p, eye + p, rhs, rows, col])
        return probs

    def solve(probs):
        for pb in probs:
            pb[0] = _dot(pb[0].astype(BF16), _block_diag(pb[0].astype(BF16), head))
        for _ in range(4):
            for pb in probs:
                both = _dot(jnp.concatenate([pb[1], pb[0]], axis=0).astype(BF16),
                            _block_diag(pb[0].astype(BF16), head))
                pb[1] = pb[1] + both[:C]
                pb[0] = both[C:]
        for p, t, rhs, rows, col in probs:
            t = t + _dot(t.astype(BF16), _block_diag(p.astype(BF16), head))
            sol = _dot(t.astype(BF16), rhs)
            u_ref[0, rows, col] = sol[:, :TW].astype(BF16)
            w_ref[0, rows, col] = sol[:, TW:].astype(BF16)

    waves = [range(c0, min(c0 + DN_WAVE_CHUNKS, DN_CHUNKS_PER_STEP))
             for c0 in range(0, DN_CHUNKS_PER_STEP, DN_WAVE_CHUNKS)]
    ready = [pb for ch in waves[0] for pb in prepare(ch)]
    for w in range(len(waves)):
        upcoming = [pb for ch in waves[w + 1] for pb in prepare(ch)] if w + 1 < len(waves) else []
        solve(ready)
        ready = upcoming


def _dn_chunk(qkvn, betax, gcx, gct):
    B, S, W = qkvn.shape
    C = DN_CHUNK
    cps = DN_CHUNKS_PER_STEP
    nc = S // C
    wide = 2 * DN_QK
    blk = lambda w_: pl.BlockSpec((1, cps * C, w_), lambda b, i: (b, i, 0))
    return pl.pallas_call(
        _dn_chunk_kernel,
        grid=(B, nc // cps),
        in_specs=[blk(W), blk(wide), blk(wide),
                  pl.BlockSpec((1, cps, 2 * DN_HEADS, C), lambda b, i: (b, i, 0, 0))],
        out_specs=[blk(wide), blk(wide), blk(wide), blk(wide),
                   pl.BlockSpec((1, cps, 2, DN_TILES, DN_TILE, C), lambda b, i: (b, i, 0, 0, 0, 0)),
                   pl.BlockSpec((1, cps, 2, DN_TILES, DN_TILE), lambda b, i: (b, i, 0, 0, 0))],
        out_shape=[jax.ShapeDtypeStruct((B, S, wide), BF16)] * 4
        + [jax.ShapeDtypeStruct((B, nc, 2, DN_TILES, DN_TILE, C), BF16),
           jax.ShapeDtypeStruct((B, nc, 2, DN_TILES, DN_TILE), F32)],
        compiler_params=_cparams(("parallel", "parallel")),
        name="dn_chunk",
    )(qkvn, betax, gcx, gct)


def _dn_scan_kernel(uf, wf, qkdf, qinf, kuptf, egtf, ub, wb, qkdb, qinb, kuptb, egtb, of_ref, ob_ref, st_ref):
    @pl.when(pl.program_id(1) == 0)
    def _():
        st_ref[...] = jnp.zeros_like(st_ref)

    C = DN_CHUNK
    TW = DN_TILE
    head = _head_of_lane((C, TW))
    rr = lax.broadcasted_iota(jnp.int32, (LANES, LANES), 0) < HALF
    cc = lax.broadcasted_iota(jnp.int32, (LANES, LANES), 1) < HALF
    on_diag = rr == cc
    zpad = jnp.zeros((LANES, LANES), BF16)
    ppt = TW // LANES
    dirs = ((uf, wf, qkdf, qinf, kuptf, egtf, of_ref), (ub, wb, qkdb, qinb, kuptb, egtb, ob_ref))
    tile_ids = [(d, tl) for d in range(2) for tl in range(DN_TILES)]
    state = {(d, pr): st_ref[d, pr] for d in range(2) for pr in range(DN_PAIRS)}
    for step in range(DN_SCAN_CHUNKS):
        ws, vb = {}, {}
        for d, tl in tile_ids:
            u_ref, w_ref, qkd_ref, qin_ref, kupt_ref, egt_ref, o_ref = dirs[d]
            ch = step if d == 0 else DN_SCAN_CHUNKS - 1 - step
            rows = slice(ch * C, (ch + 1) * C)
            sl = slice(tl * TW, (tl + 1) * TW)
            lhs = jnp.concatenate([w_ref[0, rows, sl], qin_ref[0, rows, sl]], axis=0)
            blocks = [state[d, ppt * tl + i].astype(BF16) for i in range(ppt)]
            s_tile = jnp.concatenate(
                [jnp.concatenate([blocks[i] if j == i else zpad for j in range(ppt)], axis=1) for i in range(ppt)],
                axis=0) if ppt > 1 else blocks[0]
            ws[d, tl] = _dot(lhs, s_tile)
        for d, tl in tile_ids:
            u_ref = dirs[d][0]
            ch = step if d == 0 else DN_SCAN_CHUNKS - 1 - step
            rows = slice(ch * C, (ch + 1) * C)
            sl = slice(tl * TW, (tl + 1) * TW)
            vb[d, tl] = (u_ref[0, rows, sl].astype(F32) - ws[d, tl][:C]).astype(BF16)
        for d, tl in tile_ids:
            u_ref, w_ref, qkd_ref, qin_ref, kupt_ref, egt_ref, o_ref = dirs[d]
            ch = step if d == 0 else DN_SCAN_CHUNKS - 1 - step
            rows = slice(ch * C, (ch + 1) * C)
            sl = slice(tl * TW, (tl + 1) * TW)
            for part in range(ppt):
                ls = slice(part * LANES, (part + 1) * LANES)
                upd = _dot(kupt_ref[0, ch, 0, tl, ls, :], vb[d, tl][:, ls])
                pr = ppt * tl + part
                state[d, pr] = state[d, pr] * egt_ref[0, ch, 0, tl:tl + 1, ls] + jnp.where(on_diag, upd, 0.0)
            o4 = ws[d, tl][C:] + _dot(qkd_ref[0, rows, sl], _block_diag(vb[d, tl], head))
            o_ref[0, rows, sl] = o4.astype(BF16)
    for dp, s in state.items():
        st_ref[dp[0], dp[1]] = s


def _dn_scan(u, w, qkd, qin, kupt, egt):
    B, S, _ = u.shape
    C = DN_CHUNK
    sc = DN_SCAN_CHUNKS
    nb = S // (C * sc)
    fwd3 = lambda b, i: (b, i, 0)
    bwd3 = lambda b, i: (b, nb - 1 - i, 1)
    row_f = pl.BlockSpec((1, sc * C, DN_QK), fwd3)
    row_b = pl.BlockSpec((1, sc * C, DN_QK), bwd3)
    kup_f = pl.BlockSpec((1, sc, 1, DN_TILES, DN_TILE, C), lambda b, i: (b, i, 0, 0, 0, 0))
    kup_b = pl.BlockSpec((1, sc, 1, DN_TILES, DN_TILE, C), lambda b, i: (b, nb - 1 - i, 1, 0, 0, 0))
    egt_f = pl.BlockSpec((1, sc, 1, DN_TILES, DN_TILE), lambda b, i: (b, i, 0, 0, 0))
    egt_b = pl.BlockSpec((1, sc, 1, DN_TILES, DN_TILE), lambda b, i: (b, nb - 1 - i, 1, 0, 0))
    return pl.pallas_call(
        _dn_scan_kernel,
        grid=(B, nb),
        in_specs=[row_f, row_f, row_f, row_f, kup_f, egt_f, row_b, row_b, row_b, row_b, kup_b, egt_b],
        out_specs=[pl.BlockSpec((1, sc * C, DN_V), fwd3),
                   pl.BlockSpec((1, sc * C, DN_V), lambda b, i: (b, nb - 1 - i, 0))],
        out_shape=[jax.ShapeDtypeStruct((B, S, DN_V), BF16), jax.ShapeDtypeStruct((B, S, DN_V), BF16)],
        scratch_shapes=[pltpu.VMEM((2, DN_PAIRS, LANES, LANES), F32)],
        compiler_params=_cparams(("parallel", "arbitrary")),
        name="dn_scan",
    )(u, w, qkd, qin, kupt, egt, u, w, qkd, qin, kupt, egt)


def _mem_kv_kernel(m_ref, g_ref, w_ref, kg_ref, k_ref, v_ref):
    x = m_ref[...]
    ms = jnp.mean(x * x, axis=-1, keepdims=True)
    h = ((x * lax.rsqrt(ms + EPS)) * g_ref[...]).astype(BF16)
    kv = _dot(h, w_ref[...])
    for hd in range(MEM_HEADS):
        kh = kv[:, hd * LANES:(hd + 1) * LANES]
        kms = jnp.mean(kh * kh, axis=-1, keepdims=True)
        k_ref[:, hd * LANES:(hd + 1) * LANES] = ((kh * lax.rsqrt(kms + EPS)) * kg_ref[...]).astype(BF16)
    v_ref[...] = kv[:, MEM_Q:].astype(BF16)


def _mem_kv(mem2, g, w_kv, kg):
    n = mem2.shape[0]
    return pl.pallas_call(
        _mem_kv_kernel,
        grid=(n // N_MEM,),
        in_specs=[pl.BlockSpec((N_MEM, D_MODEL), lambda i: (i, 0)),
                  pl.BlockSpec((1, D_MODEL), lambda i: (0, 0)),
                  pl.BlockSpec((D_MODEL, 2 * MEM_Q), lambda i: (0, 0)),
                  pl.BlockSpec((1, MEM_HEAD_DIM), lambda i: (0, 0))],
        out_specs=[pl.BlockSpec((N_MEM, MEM_Q), lambda i: (i, 0)), pl.BlockSpec((N_MEM, MEM_Q), lambda i: (i, 0))],
        out_shape=[jax.ShapeDtypeStruct((n, MEM_Q), BF16), jax.ShapeDtypeStruct((n, MEM_Q), BF16)],
        compiler_params=_cparams(("parallel",)),
        name="mem_kv",
    )(mem2, g, w_kv, kg)


def _mem_attend(q_ref, k_ref, v_ref, qg_ref):
    scale = MEM_HEAD_DIM ** -0.5 * LOG2E
    ones = jnp.ones((N_MEM, LANES), BF16)
    heads = []
    for hd in range(MEM_HEADS):
        sl = slice(hd * LANES, (hd + 1) * LANES)
        q = q_ref[:, sl].astype(F32)
        qms = jnp.mean(q * q, axis=-1, keepdims=True)
        qn = ((q * lax.rsqrt(qms + EPS)) * qg_ref[...] * scale).astype(BF16)
        s = _dot_nt(qn, k_ref[0, :, sl])
        p = jnp.exp2(s - jnp.max(s, axis=-1, keepdims=True))
        r = _dot(p.astype(BF16), jnp.concatenate([v_ref[0, :, sl], ones], axis=1))
        heads.append((r[:, :LANES] * (1.0 / r[:, LANES:])).astype(BF16))
    return jnp.concatenate(heads, axis=1)


def _merge_kernel(x_ref, ya_ref, of_ref, ob_ref, z_ref, mq_ref, mk_ref, mv_ref, mqg_ref, ng_ref, wg_ref, og_ref,
                  wa_ref, wd_ref, wm_ref, wo_ref, o_ref):
    y_mem = _mem_attend(mq_ref, mk_ref, mv_ref, mqg_ref)
    x = x_ref[...]
    ms = jnp.mean(x * x, axis=-1, keepdims=True)
    h = ((x * lax.rsqrt(ms + EPS)) * ng_ref[...]).astype(BF16)
    o = of_ref[...].astype(F32) + ob_ref[...].astype(F32)
    z = z_ref[...].astype(F32)
    og = og_ref[...]
    parts = []
    for pc in range(DN_V // LANES):
        os_ = o[:, pc * LANES:(pc + 1) * LANES]
        on = os_ * lax.rsqrt(_half_sums(os_ * os_) * (1.0 / DN_VALUE_DIM) + EPS) * og
        parts.append((on * _silu(z[:, pc * LANES:(pc + 1) * LANES])).astype(BF16))
    y_dn = jnp.concatenate(parts, axis=1)
    def gate(b):
        return _sigmoid(_dot(h, wg_ref[:, b * D_MODEL:(b + 1) * D_MODEL]))

    merged = (gate(0) * _dot(ya_ref[...], wa_ref[...])
              + gate(1) * _dot(y_dn, wd_ref[...])
              + gate(2) * _dot(y_mem, wm_ref[...]))
    o_ref[...] = x + _dot(merged.astype(BF16), wo_ref[...])


def _merge(x2, ya, of, ob, z, mq, mk, mv, mqg, ng, wg, og, wa, wd, wm, wo):
    n = x2.shape[0]
    tm = ROW_TILE
    tiles_per_seq = n // mk.shape[0] // tm
    row = lambda w_: pl.BlockSpec((tm, w_), lambda i: (i, 0))
    mem = pl.BlockSpec((1, N_MEM, MEM_Q), lambda i: (i // tiles_per_seq, 0, 0))
    full = lambda a, b: _resident((a, b))
    return pl.pallas_call(
        _merge_kernel,
        grid=(n // tm,),
        in_specs=[row(D_MODEL), row(ATTN_Q), row(DN_V), row(DN_V), row(DN_V), row(MEM_Q), mem, mem,
                  full(1, MEM_HEAD_DIM), full(1, D_MODEL), full(D_MODEL, N_BRANCH * D_MODEL),
                  full(1, LANES), full(ATTN_Q, D_MODEL), full(DN_V, D_MODEL), full(MEM_Q, D_MODEL),
                  full(D_MODEL, D_MODEL)],
        out_specs=row(D_MODEL),
        out_shape=jax.ShapeDtypeStruct((n, D_MODEL), F32),
        compiler_params=_cparams(("parallel",)),
        name="merge",
    )(x2, ya, of, ob, z, mq, mk, mv, mqg, ng, wg, og, wa, wd, wm, wo)


def _ffn_kernel(xp_ref, xc_ref, xn_ref, g_ref, wu_ref, cw_ref, cb_ref, wd_ref, o_ref, act_ref):
    i = pl.program_id(1)
    nt = pl.num_programs(1)
    tm = xc_ref.shape[1]
    xc = xc_ref[0]
    prev = jnp.where(i == 0, 0.0, xp_ref[0])
    nxt = jnp.where(i == nt - 1, 0.0, xn_ref[0])
    xe = jnp.concatenate([prev, xc, nxt], axis=0)
    ms = jnp.mean(xe * xe, axis=-1, keepdims=True)
    h = ((xe * lax.rsqrt(ms + EPS)) * g_ref[...]).astype(BF16)

    def both(ref, rows_, c0):
        return jnp.concatenate([ref[rows_, c0:c0 + FF_CHUNK], ref[rows_, D_FF + c0:D_FF + c0 + FF_CHUNK]], axis=1)

    for c in range(D_FF // FF_CHUNK):
        c0 = c * FF_CHUNK
        u = _dot(h, both(wu_ref, slice(None), c0))
        y = None
        for j in range(FFN_CONV):
            term = _shift_rows(u, FFN_CONV // 2 - j)[HALO:HALO + tm] * both(cw_ref, slice(j, j + 1), c0)
            y = term if y is None else y + term
        y = y + both(cb_ref, slice(None), c0)
        act_ref[:, c0:c0 + FF_CHUNK] = (_silu(y[:, :FF_CHUNK]) * y[:, FF_CHUNK:]).astype(BF16)
    o_ref[0] = xc + _dot(act_ref[...], wd_ref[...])


def _ffn(x1, g, wu, cw, cb, wd):
    B, S, _ = x1.shape
    tm = FFN_ROW_TILE
    nt = S // tm
    hb = tm // HALO
    return pl.pallas_call(
        _ffn_kernel,
        grid=(B, nt),
        in_specs=[pl.BlockSpec((1, HALO, D_MODEL), lambda b, i: (b, jnp.maximum(i * hb - 1, 0), 0)),
                  pl.BlockSpec((1, tm, D_MODEL), lambda b, i: (b, i, 0)),
                  pl.BlockSpec((1, HALO, D_MODEL), lambda b, i: (b, jnp.minimum((i + 1) * hb, S // HALO - 1), 0)),
                  _resident((1, D_MODEL)), _resident((D_MODEL, 2 * D_FF)), _resident((FFN_CONV, 2 * D_FF)),
                  _resident((1, 2 * D_FF)), _resident((D_FF, D_MODEL))],
        out_specs=pl.BlockSpec((1, tm, D_MODEL), lambda b, i: (b, i, 0)),
        out_shape=jax.ShapeDtypeStruct((B, S, D_MODEL), F32),
        scratch_shapes=[pltpu.VMEM((tm, D_FF), BF16)],
        compiler_params=_cparams(("parallel", "parallel")),
        name="ffn",
    )(x1, x1, x1, g, wu, cw, cb, wd)


def _permute_w_in(w):
    idx = np.cumsum((0,) + IN_SPLITS)
    wb = w.astype(BF16)
    return wb[:, :idx[7]], wb[:, idx[9]:idx[10]], wb[:, idx[7]:idx[9]], wb[:, idx[10]:]


def _layer(x, mem, rel_bias_table, p):
    B, S, D = x.shape
    n = B * S
    x2 = x.reshape(n, D)
    row = lambda a: a.reshape(1, -1).astype(F32)
    tile2 = lambda a: jnp.tile(a.astype(F32), 2)[None]
    w_main, w_mq, w_ba, w_gate = _permute_w_in(p["w_in"])
    aq, akv, dz, mq, qkvn, betax, gcx, gct = _inproj(
        x, row(p["norm_mix_g"]), w_main, w_mq, w_ba, tile2(p["attn_q_norm_g"]), tile2(p["attn_k_norm_g"]),
        p["dn_conv_w"], p["dn_a_log"], p["dn_dt_bias"])
    y_attn = _attn(aq, akv, rel_bias_table, p["attn_sink"])
    o_f, o_b = _dn_scan(*_dn_chunk(qkvn, betax, gcx, gct))
    mk, mv = _mem_kv(mem.reshape(B * N_MEM, D), row(p["mem_norm_g"]), p["mem_w_kv"].astype(BF16),
                     row(p["mem_k_norm_g"]))
    og2 = jnp.tile(p["dn_out_norm_g"].astype(F32), 2)[None]
    x1 = _merge(x2, y_attn.reshape(n, -1), o_f.reshape(n, -1), o_b.reshape(n, -1), dz.reshape(n, -1),
                mq.reshape(n, -1), mk.reshape(B, N_MEM, -1), mv.reshape(B, N_MEM, -1), row(p["mem_q_norm_g"]),
                row(p["norm_mix_g"]), w_gate, og2, p["w_br_attn"].astype(BF16), p["w_br_dn"].astype(BF16),
                p["w_br_mem"].astype(BF16), p["w_out"].astype(BF16))
    return _ffn(x1.reshape(B, S, D), row(p["norm_ffn_g"]), p["ffn_w_up"].astype(BF16), p["ffn_conv_w"].astype(F32),
                row(p["ffn_conv_b"]), p["ffn_w_down"].astype(BF16))


_LAYER_PARAMS = ("norm_mix_g", "w_in", "attn_q_norm_g", "attn_k_norm_g", "attn_sink", "dn_conv_w", "dn_a_log",
                 "dn_dt_bias", "dn_out_norm_g", "mem_norm_g", "mem_w_kv", "mem_q_norm_g", "mem_k_norm_g",
                 "w_br_attn", "w_br_dn", "w_br_mem", "w_out", "norm_ffn_g", "ffn_w_up", "ffn_conv_w", "ffn_conv_b",
                 "ffn_w_down")


def kernel(x, mem, rel_bias_table, norm_mix_g, w_in, attn_q_norm_g, attn_k_norm_g, attn_sink, dn_conv_w, dn_a_log,
           dn_dt_bias, dn_out_norm_g, mem_norm_g, mem_w_kv, mem_q_norm_g, mem_k_norm_g, w_br_attn, w_br_dn,
           w_br_mem, w_out, norm_ffn_g, ffn_w_up, ffn_conv_w, ffn_conv_b, ffn_w_down):
    stacked = dict(zip(_LAYER_PARAMS, (norm_mix_g, w_in, attn_q_norm_g, attn_k_norm_g, attn_sink, dn_conv_w,
                                       dn_a_log, dn_dt_bias, dn_out_norm_g, mem_norm_g, mem_w_kv, mem_q_norm_g,
                                       mem_k_norm_g, w_br_attn, w_br_dn, w_br_mem, w_out, norm_ffn_g, ffn_w_up,
                                       ffn_conv_w, ffn_conv_b, ffn_w_down)))
    depth = w_in.shape[0]
    for l in range(depth):
        x = _layer(x, mem, rel_bias_table, {k: v[l] for k, v in stacked.items()})
    return x
```

```python
import functools
import math

import numpy as np
import jax
import jax.numpy as jnp
from jax import lax
from jax.experimental import pallas as pl
from jax.experimental.pallas import tpu as pltpu

F32 = jnp.float32
BF16 = jnp.bfloat16

EPS = 1e-6
D_MODEL = 1024
N_MEM = 256
ATTN_HEADS = 8
ATTN_KV_HEADS = 2
ATTN_HEAD_DIM = 64
WINDOW = 128
ATTN_BLOCK = 128
REL_BUCKETS = 32
REL_MAX_DIST = 128
DN_HEADS = 8
DN_KEY_DIM = 64
DN_VALUE_DIM = 64
DN_CONV = 5
DN_CHUNK = 64
MEM_HEADS = 4
MEM_HEAD_DIM = 128
D_FF = 2816
FFN_CONV = 3
N_BRANCH = 3

ATTN_Q = ATTN_HEADS * ATTN_HEAD_DIM
ATTN_KV = ATTN_KV_HEADS * ATTN_HEAD_DIM
DN_QK = DN_HEADS * DN_KEY_DIM
DN_V = DN_HEADS * DN_VALUE_DIM
MEM_Q = MEM_HEADS * MEM_HEAD_DIM
IN_SPLITS = (ATTN_Q, ATTN_KV, ATTN_KV, DN_QK, DN_QK, DN_V, DN_V, 2 * DN_HEADS, 2 * DN_HEADS, MEM_Q,
             N_BRANCH * D_MODEL)

LANES = 128
HALF = 64
HALO = 8
NEG = -1e30
VMEM_LIMIT = 56 * 1024 * 1024

ROW_TILE = 1024
INPROJ_ROW_TILE = 1024
FFN_ROW_TILE = 1024
FF_CHUNK = 256


LOG2E = math.log2(math.e)


def _cparams(sem):
    return pltpu.CompilerParams(dimension_semantics=sem, vmem_limit_bytes=VMEM_LIMIT)


def _resident(shape):
    zeros = (0,) * len(shape)
    return pl.BlockSpec(shape, lambda *_: zeros, pipeline_mode=pl.Buffered(1))


def _dot(a, b):
    return jnp.dot(a, b, preferred_element_type=F32)


def _dot_nt(a, b):
    return lax.dot_general(a, b, (((1,), (1,)), ((), ())), preferred_element_type=F32)


def _dot_tn(a, b):
    return lax.dot_general(a, b, (((0,), (0,)), ((), ())), preferred_element_type=F32)


def _lane_is_low(shape):
    lane = lax.broadcasted_iota(jnp.int32, shape, len(shape) - 1)
    return (lane % LANES) < HALF


def _half_sums(sq):
    low = _lane_is_low(sq.shape)
    s_lo = jnp.sum(jnp.where(low, sq, 0.0), axis=-1, keepdims=True)
    s_hi = jnp.sum(jnp.where(low, 0.0, sq), axis=-1, keepdims=True)
    return jnp.where(low, s_lo, s_hi)


SUBLANES = 8


def _shift_rows(x, s):
    if s == 0:
        return x
    rr, cc = x.shape
    x3 = x.reshape(rr // SUBLANES, SUBLANES, cc)
    rot = pltpu.roll(x3, s % SUBLANES, axis=1)
    sub = lax.broadcasted_iota(jnp.int32, x3.shape, 1)
    if s > 0:
        other = jnp.concatenate([rot[-1:], rot[:-1]], axis=0)
        y3 = jnp.where(sub < s, other, rot)
    else:
        other = jnp.concatenate([rot[1:], rot[:1]], axis=0)
        y3 = jnp.where(sub >= SUBLANES + s, other, rot)
    return y3.reshape(rr, cc)


def _silu(x):
    h = 0.5 * x
    return h * jnp.tanh(h) + h


def _sigmoid(x):
    return 0.5 * jnp.tanh(0.5 * x) + 0.5


_C_AQ = (0, 512)
_C_AKV = (512, 768)
_C_DQKV = (768, 2304)
_C_DZ = (2304, 2816)
_N_IN = 2816


def _head_rmsnorm(t, gain2):
    return t * lax.rsqrt(_half_sums(t * t) * (1.0 / ATTN_HEAD_DIM) + EPS) * gain2


def _split_hi_lo(x):
    hi = x.astype(BF16)
    lo = (x - hi.astype(F32)).astype(BF16)
    return hi, lo


def _inproj_kernel(xp_ref, xc_ref, xn_ref, g_ref, w_ref, wmq_ref, wba_ref, qg_ref, kg_ref, cw_ref, alog_ref,
                   dtb_ref, tri_ref, expand_ref, aq_ref, akv_ref, dz_ref, mq_ref, qkv_ref, betax_ref, gcx_ref, gct_ref):
    i = pl.program_id(1)
    nt = pl.num_programs(1)
    tm = xc_ref.shape[1]
    prev = jnp.where(i == 0, 0.0, xp_ref[0])
    nxt = jnp.where(i == nt - 1, 0.0, xn_ref[0])
    xe = jnp.concatenate([prev, xc_ref[0], nxt], axis=0)
    ms = jnp.mean(xe * xe, axis=-1, keepdims=True)
    he = ((xe * lax.rsqrt(ms + EPS)) * g_ref[...]).astype(BF16)
    h = he[HALO:HALO + tm]

    def proj(c):
        return _dot(h, w_ref[:, c[0]:c[1]])

    cw_chunk = 2 * LANES

    def dn_qkv_chunk(c0):
        ue = _dot(he, w_ref[:, _C_DQKV[0] + c0:_C_DQKV[0] + c0 + cw_chunk])
        acc = None
        for j in range(DN_CONV):
            term = _shift_rows(ue, DN_CONV // 2 - j)[HALO:HALO + tm] * cw_ref[j:j + 1, c0:c0 + cw_chunk]
            acc = term if acc is None else acc + term
        y = _silu(acc)
        for l0 in range(0, cw_chunk, LANES):
            ys = y[:, l0:l0 + LANES]
            if c0 < 2 * DN_QK:
                ys = ys * lax.rsqrt(_half_sums(ys * ys) + EPS)
            if c0 < DN_QK:
                ys = ys * (DN_KEY_DIM ** -0.5)
            qkv_ref[0, :, c0 + l0:c0 + l0 + LANES] = ys.astype(BF16)

    def dn_beta_decay():
        ba = _dot(h, wba_ref[...])
        nh2 = 2 * DN_HEADS
        beta = _sigmoid(ba[:, :nh2])
        z = ba[:, nh2:2 * nh2] + dtb_ref[...]
        sp = jnp.maximum(z, 0.0) + jnp.log1p(jnp.exp(-jnp.abs(z)))
        g = -jnp.exp(alog_ref[...]) * sp
        g_hi, g_lo = _split_hi_lo(g)
        g_lo2 = (g - g_hi.astype(F32) - g_lo.astype(F32)).astype(BF16)
        gg = jnp.concatenate([g_hi, g_lo, g_lo2], axis=1)
        tb = tri_ref.shape[1]
        pre = jnp.concatenate([_dot(tri_ref[0], gg[r0:r0 + tb]) for r0 in range(0, tm, tb)], axis=0)
        suf = jnp.concatenate([_dot(tri_ref[1], gg[r0:r0 + tb]) for r0 in range(0, tm, tb)], axis=0)
        lane16 = lax.broadcasted_iota(jnp.int32, (tm, nh2), 1)
        gc = jnp.where(lane16 < DN_HEADS, pre[:, :nh2] + pre[:, nh2:2 * nh2] + pre[:, 2 * nh2:],
                       suf[:, :nh2] + suf[:, nh2:2 * nh2] + suf[:, 2 * nh2:])
        b_hi, b_lo = _split_hi_lo(beta)
        c_hi, c_lo = _split_hi_lo(gc)
        c_lo2 = (gc - c_hi.astype(F32) - c_lo.astype(F32)).astype(BF16)
        bx = _dot(jnp.concatenate([b_hi, b_lo, c_hi, c_lo, c_lo2], axis=1), expand_ref[...])
        betax_ref[0] = bx[:, :nh2 * HALF]
        gcx_ref[0] = bx[:, nh2 * HALF:]
        gct = jnp.concatenate([gc, jnp.zeros((tm, LANES - nh2), F32)], axis=1).T
        for c in range(tm // DN_CHUNK):
            gct_ref[0, c] = gct[:nh2, c * DN_CHUNK:(c + 1) * DN_CHUNK]

    def attn_q():
        aq = proj(_C_AQ)
        q_scale = ATTN_HEAD_DIM ** -0.5 * LOG2E
        for pc in range(ATTN_Q // LANES):
            sl = slice(pc * LANES, (pc + 1) * LANES)
            aq_ref[0, :, sl] = (_head_rmsnorm(aq[:, sl], qg_ref[...]) * q_scale).astype(BF16)

    def attn_kv():
        akv = proj(_C_AKV)
        kn = _head_rmsnorm(akv[:, :LANES], kg_ref[...])
        av = akv[:, LANES:]
        akv_ref[0, :, 0 * LANES:1 * LANES] = kn.astype(BF16)
        akv_ref[0, :, 1 * LANES:2 * LANES] = av.astype(BF16)
        akv_ref[0, :, 2 * LANES:3 * LANES] = pltpu.roll(kn, HALF, axis=1).astype(BF16)
        akv_ref[0, :, 3 * LANES:4 * LANES] = pltpu.roll(av, HALF, axis=1).astype(BF16)

    def dn_z():
        dz_ref[0] = proj(_C_DZ).astype(BF16)

    def mem_q():
        mq_ref[0] = _dot(h, wmq_ref[...]).astype(BF16)

    for c0 in range(0, 2 * DN_QK + DN_V, cw_chunk):
        dn_qkv_chunk(c0)
    for part in (dn_beta_decay, attn_q, attn_kv, dn_z, mem_q):
        part()


def _inproj(x, g, w, w_mq, w_ba, qg2, kg2, conv_w, a_log, dt_bias):
    B, S, D = x.shape
    tm = INPROJ_ROW_TILE
    nt = S // tm
    hb = tm // HALO
    nh2 = 2 * DN_HEADS
    tb = 2 * LANES
    r = np.arange(tb)
    same = (r[:, None] // DN_CHUNK) == (r[None, :] // DN_CHUNK)
    tri = np.stack([same & (r[:, None] >= r[None, :]), same & (r[:, None] <= r[None, :])]).astype(np.float32)
    rep = np.repeat(np.eye(nh2, dtype=np.float32), HALF, axis=1)
    zero = np.zeros_like(rep)
    expand = np.block([[rep, zero]] * 2 + [[zero, rep]] * 3)
    blk = lambda w_: pl.BlockSpec((1, tm, w_), lambda b, i: (b, i, 0))
    outs = [(ATTN_Q, BF16), (4 * LANES, BF16), (DN_V, BF16), (MEM_Q, BF16), (2 * DN_QK + DN_V, BF16),
            (nh2 * HALF, F32), (nh2 * HALF, F32)]
    return pl.pallas_call(
        _inproj_kernel,
        grid=(B, nt),
        in_specs=[pl.BlockSpec((1, HALO, D), lambda b, i: (b, jnp.maximum(i * hb - 1, 0), 0)),
                  blk(D),
                  pl.BlockSpec((1, HALO, D), lambda b, i: (b, jnp.minimum((i + 1) * hb, S // HALO - 1), 0)),
                  _resident((1, D)), _resident((D, _N_IN)), _resident((D, MEM_Q)), _resident((D, 2 * nh2)),
                  _resident((1, LANES)), _resident((1, LANES)),
                  _resident((DN_CONV, 2 * DN_QK + DN_V)), _resident((1, nh2)), _resident((1, nh2)),
                  _resident((2, tb, tb)), _resident((5 * nh2, 2 * nh2 * HALF))],
        out_specs=[blk(w_) for w_, _ in outs]
        + [pl.BlockSpec((1, tm // DN_CHUNK, nh2, DN_CHUNK), lambda b, i: (b, i, 0, 0))],
        out_shape=[jax.ShapeDtypeStruct((B, S, w_), dt) for w_, dt in outs]
        + [jax.ShapeDtypeStruct((B, S // DN_CHUNK, nh2, DN_CHUNK), F32)],
        compiler_params=_cparams(("parallel", "parallel")),
        name="inproj",
    )(x, x, x, g, w, w_mq, w_ba, qg2, kg2, conv_w.astype(F32), a_log.reshape(1, nh2).astype(F32),
      dt_bias.reshape(1, nh2).astype(F32), jnp.asarray(tri, BF16), jnp.asarray(expand, BF16))


def _t5_buckets(rel):
    nb = REL_BUCKETS // 2
    max_exact = nb // 2
    ret = (rel > 0).astype(np.int32) * nb
    n = np.abs(rel)
    large = max_exact + (np.log(np.maximum(n, 1) / max_exact) / np.log(REL_MAX_DIST / max_exact)
                         * (nb - max_exact)).astype(np.int32)
    large = np.minimum(large, nb - 1)
    return (ret + np.where(n < max_exact, n, large)).astype(np.int32)


_ATTN_GROUPS = ((0, 1, True, False), (0, 1, False, True), (2, 3, True, True), (2, 3, False, False))
_ATTN_GROUP_HEADS = ((0, 2), (1, 3), (4, 6), (5, 7))
ATTN_BLOCKS_PER_STEP = 8
ATTN_WAVE_BLOCKS = 4


def _attn_kernel(q_ref, kp_ref, kc_ref, kn_ref, bias_ref, sink_ref, o_ref):
    T = ATTN_BLOCK
    nq = ATTN_BLOCKS_PER_STEP
    first_blk = pl.program_id(1) * nq
    last_blk = pl.num_programs(1) * nq - 1
    kv_all = jnp.concatenate([kp_ref[0], kc_ref[0], kn_ref[0]], axis=0)
    ones = jnp.ones((3 * T, LANES), BF16)
    low_q = _lane_is_low((T, LANES))
    zero = jnp.zeros((T, LANES), BF16)
    groups = range(len(_ATTN_GROUPS))
    kvs, scores, maxes, res = {}, {}, {}, {}

    def score_stage(blocks):
        for qb in blocks:
            kv = kv_all[qb * T:(qb + 3) * T]
            kvs[qb, False] = (kv[:, 0 * LANES:1 * LANES],
                              jnp.concatenate([kv[:, 1 * LANES:2 * LANES], ones], axis=1))
            kvs[qb, True] = (kv[:, 2 * LANES:3 * LANES],
                             jnp.concatenate([kv[:, 3 * LANES:4 * LANES], ones], axis=1))
            rq = slice(qb * T, (qb + 1) * T)
            blk = first_blk + qb
            edge = jnp.where(blk == 0, 0, jnp.where(blk == last_blk, 2, 1))
            for gi in groups:
                pa, pb, low, swapped = _ATTN_GROUPS[gi]
                sel = low_q if low else jnp.logical_not(low_q)
                lhs = jnp.concatenate([jnp.where(sel, q_ref[0, rq, pa * LANES:(pa + 1) * LANES], zero),
                                       jnp.where(sel, q_ref[0, rq, pb * LANES:(pb + 1) * LANES], zero)], axis=0)
                scores[qb, gi] = _dot_nt(lhs, kvs[qb, swapped][0]) + bias_ref[edge, gi]

    def softmax_stages(blocks):
        probs = [(qb, gi) for qb in blocks for gi in groups]
        for qb, gi in probs:
            s = scores.pop((qb, gi))
            m = jnp.maximum(jnp.max(s, axis=-1, keepdims=True), sink_ref[gi])
            maxes[qb, gi] = m
            scores[qb, gi] = jnp.exp2(s - m).astype(BF16)
        for qb, gi in probs:
            res[qb, gi] = _dot(scores.pop((qb, gi)), kvs[qb, _ATTN_GROUPS[gi][3]][1])
        for qb, gi in probs:
            r = res[qb, gi]
            den = r[:, LANES:] + jnp.exp2(sink_ref[gi] - maxes[qb, gi])
            res[qb, gi] = r[:, :LANES] * (1.0 / den)
        for qb in blocks:
            rq = slice(qb * T, (qb + 1) * T)
            for pc, (ge, go) in enumerate(((0, 1), (0, 1), (2, 3), (2, 3))):
                r0 = (pc % 2) * T
                out = jnp.where(low_q, res[qb, ge][r0:r0 + T], res[qb, go][r0:r0 + T])
                o_ref[0, rq, pc * LANES:(pc + 1) * LANES] = out.astype(BF16)

    waves = [range(w0, min(w0 + ATTN_WAVE_BLOCKS, nq)) for w0 in range(0, nq, ATTN_WAVE_BLOCKS)]
    score_stage(waves[0])
    for w, blocks in enumerate(waves):
        if w + 1 < len(waves):
            score_stage(waves[w + 1])
        softmax_stages(blocks)


def _attn(aq, akv, rel_table, sink):
    B, S, _ = aq.shape
    T = ATTN_BLOCK
    nb = S // T
    assert nb >= 2
    t_idx = np.arange(T)[:, None]
    j_idx = np.arange(3 * T)[None, :]
    rel = j_idx - T - t_idx
    onehot = jnp.asarray(np.eye(REL_BUCKETS, dtype=np.float32)[_t5_buckets(rel)])
    bias = jnp.einsum("tjr,rh->htj", onehot, rel_table.astype(F32), precision=lax.Precision.HIGHEST) * LOG2E
    in_win = np.abs(rel) <= WINDOW
    edge_ok = np.stack([in_win & (j_idx >= T), in_win, in_win & (j_idx < 2 * T)])
    bias = jnp.where(jnp.asarray(edge_ok)[:, None], bias[None], NEG)
    bias_g = jnp.stack([jnp.concatenate([bias[:, a], bias[:, b]], axis=1) for a, b in _ATTN_GROUP_HEADS], axis=1)
    sk = sink.astype(F32) * LOG2E
    sink_g = jnp.stack([jnp.concatenate([jnp.full((T, 1), 1.0) * sk[a], jnp.full((T, 1), 1.0) * sk[b]], axis=0)
                        for a, b in _ATTN_GROUP_HEADS])
    kv_w = akv.shape[-1]
    nq = ATTN_BLOCKS_PER_STEP
    assert nb % nq == 0
    return pl.pallas_call(
        _attn_kernel,
        grid=(B, nb // nq),
        in_specs=[pl.BlockSpec((1, nq * T, ATTN_Q), lambda b, i: (b, i, 0)),
                  pl.BlockSpec((1, T, kv_w), lambda b, i: (b, jnp.maximum(i * nq - 1, 0), 0)),
                  pl.BlockSpec((1, nq * T, kv_w), lambda b, i: (b, i, 0)),
                  pl.BlockSpec((1, T, kv_w), lambda b, i: (b, jnp.minimum((i + 1) * nq, nb - 1), 0)),
                  _resident((3, 4, 2 * T, 3 * T)), _resident((4, 2 * T, 1))],
        out_specs=pl.BlockSpec((1, nq * T, ATTN_Q), lambda b, i: (b, i, 0)),
        out_shape=jax.ShapeDtypeStruct((B, S, ATTN_Q), BF16),
        compiler_params=_cparams(("parallel", "parallel")),
        name="attn",
    )(aq, akv, akv, akv, bias_g, sink_g)


DN_TILE_HEADS = 2
DN_TILE = DN_TILE_HEADS * HALF
DN_TILES = DN_HEADS // DN_TILE_HEADS
DN_PAIRS = DN_HEADS // 2
DN_CHUNKS_PER_STEP = 8
DN_WAVE_CHUNKS = 2
DN_SCAN_CHUNKS = 16


def _head_of_lane(shape):
    return lax.broadcasted_iota(jnp.int32, shape, len(shape) - 1) // HALF


def _block_diag(x, head):
    zero = jnp.zeros_like(x)
    return jnp.concatenate([jnp.where(head == h, x, zero) for h in range(DN_TILE_HEADS)], axis=0)


_PK_U, _PK_W, _PK_QKD, _PK_QIN = range(4)


def _packed_cols(kind, tl):
    c0 = kind * DN_QK + tl * DN_TILE
    return slice(c0, c0 + DN_TILE)


def _dn_chunk_kernel(qkv_ref, bx_ref, gx_ref, gt_ref, pk_ref, kupt_ref, egt_ref):
    C = DN_CHUNK
    TW = DN_TILE
    head = _head_of_lane((C, TW))
    r = lax.broadcasted_iota(jnp.int32, (C, TW), 0)
    m = lax.broadcasted_iota(jnp.int32, (C, TW), 1) % HALF
    eye = jnp.where(r == m, 1.0, 0.0)
    dir_w = len((_PK_U, _PK_W, _PK_QKD, _PK_QIN)) * DN_QK

    def put(kind, d, rows, tl, val):
        c = _packed_cols(kind, tl)
        pk_ref[0, rows, d * dir_w + c.start:d * dir_w + c.stop] = val.astype(BF16)

    def prepare(ch):
        probs = []
        rows = slice(ch * C, (ch + 1) * C)
        for tl in range(DN_TILES):
            q4 = qkv_ref[0, rows, tl * TW:(tl + 1) * TW]
            k4 = qkv_ref[0, rows, DN_QK + tl * TW:DN_QK + (tl + 1) * TW]
            v4 = qkv_ref[0, rows, 2 * DN_QK + tl * TW:2 * DN_QK + (tl + 1) * TW]
            q4f, k4f, v4f = q4.astype(F32), k4.astype(F32), v4.astype(F32)
            qkk = _dot_nt(jnp.concatenate([q4, k4], axis=0), _block_diag(k4, head))
            qk, kk = qkk[:C], qkk[C:]
            for d in range(2):
                col = slice(d * DN_QK + tl * TW, d * DN_QK + (tl + 1) * TW)
                beta4 = bx_ref[0, rows, col]
                gcol4 = gx_ref[0, rows, col]
                h0 = d * DN_HEADS + tl * DN_TILE_HEADS
                grow4 = jnp.concatenate([gt_ref[0, ch, h0 + h:h0 + h + 1, :] for h in range(DN_TILE_HEADS)],
                                        axis=1)
                incl = (r >= m) if d == 0 else (r <= m)
                strict = (r > m) if d == 0 else (r < m)
                decay = jnp.where(incl, jnp.exp(jnp.where(incl, gcol4 - grow4, 0.0)), 0.0)
                gtot4 = gcol4[C - 1:C] if d == 0 else gcol4[0:1]
                e_col = jnp.exp(gcol4)
                p = jnp.where(strict, -(kk * beta4 * decay), 0.0)
                vb = (v4f * beta4).astype(BF16)
                kbe = (k4f * beta4 * e_col).astype(BF16)
                rhs = jnp.concatenate([_block_diag(vb, head), _block_diag(kbe, head)], axis=1)
                put(_PK_QKD, d, rows, tl, qk * decay)
                put(_PK_QIN, d, rows, tl, q4f * e_col)
                kupt_ref[0, ch, d, tl] = (k4f * jnp.exp(gtot4 - gcol4)).T.astype(BF16)
                egt_ref[0, ch, d, tl:tl + 1, :] = jnp.exp(gtot4)
                probs.append([p, eye + p, rhs, rows, d, tl])
        return probs

    def solve(probs):
        for pb in probs:
            pb[0] = _dot(pb[0].astype(BF16), _block_diag(pb[0].astype(BF16), head))
        for _ in range(4):
            for pb in probs:
                both = _dot(jnp.concatenate([pb[1], pb[0]], axis=0).astype(BF16),
                            _block_diag(pb[0].astype(BF16), head))
                pb[1] = pb[1] + both[:C]
                pb[0] = both[C:]
        for p, t, rhs, rows, d, tl in probs:
            t = t + _dot(t.astype(BF16), _block_diag(p.astype(BF16), head))
            sol = _dot(t.astype(BF16), rhs)
            put(_PK_U, d, rows, tl, sol[:, :TW])
            put(_PK_W, d, rows, tl, sol[:, TW:])

    waves = [range(c0, min(c0 + DN_WAVE_CHUNKS, DN_CHUNKS_PER_STEP))
             for c0 in range(0, DN_CHUNKS_PER_STEP, DN_WAVE_CHUNKS)]
    ready = [pb for ch in waves[0] for pb in prepare(ch)]
    for w in range(len(waves)):
        upcoming = [pb for ch in waves[w + 1] for pb in prepare(ch)] if w + 1 < len(waves) else []
        solve(ready)
        ready = upcoming


def _dn_chunk(qkvn, betax, gcx, gct):
    B, S, W = qkvn.shape
    C = DN_CHUNK
    cps = DN_CHUNKS_PER_STEP
    nc = S // C
    wide = 2 * DN_QK
    blk = lambda w_: pl.BlockSpec((1, cps * C, w_), lambda b, i: (b, i, 0))
    return pl.pallas_call(
        _dn_chunk_kernel,
        grid=(B, nc // cps),
        in_specs=[blk(W), blk(wide), blk(wide),
                  pl.BlockSpec((1, cps, 2 * DN_HEADS, C), lambda b, i: (b, i, 0, 0))],
        out_specs=[blk(4 * wide),
                   pl.BlockSpec((1, cps, 2, DN_TILES, DN_TILE, C), lambda b, i: (b, i, 0, 0, 0, 0)),
                   pl.BlockSpec((1, cps, 2, DN_TILES, DN_TILE), lambda b, i: (b, i, 0, 0, 0))],
        out_shape=[jax.ShapeDtypeStruct((B, S, 4 * wide), BF16),
                   jax.ShapeDtypeStruct((B, nc, 2, DN_TILES, DN_TILE, C), BF16),
                   jax.ShapeDtypeStruct((B, nc, 2, DN_TILES, DN_TILE), F32)],
        compiler_params=_cparams(("parallel", "parallel")),
        name="dn_chunk",
    )(qkvn, betax, gcx, gct)


def _dn_scan_kernel(pkf, kuptf, egtf, pkb, kuptb, egtb, of_ref, ob_ref, st_ref):
    @pl.when(pl.program_id(1) == 0)
    def _():
        st_ref[...] = jnp.zeros_like(st_ref)

    C = DN_CHUNK
    TW = DN_TILE
    head = _head_of_lane((C, TW))
    rr = lax.broadcasted_iota(jnp.int32, (LANES, LANES), 0) < HALF
    cc = lax.broadcasted_iota(jnp.int32, (LANES, LANES), 1) < HALF
    on_diag = rr == cc
    zpad = jnp.zeros((LANES, LANES), BF16)
    ppt = TW // LANES
    dirs = ((pkf, kuptf, egtf, of_ref), (pkb, kuptb, egtb, ob_ref))
    tile_ids = [(d, tl) for d in range(2) for tl in range(DN_TILES)]
    state = {(d, pr): st_ref[d, pr] for d in range(2) for pr in range(DN_PAIRS)}
    for step in range(DN_SCAN_CHUNKS):
        ws, vb = {}, {}
        for d, tl in tile_ids:
            pk_ref, kupt_ref, egt_ref, o_ref = dirs[d]
            ch = step if d == 0 else DN_SCAN_CHUNKS - 1 - step
            rows = slice(ch * C, (ch + 1) * C)
            lhs = jnp.concatenate([pk_ref[0, rows, _packed_cols(_PK_W, tl)],
                                   pk_ref[0, rows, _packed_cols(_PK_QIN, tl)]], axis=0)
            blocks = [state[d, ppt * tl + i].astype(BF16) for i in range(ppt)]
            s_tile = jnp.concatenate(
                [jnp.concatenate([blocks[i] if j == i else zpad for j in range(ppt)], axis=1) for i in range(ppt)],
                axis=0) if ppt > 1 else blocks[0]
            ws[d, tl] = _dot(lhs, s_tile)
        for d, tl in tile_ids:
            pk_ref = dirs[d][0]
            ch = step if d == 0 else DN_SCAN_CHUNKS - 1 - step
            rows = slice(ch * C, (ch + 1) * C)
            vb[d, tl] = (pk_ref[0, rows, _packed_cols(_PK_U, tl)].astype(F32) - ws[d, tl][:C]).astype(BF16)
        for d, tl in tile_ids:
            pk_ref, kupt_ref, egt_ref, o_ref = dirs[d]
            ch = step if d == 0 else DN_SCAN_CHUNKS - 1 - step
            rows = slice(ch * C, (ch + 1) * C)
            sl = slice(tl * TW, (tl + 1) * TW)
            for part in range(ppt):
                ls = slice(part * LANES, (part + 1) * LANES)
                upd = _dot(kupt_ref[0, ch, 0, tl, ls, :], vb[d, tl][:, ls])
                pr = ppt * tl + part
                state[d, pr] = state[d, pr] * egt_ref[0, ch, 0, tl:tl + 1, ls] + jnp.where(on_diag, upd, 0.0)
            o4 = ws[d, tl][C:] + _dot(pk_ref[0, rows, _packed_cols(_PK_QKD, tl)], _block_diag(vb[d, tl], head))
            o_ref[0, rows, sl] = o4.astype(BF16)
    for dp, s in state.items():
        st_ref[dp[0], dp[1]] = s


def _dn_scan(packed, kupt, egt):
    B, S, pw = packed.shape
    C = DN_CHUNK
    sc = DN_SCAN_CHUNKS
    nb = S // (C * sc)
    fwd3 = lambda b, i: (b, i, 0)
    bwd3 = lambda b, i: (b, nb - 1 - i, 1)
    row_f = pl.BlockSpec((1, sc * C, pw // 2), fwd3)
    row_b = pl.BlockSpec((1, sc * C, pw // 2), bwd3)
    kup_f = pl.BlockSpec((1, sc, 1, DN_TILES, DN_TILE, C), lambda b, i: (b, i, 0, 0, 0, 0))
    kup_b = pl.BlockSpec((1, sc, 1, DN_TILES, DN_TILE, C), lambda b, i: (b, nb - 1 - i, 1, 0, 0, 0))
    egt_f = pl.BlockSpec((1, sc, 1, DN_TILES, DN_TILE), lambda b, i: (b, i, 0, 0, 0))
    egt_b = pl.BlockSpec((1, sc, 1, DN_TILES, DN_TILE), lambda b, i: (b, nb - 1 - i, 1, 0, 0))
    return pl.pallas_call(
        _dn_scan_kernel,
        grid=(B, nb),
        in_specs=[row_f, kup_f, egt_f, row_b, kup_b, egt_b],
        out_specs=[pl.BlockSpec((1, sc * C, DN_V), fwd3),
                   pl.BlockSpec((1, sc * C, DN_V), lambda b, i: (b, nb - 1 - i, 0))],
        out_shape=[jax.ShapeDtypeStruct((B, S, DN_V), BF16), jax.ShapeDtypeStruct((B, S, DN_V), BF16)],
        scratch_shapes=[pltpu.VMEM((2, DN_PAIRS, LANES, LANES), F32)],
        compiler_params=_cparams(("parallel", "arbitrary")),
        name="dn_scan",
    )(packed, kupt, egt, packed, kupt, egt)


def _mem_kv_kernel(m_ref, g_ref, w_ref, kg_ref, k_ref, v_ref):
    x = m_ref[...]
    ms = jnp.mean(x * x, axis=-1, keepdims=True)
    h = ((x * lax.rsqrt(ms + EPS)) * g_ref[...]).astype(BF16)
    kv = _dot(h, w_ref[...])
    for hd in range(MEM_HEADS):
        kh = kv[:, hd * LANES:(hd + 1) * LANES]
        kms = jnp.mean(kh * kh, axis=-1, keepdims=True)
        k_ref[:, hd * LANES:(hd + 1) * LANES] = ((kh * lax.rsqrt(kms + EPS)) * kg_ref[...]).astype(BF16)
    v_ref[...] = kv[:, MEM_Q:].astype(BF16)


def _mem_kv(mem2, g, w_kv, kg):
    n = mem2.shape[0]
    return pl.pallas_call(
        _mem_kv_kernel,
        grid=(n // N_MEM,),
        in_specs=[pl.BlockSpec((N_MEM, D_MODEL), lambda i: (i, 0)),
                  pl.BlockSpec((1, D_MODEL), lambda i: (0, 0)),
                  pl.BlockSpec((D_MODEL, 2 * MEM_Q), lambda i: (0, 0)),
                  pl.BlockSpec((1, MEM_HEAD_DIM), lambda i: (0, 0))],
        out_specs=[pl.BlockSpec((N_MEM, MEM_Q), lambda i: (i, 0)), pl.BlockSpec((N_MEM, MEM_Q), lambda i: (i, 0))],
        out_shape=[jax.ShapeDtypeStruct((n, MEM_Q), BF16), jax.ShapeDtypeStruct((n, MEM_Q), BF16)],
        compiler_params=_cparams(("parallel",)),
        name="mem_kv",
    )(mem2, g, w_kv, kg)


def _mem_attend(q_ref, k_ref, v_ref, qg_ref):
    scale = MEM_HEAD_DIM ** -0.5 * LOG2E
    ones = jnp.ones((N_MEM, LANES), BF16)
    heads = []
    for hd in range(MEM_HEADS):
        sl = slice(hd * LANES, (hd + 1) * LANES)
        q = q_ref[:, sl].astype(F32)
        qms = jnp.mean(q * q, axis=-1, keepdims=True)
        qn = ((q * lax.rsqrt(qms + EPS)) * qg_ref[...] * scale).astype(BF16)
        s = _dot_nt(qn, k_ref[0, :, sl])
        p = jnp.exp2(s - jnp.max(s, axis=-1, keepdims=True))
        r = _dot(p.astype(BF16), jnp.concatenate([v_ref[0, :, sl], ones], axis=1))
        heads.append((r[:, :LANES] * (1.0 / r[:, LANES:])).astype(BF16))
    return jnp.concatenate(heads, axis=1)


def _merge_kernel(x_ref, ya_ref, of_ref, ob_ref, z_ref, mq_ref, mk_ref, mv_ref, mqg_ref, ng_ref, wg_ref, og_ref,
                  wa_ref, wd_ref, wm_ref, wo_ref, o_ref):
    y_mem = _mem_attend(mq_ref, mk_ref, mv_ref, mqg_ref)
    x = x_ref[...]
    ms = jnp.mean(x * x, axis=-1, keepdims=True)
    h = ((x * lax.rsqrt(ms + EPS)) * ng_ref[...]).astype(BF16)
    o = of_ref[...].astype(F32) + ob_ref[...].astype(F32)
    z = z_ref[...].astype(F32)
    og = og_ref[...]
    parts = []
    for pc in range(DN_V // LANES):
        os_ = o[:, pc * LANES:(pc + 1) * LANES]
        on = os_ * lax.rsqrt(_half_sums(os_ * os_) * (1.0 / DN_VALUE_DIM) + EPS) * og
        parts.append((on * _silu(z[:, pc * LANES:(pc + 1) * LANES])).astype(BF16))
    y_dn = jnp.concatenate(parts, axis=1)
    def gate(b):
        return _sigmoid(_dot(h, wg_ref[:, b * D_MODEL:(b + 1) * D_MODEL]))

    merged = (gate(0) * _dot(ya_ref[...], wa_ref[...])
              + gate(1) * _dot(y_dn, wd_ref[...])
              + gate(2) * _dot(y_mem, wm_ref[...]))
    o_ref[...] = x + _dot(merged.astype(BF16), wo_ref[...])


def _merge(x2, ya, of, ob, z, mq, mk, mv, mqg, ng, wg, og, wa, wd, wm, wo):
    n = x2.shape[0]
    tm = ROW_TILE
    tiles_per_seq = n // mk.shape[0] // tm
    row = lambda w_: pl.BlockSpec((tm, w_), lambda i: (i, 0))
    mem = pl.BlockSpec((1, N_MEM, MEM_Q), lambda i: (i // tiles_per_seq, 0, 0))
    full = lambda a, b: _resident((a, b))
    return pl.pallas_call(
        _merge_kernel,
        grid=(n // tm,),
        in_specs=[row(D_MODEL), row(ATTN_Q), row(DN_V), row(DN_V), row(DN_V), row(MEM_Q), mem, mem,
                  full(1, MEM_HEAD_DIM), full(1, D_MODEL), full(D_MODEL, N_BRANCH * D_MODEL),
                  full(1, LANES), full(ATTN_Q, D_MODEL), full(DN_V, D_MODEL), full(MEM_Q, D_MODEL),
                  full(D_MODEL, D_MODEL)],
        out_specs=row(D_MODEL),
        out_shape=jax.ShapeDtypeStruct((n, D_MODEL), F32),
        compiler_params=_cparams(("parallel",)),
        name="merge",
    )(x2, ya, of, ob, z, mq, mk, mv, mqg, ng, wg, og, wa, wd, wm, wo)


def _ffn_kernel(xp_ref, xc_ref, xn_ref, g_ref, wu_ref, cw_ref, cb_ref, wd_ref, o_ref, act_ref):
    i = pl.program_id(1)
    nt = pl.num_programs(1)
    tm = xc_ref.shape[1]
    xc = xc_ref[0]
    prev = jnp.where(i == 0, 0.0, xp_ref[0])
    nxt = jnp.where(i == nt - 1, 0.0, xn_ref[0])
    xe = jnp.concatenate([prev, xc, nxt], axis=0)
    ms = jnp.mean(xe * xe, axis=-1, keepdims=True)
    h = ((xe * lax.rsqrt(ms + EPS)) * g_ref[...]).astype(BF16)

    def both(ref, rows_, c0):
        return jnp.concatenate([ref[rows_, c0:c0 + FF_CHUNK], ref[rows_, D_FF + c0:D_FF + c0 + FF_CHUNK]], axis=1)

    for c in range(D_FF // FF_CHUNK):
        c0 = c * FF_CHUNK
        u = _dot(h, both(wu_ref, slice(None), c0))
        y = None
        for j in range(FFN_CONV):
            term = _shift_rows(u, FFN_CONV // 2 - j)[HALO:HALO + tm] * both(cw_ref, slice(j, j + 1), c0)
            y = term if y is None else y + term
        y = y + both(cb_ref, slice(None), c0)
        act_ref[:, c0:c0 + FF_CHUNK] = (_silu(y[:, :FF_CHUNK]) * y[:, FF_CHUNK:]).astype(BF16)
    o_ref[0] = xc + _dot(act_ref[...], wd_ref[...])


def _ffn(x1, g, wu, cw, cb, wd):
    B, S, _ = x1.shape
    tm = FFN_ROW_TILE
    nt = S // tm
    hb = tm // HALO
    return pl.pallas_call(
        _ffn_kernel,
        grid=(B, nt),
        in_specs=[pl.BlockSpec((1, HALO, D_MODEL), lambda b, i: (b, jnp.maximum(i * hb - 1, 0), 0)),
                  pl.BlockSpec((1, tm, D_MODEL), lambda b, i: (b, i, 0)),
                  pl.BlockSpec((1, HALO, D_MODEL), lambda b, i: (b, jnp.minimum((i + 1) * hb, S // HALO - 1), 0)),
                  _resident((1, D_MODEL)), _resident((D_MODEL, 2 * D_FF)), _resident((FFN_CONV, 2 * D_FF)),
                  _resident((1, 2 * D_FF)), _resident((D_FF, D_MODEL))],
        out_specs=pl.BlockSpec((1, tm, D_MODEL), lambda b, i: (b, i, 0)),
        out_shape=jax.ShapeDtypeStruct((B, S, D_MODEL), F32),
        scratch_shapes=[pltpu.VMEM((tm, D_FF), BF16)],
        compiler_params=_cparams(("parallel", "parallel")),
        name="ffn",
    )(x1, x1, x1, g, wu, cw, cb, wd)


def _permute_w_in(w):
    idx = np.cumsum((0,) + IN_SPLITS)
    wb = w.astype(BF16)
    return wb[:, :idx[7]], wb[:, idx[9]:idx[10]], wb[:, idx[7]:idx[9]], wb[:, idx[10]:]


def _layer(x, mem, rel_bias_table, p):
    B, S, D = x.shape
    n = B * S
    x2 = x.reshape(n, D)
    row = lambda a: a.reshape(1, -1).astype(F32)
    tile2 = lambda a: jnp.tile(a.astype(F32), 2)[None]
    w_main, w_mq, w_ba, w_gate = _permute_w_in(p["w_in"])
    aq, akv, dz, mq, qkvn, betax, gcx, gct = _inproj(
        x, row(p["norm_mix_g"]), w_main, w_mq, w_ba, tile2(p["attn_q_norm_g"]), tile2(p["attn_k_norm_g"]),
        p["dn_conv_w"], p["dn_a_log"], p["dn_dt_bias"])
    y_attn = _attn(aq, akv, rel_bias_table, p["attn_sink"])
    o_f, o_b = _dn_scan(*_dn_chunk(qkvn, betax, gcx, gct))
    mk, mv = _mem_kv(mem.reshape(B * N_MEM, D), row(p["mem_norm_g"]), p["mem_w_kv"].astype(BF16),
                     row(p["mem_k_norm_g"]))
    og2 = jnp.tile(p["dn_out_norm_g"].astype(F32), 2)[None]
    x1 = _merge(x2, y_attn.reshape(n, -1), o_f.reshape(n, -1), o_b.reshape(n, -1), dz.reshape(n, -1),
                mq.reshape(n, -1), mk.reshape(B, N_MEM, -1), mv.reshape(B, N_MEM, -1), row(p["mem_q_norm_g"]),
                row(p["norm_mix_g"]), w_gate, og2, p["w_br_attn"].astype(BF16), p["w_br_dn"].astype(BF16),
                p["w_br_mem"].astype(BF16), p["w_out"].astype(BF16))
    return _ffn(x1.reshape(B, S, D), row(p["norm_ffn_g"]), p["ffn_w_up"].astype(BF16), p["ffn_conv_w"].astype(F32),
                row(p["ffn_conv_b"]), p["ffn_w_down"].astype(BF16))


_LAYER_PARAMS = ("norm_mix_g", "w_in", "attn_q_norm_g", "attn_k_norm_g", "attn_sink", "dn_conv_w", "dn_a_log",
                 "dn_dt_bias", "dn_out_norm_g", "mem_norm_g", "mem_w_kv", "mem_q_norm_g", "mem_k_norm_g",
                 "w_br_attn", "w_br_dn", "w_br_mem", "w_out", "norm_ffn_g", "ffn_w_up", "ffn_conv_w", "ffn_conv_b",
                 "ffn_w_down")


def kernel(x, mem, rel_bias_table, norm_mix_g, w_in, attn_q_norm_g, attn_k_norm_g, attn_sink, dn_conv_w, dn_a_log,
           dn_dt_bias, dn_out_norm_g, mem_norm_g, mem_w_kv, mem_q_norm_g, mem_k_norm_g, w_br_attn, w_br_dn,
           w_br_mem, w_out, norm_ffn_g, ffn_w_up, ffn_conv_w, ffn_conv_b, ffn_w_down):
    stacked = dict(zip(_LAYER_PARAMS, (norm_mix_g, w_in, attn_q_norm_g, attn_k_norm_g, attn_sink, dn_conv_w,
                                       dn_a_log, dn_dt_bias, dn_out_norm_g, mem_norm_g, mem_w_kv, mem_q_norm_g,
                                       mem_k_norm_g, w_br_attn, w_br_dn, w_br_mem, w_out, norm_ffn_g, ffn_w_up,
                                       ffn_conv_w, ffn_conv_b, ffn_w_down)))
    depth = w_in.shape[0]
    for l in range(depth):
        x = _layer(x, mem, rel_bias_table, {k: v[l] for k, v in stacked.items()})
    return x
```

```python
import math

import numpy as np
import jax
import jax.numpy as jnp
from jax import lax
from jax.experimental import pallas as pl
from jax.experimental.pallas import tpu as pltpu

F32 = jnp.float32
BF16 = jnp.bfloat16

EPS = 1e-6
D_MODEL = 1024
N_MEM = 256
ATTN_HEADS = 8
ATTN_KV_HEADS = 2
ATTN_HEAD_DIM = 64
WINDOW = 128
ATTN_BLOCK = 128
REL_BUCKETS = 32
REL_MAX_DIST = 128
DN_HEADS = 8
DN_KEY_DIM = 64
DN_VALUE_DIM = 64
DN_CONV = 5
DN_CHUNK = 64
MEM_HEADS = 4
MEM_HEAD_DIM = 128
D_FF = 2816
FFN_CONV = 3
N_BRANCH = 3

ATTN_Q = ATTN_HEADS * ATTN_HEAD_DIM
ATTN_KV = ATTN_KV_HEADS * ATTN_HEAD_DIM
DN_QK = DN_HEADS * DN_KEY_DIM
DN_V = DN_HEADS * DN_VALUE_DIM
MEM_Q = MEM_HEADS * MEM_HEAD_DIM
IN_SPLITS = (ATTN_Q, ATTN_KV, ATTN_KV, DN_QK, DN_QK, DN_V, DN_V, 2 * DN_HEADS, 2 * DN_HEADS, MEM_Q,
             N_BRANCH * D_MODEL)

LANES = 128
SUBLANES = 8
HALF = 64
HALO = SUBLANES
NEG = -1e30
VMEM_LIMIT = 56 * 1024 * 1024
LOG2E = math.log2(math.e)

ROW_TILE = 1024
FF_CHUNK = 256


def _cparams(sem):
    return pltpu.CompilerParams(dimension_semantics=sem, vmem_limit_bytes=VMEM_LIMIT)


def _resident(shape):
    zeros = (0,) * len(shape)
    return pl.BlockSpec(shape, lambda *_: zeros, pipeline_mode=pl.Buffered(1))


def _dot(a, b):
    return jnp.dot(a, b, preferred_element_type=F32)


def _dot_nt(a, b):
    return lax.dot_general(a, b, (((1,), (1,)), ((), ())), preferred_element_type=F32)


def _lane_is_low(shape):
    lane = lax.broadcasted_iota(jnp.int32, shape, len(shape) - 1)
    return (lane % LANES) < HALF


def _half_sums(sq):
    low = _lane_is_low(sq.shape)
    s_lo = jnp.sum(jnp.where(low, sq, 0.0), axis=-1, keepdims=True)
    s_hi = jnp.sum(jnp.where(low, 0.0, sq), axis=-1, keepdims=True)
    return jnp.where(low, s_lo, s_hi)


def _shift_rows(x, s):
    if s == 0:
        return x
    rr, cc = x.shape
    x3 = x.reshape(rr // SUBLANES, SUBLANES, cc)
    rot = pltpu.roll(x3, s % SUBLANES, axis=1)
    sub = lax.broadcasted_iota(jnp.int32, x3.shape, 1)
    if s > 0:
        other = jnp.concatenate([rot[-1:], rot[:-1]], axis=0)
        y3 = jnp.where(sub < s, other, rot)
    else:
        other = jnp.concatenate([rot[1:], rot[:1]], axis=0)
        y3 = jnp.where(sub >= SUBLANES + s, other, rot)
    return y3.reshape(rr, cc)


def _silu(x):
    h = 0.5 * x
    return h * jnp.tanh(h) + h


def _sigmoid(x):
    return 0.5 * jnp.tanh(0.5 * x) + 0.5


_C_AQ = (0, 512)
_C_AKV = (512, 768)
_C_DQKV = (768, 2304)
_C_DZ = (2304, 2816)
_N_IN = 2816


def _head_rmsnorm(t, gain2):
    return t * lax.rsqrt(_half_sums(t * t) * (1.0 / ATTN_HEAD_DIM) + EPS) * gain2


def _split_hi_lo(x):
    hi = x.astype(BF16)
    lo = (x - hi.astype(F32)).astype(BF16)
    return hi, lo


def _inproj_kernel(xp_ref, xc_ref, xn_ref, g_ref, w_ref, wmq_ref, wba_ref, qg_ref, kg_ref, cw_ref, alog_ref,
                   dtb_ref, tri_ref, expand_ref, aq_ref, akv_ref, dz_ref, mq_ref, qkv_ref, betax_ref, gcx_ref, gct_ref):
    i = pl.program_id(1)
    nt = pl.num_programs(1)
    tm = xc_ref.shape[1]
    prev = jnp.where(i == 0, 0.0, xp_ref[0])
    nxt = jnp.where(i == nt - 1, 0.0, xn_ref[0])
    xe = jnp.concatenate([prev, xc_ref[0], nxt], axis=0)
    ms = jnp.mean(xe * xe, axis=-1, keepdims=True)
    he = ((xe * lax.rsqrt(ms + EPS)) * g_ref[...]).astype(BF16)
    h = he[HALO:HALO + tm]

    def proj(c):
        return _dot(h, w_ref[:, c[0]:c[1]])

    cw_chunk = 2 * LANES

    def dn_qkv_chunk(c0):
        ue = _dot(he, w_ref[:, _C_DQKV[0] + c0:_C_DQKV[0] + c0 + cw_chunk])
        acc = None
        for j in range(DN_CONV):
            term = _shift_rows(ue, DN_CONV // 2 - j)[HALO:HALO + tm] * cw_ref[j:j + 1, c0:c0 + cw_chunk]
            acc = term if acc is None else acc + term
        y = _silu(acc)
        for l0 in range(0, cw_chunk, LANES):
            ys = y[:, l0:l0 + LANES]
            if c0 < 2 * DN_QK:
                ys = ys * lax.rsqrt(_half_sums(ys * ys) + EPS)
            if c0 < DN_QK:
                ys = ys * (DN_KEY_DIM ** -0.5)
            qkv_ref[0, :, c0 + l0:c0 + l0 + LANES] = ys.astype(BF16)

    def dn_beta_decay():
        ba = _dot(h, wba_ref[...])
        nh2 = 2 * DN_HEADS
        beta = _sigmoid(ba[:, :nh2])
        z = ba[:, nh2:2 * nh2] + dtb_ref[...]
        sp = jnp.maximum(z, 0.0) + jnp.log1p(jnp.exp(-jnp.abs(z)))
        g = -jnp.exp(alog_ref[...]) * sp
        g_hi, g_lo = _split_hi_lo(g)
        g_lo2 = (g - g_hi.astype(F32) - g_lo.astype(F32)).astype(BF16)
        gg = jnp.concatenate([g_hi, g_lo, g_lo2], axis=1)
        tb = tri_ref.shape[1]
        pre = jnp.concatenate([_dot(tri_ref[0], gg[r0:r0 + tb]) for r0 in range(0, tm, tb)], axis=0)
        suf = jnp.concatenate([_dot(tri_ref[1], gg[r0:r0 + tb]) for r0 in range(0, tm, tb)], axis=0)
        lane16 = lax.broadcasted_iota(jnp.int32, (tm, nh2), 1)
        gc = jnp.where(lane16 < DN_HEADS, pre[:, :nh2] + pre[:, nh2:2 * nh2] + pre[:, 2 * nh2:],
                       suf[:, :nh2] + suf[:, nh2:2 * nh2] + suf[:, 2 * nh2:])
        b_hi, b_lo = _split_hi_lo(beta)
        c_hi, c_lo = _split_hi_lo(gc)
        c_lo2 = (gc - c_hi.astype(F32) - c_lo.astype(F32)).astype(BF16)
        bx = _dot(jnp.concatenate([b_hi, b_lo, c_hi, c_lo, c_lo2], axis=1), expand_ref[...])
        betax_ref[0] = bx[:, :nh2 * HALF]
        gcx_ref[0] = bx[:, nh2 * HALF:]
        gct = jnp.concatenate([gc, jnp.zeros((tm, LANES - nh2), F32)], axis=1).T
        for c in range(tm // DN_CHUNK):
            gct_ref[0, c] = gct[:nh2, c * DN_CHUNK:(c + 1) * DN_CHUNK]

    def attn_q():
        aq = proj(_C_AQ)
        q_scale = ATTN_HEAD_DIM ** -0.5 * LOG2E
        for pc in range(ATTN_Q // LANES):
            sl = slice(pc * LANES, (pc + 1) * LANES)
            aq_ref[0, :, sl] = (_head_rmsnorm(aq[:, sl], qg_ref[...]) * q_scale).astype(BF16)

    def attn_kv():
        akv = proj(_C_AKV)
        kn = _head_rmsnorm(akv[:, :LANES], kg_ref[...])
        av = akv[:, LANES:]
        akv_ref[0, :, 0 * LANES:1 * LANES] = kn.astype(BF16)
        akv_ref[0, :, 1 * LANES:2 * LANES] = av.astype(BF16)
        akv_ref[0, :, 2 * LANES:3 * LANES] = pltpu.roll(kn, HALF, axis=1).astype(BF16)
        akv_ref[0, :, 3 * LANES:4 * LANES] = pltpu.roll(av, HALF, axis=1).astype(BF16)

    def dn_z():
        dz_ref[0] = proj(_C_DZ).astype(BF16)

    def mem_q():
        mq_ref[0] = _dot(h, wmq_ref[...]).astype(BF16)

    for c0 in range(0, 2 * DN_QK + DN_V, cw_chunk):
        dn_qkv_chunk(c0)
    for part in (dn_beta_decay, attn_q, attn_kv, dn_z, mem_q):
        part()


def _inproj(x, g, w, w_mq, w_ba, qg2, kg2, conv_w, a_log, dt_bias):
    B, S, D = x.shape
    tm = ROW_TILE
    nt = S // tm
    hb = tm // HALO
    nh2 = 2 * DN_HEADS
    tb = 2 * LANES
    r = np.arange(tb)
    same = (r[:, None] // DN_CHUNK) == (r[None, :] // DN_CHUNK)
    tri = np.stack([same & (r[:, None] >= r[None, :]), same & (r[:, None] <= r[None, :])]).astype(np.float32)
    rep = np.repeat(np.eye(nh2, dtype=np.float32), HALF, axis=1)
    zero = np.zeros_like(rep)
    expand = np.block([[rep, zero]] * 2 + [[zero, rep]] * 3)
    blk = lambda w_: pl.BlockSpec((1, tm, w_), lambda b, i: (b, i, 0))
    outs = [(ATTN_Q, BF16), (4 * LANES, BF16), (DN_V, BF16), (MEM_Q, BF16), (2 * DN_QK + DN_V, BF16),
            (nh2 * HALF, F32), (nh2 * HALF, F32)]
    return pl.pallas_call(
        _inproj_kernel,
        grid=(B, nt),
        in_specs=[pl.BlockSpec((1, HALO, D), lambda b, i: (b, jnp.maximum(i * hb - 1, 0), 0)),
                  blk(D),
                  pl.BlockSpec((1, HALO, D), lambda b, i: (b, jnp.minimum((i + 1) * hb, S // HALO - 1), 0)),
                  _resident((1, D)), _resident((D, _N_IN)), _resident((D, MEM_Q)), _resident((D, 2 * nh2)),
                  _resident((1, LANES)), _resident((1, LANES)),
                  _resident((DN_CONV, 2 * DN_QK + DN_V)), _resident((1, nh2)), _resident((1, nh2)),
                  _resident((2, tb, tb)), _resident((5 * nh2, 2 * nh2 * HALF))],
        out_specs=[blk(w_) for w_, _ in outs]
        + [pl.BlockSpec((1, tm // DN_CHUNK, nh2, DN_CHUNK), lambda b, i: (b, i, 0, 0))],
        out_shape=[jax.ShapeDtypeStruct((B, S, w_), dt) for w_, dt in outs]
        + [jax.ShapeDtypeStruct((B, S // DN_CHUNK, nh2, DN_CHUNK), F32)],
        compiler_params=_cparams(("parallel", "parallel")),
        name="inproj",
    )(x, x, x, g, w, w_mq, w_ba, qg2, kg2, conv_w.astype(F32), a_log.reshape(1, nh2).astype(F32),
      dt_bias.reshape(1, nh2).astype(F32), jnp.asarray(tri, BF16), jnp.asarray(expand, BF16))


def _t5_buckets(rel):
    nb = REL_BUCKETS // 2
    max_exact = nb // 2
    ret = (rel > 0).astype(np.int32) * nb
    n = np.abs(rel)
    large = max_exact + (np.log(np.maximum(n, 1) / max_exact) / np.log(REL_MAX_DIST / max_exact)
                         * (nb - max_exact)).astype(np.int32)
    large = np.minimum(large, nb - 1)
    return (ret + np.where(n < max_exact, n, large)).astype(np.int32)


_ATTN_GROUPS = ((0, 1, True, False), (0, 1, False, True), (2, 3, True, True), (2, 3, False, False))
_ATTN_GROUP_HEADS = ((0, 2), (1, 3), (4, 6), (5, 7))
ATTN_BLOCKS_PER_STEP = 8
ATTN_WAVE_BLOCKS = 4


def _attn_kernel(q_ref, kp_ref, kc_ref, kn_ref, bias_ref, sink_ref, o_ref):
    T = ATTN_BLOCK
    nq = ATTN_BLOCKS_PER_STEP
    first_blk = pl.program_id(1) * nq
    last_blk = pl.num_programs(1) * nq - 1
    kv_all = jnp.concatenate([kp_ref[0], kc_ref[0], kn_ref[0]], axis=0)
    ones = jnp.ones((3 * T, LANES), BF16)
    low_q = _lane_is_low((T, LANES))
    zero = jnp.zeros((T, LANES), BF16)
    groups = range(len(_ATTN_GROUPS))
    kvs, scores, maxes, res = {}, {}, {}, {}

    def score_stage(blocks):
        for qb in blocks:
            kv = kv_all[qb * T:(qb + 3) * T]
            kvs[qb, False] = (kv[:, 0 * LANES:1 * LANES],
                              jnp.concatenate([kv[:, 1 * LANES:2 * LANES], ones], axis=1))
            kvs[qb, True] = (kv[:, 2 * LANES:3 * LANES],
                             jnp.concatenate([kv[:, 3 * LANES:4 * LANES], ones], axis=1))
            rq = slice(qb * T, (qb + 1) * T)
            blk = first_blk + qb
            edge = jnp.where(blk == 0, 0, jnp.where(blk == last_blk, 2, 1))
            for gi in groups:
                pa, pb, low, swapped = _ATTN_GROUPS[gi]
                sel = low_q if low else jnp.logical_not(low_q)
                lhs = jnp.concatenate([jnp.where(sel, q_ref[0, rq, pa * LANES:(pa + 1) * LANES], zero),
                                       jnp.where(sel, q_ref[0, rq, pb * LANES:(pb + 1) * LANES], zero)], axis=0)
                scores[qb, gi] = _dot_nt(lhs, kvs[qb, swapped][0]) + bias_ref[edge, gi]

    def softmax_stages(blocks):
        probs = [(qb, gi) for qb in blocks for gi in groups]
        for qb, gi in probs:
            s = scores.pop((qb, gi))
            m = jnp.maximum(jnp.max(s, axis=-1, keepdims=True), sink_ref[gi])
            maxes[qb, gi] = m
            scores[qb, gi] = jnp.exp2(s - m).astype(BF16)
        for qb, gi in probs:
            res[qb, gi] = _dot(scores.pop((qb, gi)), kvs[qb, _ATTN_GROUPS[gi][3]][1])
        for qb, gi in probs:
            r = res[qb, gi]
            den = r[:, LANES:] + jnp.exp2(sink_ref[gi] - maxes[qb, gi])
            res[qb, gi] = r[:, :LANES] * (1.0 / den)
        for qb in blocks:
            rq = slice(qb * T, (qb + 1) * T)
            for pc, (ge, go) in enumerate(((0, 1), (0, 1), (2, 3), (2, 3))):
                r0 = (pc % 2) * T
                out = jnp.where(low_q, res[qb, ge][r0:r0 + T], res[qb, go][r0:r0 + T])
                o_ref[0, rq, pc * LANES:(pc + 1) * LANES] = out.astype(BF16)

    waves = [range(w0, min(w0 + ATTN_WAVE_BLOCKS, nq)) for w0 in range(0, nq, ATTN_WAVE_BLOCKS)]
    score_stage(waves[0])
    for w, blocks in enumerate(waves):
        if w + 1 < len(waves):
            score_stage(waves[w + 1])
        softmax_stages(blocks)


def _attn(aq, akv, rel_table, sink):
    B, S, _ = aq.shape
    T = ATTN_BLOCK
    nb = S // T
    assert nb >= 2
    t_idx = np.arange(T)[:, None]
    j_idx = np.arange(3 * T)[None, :]
    rel = j_idx - T - t_idx
    onehot = jnp.asarray(np.eye(REL_BUCKETS, dtype=np.float32)[_t5_buckets(rel)])
    bias = jnp.einsum("tjr,rh->htj", onehot, rel_table.astype(F32), precision=lax.Precision.HIGHEST) * LOG2E
    in_win = np.abs(rel) <= WINDOW
    edge_ok = np.stack([in_win & (j_idx >= T), in_win, in_win & (j_idx < 2 * T)])
    bias = jnp.where(jnp.asarray(edge_ok)[:, None], bias[None], NEG)
    bias_g = jnp.stack([jnp.concatenate([bias[:, a], bias[:, b]], axis=1) for a, b in _ATTN_GROUP_HEADS], axis=1)
    sk = sink.astype(F32) * LOG2E
    sink_g = jnp.stack([jnp.concatenate([jnp.full((T, 1), 1.0) * sk[a], jnp.full((T, 1), 1.0) * sk[b]], axis=0)
                        for a, b in _ATTN_GROUP_HEADS])
    kv_w = akv.shape[-1]
    nq = ATTN_BLOCKS_PER_STEP
    assert nb % nq == 0
    return pl.pallas_call(
        _attn_kernel,
        grid=(B, nb // nq),
        in_specs=[pl.BlockSpec((1, nq * T, ATTN_Q), lambda b, i: (b, i, 0)),
                  pl.BlockSpec((1, T, kv_w), lambda b, i: (b, jnp.maximum(i * nq - 1, 0), 0)),
                  pl.BlockSpec((1, nq * T, kv_w), lambda b, i: (b, i, 0)),
                  pl.BlockSpec((1, T, kv_w), lambda b, i: (b, jnp.minimum((i + 1) * nq, nb - 1), 0)),
                  _resident((3, 4, 2 * T, 3 * T)), _resident((4, 2 * T, 1))],
        out_specs=pl.BlockSpec((1, nq * T, ATTN_Q), lambda b, i: (b, i, 0)),
        out_shape=jax.ShapeDtypeStruct((B, S, ATTN_Q), BF16),
        compiler_params=_cparams(("parallel", "parallel")),
        name="attn",
    )(aq, akv, akv, akv, bias_g, sink_g)


DN_TILE_HEADS = 2
DN_TILE = DN_TILE_HEADS * HALF
DN_TILES = DN_HEADS // DN_TILE_HEADS
DN_PAIRS = DN_HEADS // 2
DN_CHUNKS_PER_STEP = 8
DN_WAVE_CHUNKS = 2
DN_SCAN_CHUNKS = 16


def _head_of_lane(shape):
    return lax.broadcasted_iota(jnp.int32, shape, len(shape) - 1) // HALF


def _block_diag(x, head):
    zero = jnp.zeros_like(x)
    return jnp.concatenate([jnp.where(head == h, x, zero) for h in range(DN_TILE_HEADS)], axis=0)


_PK_U, _PK_W, _PK_QKD, _PK_QIN = range(4)


def _packed_cols(kind, tl):
    c0 = kind * DN_QK + tl * DN_TILE
    return slice(c0, c0 + DN_TILE)


def _dn_chunk_kernel(qkv_ref, bx_ref, gx_ref, gt_ref, pk_ref, kupt_ref, egt_ref):
    C = DN_CHUNK
    TW = DN_TILE
    head = _head_of_lane((C, TW))
    r = lax.broadcasted_iota(jnp.int32, (C, TW), 0)
    m = lax.broadcasted_iota(jnp.int32, (C, TW), 1) % HALF
    eye = jnp.where(r == m, 1.0, 0.0)
    dir_w = len((_PK_U, _PK_W, _PK_QKD, _PK_QIN)) * DN_QK

    def put(kind, d, rows, tl, val):
        c = _packed_cols(kind, tl)
        pk_ref[0, rows, d * dir_w + c.start:d * dir_w + c.stop] = val.astype(BF16)

    def prepare(ch):
        probs = []
        rows = slice(ch * C, (ch + 1) * C)
        for tl in range(DN_TILES):
            q4 = qkv_ref[0, rows, tl * TW:(tl + 1) * TW]
            k4 = qkv_ref[0, rows, DN_QK + tl * TW:DN_QK + (tl + 1) * TW]
            v4 = qkv_ref[0, rows, 2 * DN_QK + tl * TW:2 * DN_QK + (tl + 1) * TW]
            q4f, k4f, v4f = q4.astype(F32), k4.astype(F32), v4.astype(F32)
            qkk = _dot_nt(jnp.concatenate([q4, k4], axis=0), _block_diag(k4, head))
            qk, kk = qkk[:C], qkk[C:]
            for d in range(2):
                col = slice(d * DN_QK + tl * TW, d * DN_QK + (tl + 1) * TW)
                beta4 = bx_ref[0, rows, col]
                gcol4 = gx_ref[0, rows, col]
                h0 = d * DN_HEADS + tl * DN_TILE_HEADS
                grow4 = jnp.concatenate([gt_ref[0, ch, h0 + h:h0 + h + 1, :] for h in range(DN_TILE_HEADS)],
                                        axis=1)
                incl = (r >= m) if d == 0 else (r <= m)
                strict = (r > m) if d == 0 else (r < m)
                decay = jnp.where(incl, jnp.exp(jnp.where(incl, gcol4 - grow4, 0.0)), 0.0)
                gtot4 = gcol4[C - 1:C] if d == 0 else gcol4[0:1]
                e_col = jnp.exp(gcol4)
                p = jnp.where(strict, -(kk * beta4 * decay), 0.0)
                vb = (v4f * beta4).astype(BF16)
                kbe = (k4f * beta4 * e_col).astype(BF16)
                rhs = jnp.concatenate([_block_diag(vb, head), _block_diag(kbe, head)], axis=1)
                put(_PK_QKD, d, rows, tl, qk * decay)
                put(_PK_QIN, d, rows, tl, q4f * e_col)
                kupt_ref[0, ch, d, tl] = (k4f * jnp.exp(gtot4 - gcol4)).T.astype(BF16)
                egt_ref[0, ch, d, tl:tl + 1, :] = jnp.exp(gtot4)
                probs.append([p, eye + p, rhs, rows, d, tl])
        return probs

    def solve(probs):
        for pb in probs:
            pb[0] = _dot(pb[0].astype(BF16), _block_diag(pb[0].astype(BF16), head))
        for _ in range(4):
            for pb in probs:
                both = _dot(jnp.concatenate([pb[1], pb[0]], axis=0).astype(BF16),
                            _block_diag(pb[0].astype(BF16), head))
                pb[1] = pb[1] + both[:C]
                pb[0] = both[C:]
        for p, t, rhs, rows, d, tl in probs:
            t = t + _dot(t.astype(BF16), _block_diag(p.astype(BF16), head))
            sol = _dot(t.astype(BF16), rhs)
            put(_PK_U, d, rows, tl, sol[:, :TW])
            put(_PK_W, d, rows, tl, sol[:, TW:])

    waves = [range(c0, min(c0 + DN_WAVE_CHUNKS, DN_CHUNKS_PER_STEP))
             for c0 in range(0, DN_CHUNKS_PER_STEP, DN_WAVE_CHUNKS)]
    ready = [pb for ch in waves[0] for pb in prepare(ch)]
    for w in range(len(waves)):
        upcoming = [pb for ch in waves[w + 1] for pb in prepare(ch)] if w + 1 < len(waves) else []
        solve(ready)
        ready = upcoming


def _dn_chunk(qkvn, betax, gcx, gct):
    B, S, W = qkvn.shape
    C = DN_CHUNK
    cps = DN_CHUNKS_PER_STEP
    nc = S // C
    wide = 2 * DN_QK
    blk = lambda w_: pl.BlockSpec((1, cps * C, w_), lambda b, i: (b, i, 0))
    return pl.pallas_call(
        _dn_chunk_kernel,
        grid=(B, nc // cps),
        in_specs=[blk(W), blk(wide), blk(wide),
                  pl.BlockSpec((1, cps, 2 * DN_HEADS, C), lambda b, i: (b, i, 0, 0))],
        out_specs=[blk(4 * wide),
                   pl.BlockSpec((1, cps, 2, DN_TILES, DN_TILE, C), lambda b, i: (b, i, 0, 0, 0, 0)),
                   pl.BlockSpec((1, cps, 2, DN_TILES, DN_TILE), lambda b, i: (b, i, 0, 0, 0))],
        out_shape=[jax.ShapeDtypeStruct((B, S, 4 * wide), BF16),
                   jax.ShapeDtypeStruct((B, nc, 2, DN_TILES, DN_TILE, C), BF16),
                   jax.ShapeDtypeStruct((B, nc, 2, DN_TILES, DN_TILE), F32)],
        compiler_params=_cparams(("parallel", "parallel")),
        name="dn_chunk",
    )(qkvn, betax, gcx, gct)


def _dn_scan_kernel(pkf, kuptf, egtf, pkb, kuptb, egtb, of_ref, ob_ref, st_ref):
    @pl.when(pl.program_id(1) == 0)
    def _():
        st_ref[...] = jnp.zeros_like(st_ref)

    C = DN_CHUNK
    TW = DN_TILE
    head = _head_of_lane((C, TW))
    rr = lax.broadcasted_iota(jnp.int32, (LANES, LANES), 0) < HALF
    cc = lax.broadcasted_iota(jnp.int32, (LANES, LANES), 1) < HALF
    on_diag = rr == cc
    zpad = jnp.zeros((LANES, LANES), BF16)
    ppt = TW // LANES
    dirs = ((pkf, kuptf, egtf, of_ref), (pkb, kuptb, egtb, ob_ref))
    tile_ids = [(d, tl) for d in range(2) for tl in range(DN_TILES)]
    state = {(d, pr): st_ref[d, pr] for d in range(2) for pr in range(DN_PAIRS)}
    for step in range(DN_SCAN_CHUNKS):
        ws, vb = {}, {}
        for d, tl in tile_ids:
            pk_ref, kupt_ref, egt_ref, o_ref = dirs[d]
            ch = step if d == 0 else DN_SCAN_CHUNKS - 1 - step
            rows = slice(ch * C, (ch + 1) * C)
            lhs = jnp.concatenate([pk_ref[0, rows, _packed_cols(_PK_W, tl)],
                                   pk_ref[0, rows, _packed_cols(_PK_QIN, tl)]], axis=0)
            blocks = [state[d, ppt * tl + i].astype(BF16) for i in range(ppt)]
            s_tile = jnp.concatenate(
                [jnp.concatenate([blocks[i] if j == i else zpad for j in range(ppt)], axis=1) for i in range(ppt)],
                axis=0) if ppt > 1 else blocks[0]
            ws[d, tl] = _dot(lhs, s_tile)
        for d, tl in tile_ids:
            pk_ref = dirs[d][0]
            ch = step if d == 0 else DN_SCAN_CHUNKS - 1 - step
            rows = slice(ch * C, (ch + 1) * C)
            vb[d, tl] = (pk_ref[0, rows, _packed_cols(_PK_U, tl)].astype(F32) - ws[d, tl][:C]).astype(BF16)
        for d, tl in tile_ids:
            pk_ref, kupt_ref, egt_ref, o_ref = dirs[d]
            ch = step if d == 0 else DN_SCAN_CHUNKS - 1 - step
            rows = slice(ch * C, (ch + 1) * C)
            sl = slice(tl * TW, (tl + 1) * TW)
            for part in range(ppt):
                ls = slice(part * LANES, (part + 1) * LANES)
                upd = _dot(kupt_ref[0, ch, 0, tl, ls, :], vb[d, tl][:, ls])
                pr = ppt * tl + part
                state[d, pr] = state[d, pr] * egt_ref[0, ch, 0, tl:tl + 1, ls] + jnp.where(on_diag, upd, 0.0)
            o4 = ws[d, tl][C:] + _dot(pk_ref[0, rows, _packed_cols(_PK_QKD, tl)], _block_diag(vb[d, tl], head))
            o_ref[0, rows, sl] = o4.astype(BF16)
    for dp, s in state.items():
        st_ref[dp[0], dp[1]] = s


def _dn_scan(packed, kupt, egt):
    B, S, pw = packed.shape
    C = DN_CHUNK
    sc = DN_SCAN_CHUNKS
    nb = S // (C * sc)
    fwd3 = lambda b, i: (b, i, 0)
    bwd3 = lambda b, i: (b, nb - 1 - i, 1)
    row_f = pl.BlockSpec((1, sc * C, pw // 2), fwd3)
    row_b = pl.BlockSpec((1, sc * C, pw // 2), bwd3)
    kup_f = pl.BlockSpec((1, sc, 1, DN_TILES, DN_TILE, C), lambda b, i: (b, i, 0, 0, 0, 0))
    kup_b = pl.BlockSpec((1, sc, 1, DN_TILES, DN_TILE, C), lambda b, i: (b, nb - 1 - i, 1, 0, 0, 0))
    egt_f = pl.BlockSpec((1, sc, 1, DN_TILES, DN_TILE), lambda b, i: (b, i, 0, 0, 0))
    egt_b = pl.BlockSpec((1, sc, 1, DN_TILES, DN_TILE), lambda b, i: (b, nb - 1 - i, 1, 0, 0))
    return pl.pallas_call(
        _dn_scan_kernel,
        grid=(B, nb),
        in_specs=[row_f, kup_f, egt_f, row_b, kup_b, egt_b],
        out_specs=[pl.BlockSpec((1, sc * C, DN_V), fwd3),
                   pl.BlockSpec((1, sc * C, DN_V), lambda b, i: (b, nb - 1 - i, 0))],
        out_shape=[jax.ShapeDtypeStruct((B, S, DN_V), BF16), jax.ShapeDtypeStruct((B, S, DN_V), BF16)],
        scratch_shapes=[pltpu.VMEM((2, DN_PAIRS, LANES, LANES), F32)],
        compiler_params=_cparams(("parallel", "arbitrary")),
        name="dn_scan",
    )(packed, kupt, egt, packed, kupt, egt)


def _mem_kv_kernel(m_ref, g_ref, w_ref, kg_ref, k_ref, v_ref):
    x = m_ref[...]
    ms = jnp.mean(x * x, axis=-1, keepdims=True)
    h = ((x * lax.rsqrt(ms + EPS)) * g_ref[...]).astype(BF16)
    kv = _dot(h, w_ref[...])
    for hd in range(MEM_HEADS):
        kh = kv[:, hd * LANES:(hd + 1) * LANES]
        kms = jnp.mean(kh * kh, axis=-1, keepdims=True)
        k_ref[:, hd * LANES:(hd + 1) * LANES] = ((kh * lax.rsqrt(kms + EPS)) * kg_ref[...]).astype(BF16)
    v_ref[...] = kv[:, MEM_Q:].astype(BF16)


def _mem_kv(mem2, g, w_kv, kg):
    n = mem2.shape[0]
    return pl.pallas_call(
        _mem_kv_kernel,
        grid=(n // N_MEM,),
        in_specs=[pl.BlockSpec((N_MEM, D_MODEL), lambda i: (i, 0)),
                  pl.BlockSpec((1, D_MODEL), lambda i: (0, 0)),
                  pl.BlockSpec((D_MODEL, 2 * MEM_Q), lambda i: (0, 0)),
                  pl.BlockSpec((1, MEM_HEAD_DIM), lambda i: (0, 0))],
        out_specs=[pl.BlockSpec((N_MEM, MEM_Q), lambda i: (i, 0)), pl.BlockSpec((N_MEM, MEM_Q), lambda i: (i, 0))],
        out_shape=[jax.ShapeDtypeStruct((n, MEM_Q), BF16), jax.ShapeDtypeStruct((n, MEM_Q), BF16)],
        compiler_params=_cparams(("parallel",)),
        name="mem_kv",
    )(mem2, g, w_kv, kg)


def _mem_attend(q_ref, k_ref, v_ref, qg_ref):
    scale = MEM_HEAD_DIM ** -0.5 * LOG2E
    ones = jnp.ones((N_MEM, LANES), BF16)
    heads = []
    for hd in range(MEM_HEADS):
        sl = slice(hd * LANES, (hd + 1) * LANES)
        q = q_ref[:, sl].astype(F32)
        qms = jnp.mean(q * q, axis=-1, keepdims=True)
        qn = ((q * lax.rsqrt(qms + EPS)) * qg_ref[...] * scale).astype(BF16)
        s = _dot_nt(qn, k_ref[0, :, sl])
        p = jnp.exp2(s - jnp.max(s, axis=-1, keepdims=True))
        r = _dot(p.astype(BF16), jnp.concatenate([v_ref[0, :, sl], ones], axis=1))
        heads.append((r[:, :LANES] * (1.0 / r[:, LANES:])).astype(BF16))
    return jnp.concatenate(heads, axis=1)


def _merge_kernel(x_ref, ya_ref, of_ref, ob_ref, z_ref, mq_ref, mk_ref, mv_ref, mqg_ref, ng_ref, wg_ref, og_ref,
                  wa_ref, wd_ref, wm_ref, wo_ref, o_ref):
    y_mem = _mem_attend(mq_ref, mk_ref, mv_ref, mqg_ref)
    x = x_ref[...]
    ms = jnp.mean(x * x, axis=-1, keepdims=True)
    h = ((x * lax.rsqrt(ms + EPS)) * ng_ref[...]).astype(BF16)
    o = of_ref[...].astype(F32) + ob_ref[...].astype(F32)
    z = z_ref[...].astype(F32)
    og = og_ref[...]
    parts = []
    for pc in range(DN_V // LANES):
        os_ = o[:, pc * LANES:(pc + 1) * LANES]
        on = os_ * lax.rsqrt(_half_sums(os_ * os_) * (1.0 / DN_VALUE_DIM) + EPS) * og
        parts.append((on * _silu(z[:, pc * LANES:(pc + 1) * LANES])).astype(BF16))
    y_dn = jnp.concatenate(parts, axis=1)
    def gate(b):
        return _sigmoid(_dot(h, wg_ref[:, b * D_MODEL:(b + 1) * D_MODEL]))

    merged = (gate(0) * _dot(ya_ref[...], wa_ref[...])
              + gate(1) * _dot(y_dn, wd_ref[...])
              + gate(2) * _dot(y_mem, wm_ref[...]))
    o_ref[...] = x + _dot(merged.astype(BF16), wo_ref[...])


def _merge(x2, ya, of, ob, z, mq, mk, mv, mqg, ng, wg, og, wa, wd, wm, wo):
    n = x2.shape[0]
    tm = ROW_TILE
    tiles_per_seq = n // mk.shape[0] // tm
    row = lambda w_: pl.BlockSpec((tm, w_), lambda i: (i, 0))
    mem = pl.BlockSpec((1, N_MEM, MEM_Q), lambda i: (i // tiles_per_seq, 0, 0))
    full = lambda a, b: _resident((a, b))
    return pl.pallas_call(
        _merge_kernel,
        grid=(n // tm,),
        in_specs=[row(D_MODEL), row(ATTN_Q), row(DN_V), row(DN_V), row(DN_V), row(MEM_Q), mem, mem,
                  full(1, MEM_HEAD_DIM), full(1, D_MODEL), full(D_MODEL, N_BRANCH * D_MODEL),
                  full(1, LANES), full(ATTN_Q, D_MODEL), full(DN_V, D_MODEL), full(MEM_Q, D_MODEL),
                  full(D_MODEL, D_MODEL)],
        out_specs=row(D_MODEL),
        out_shape=jax.ShapeDtypeStruct((n, D_MODEL), F32),
        compiler_params=_cparams(("parallel",)),
        name="merge",
    )(x2, ya, of, ob, z, mq, mk, mv, mqg, ng, wg, og, wa, wd, wm, wo)


def _ffn_kernel(xp_ref, xc_ref, xn_ref, g_ref, wu_ref, cw_ref, cb_ref, wd_ref, o_ref, act_ref):
    i = pl.program_id(1)
    nt = pl.num_programs(1)
    tm = xc_ref.shape[1]
    xc = xc_ref[0]
    prev = jnp.where(i == 0, 0.0, xp_ref[0])
    nxt = jnp.where(i == nt - 1, 0.0, xn_ref[0])
    xe = jnp.concatenate([prev, xc, nxt], axis=0)
    ms = jnp.mean(xe * xe, axis=-1, keepdims=True)
    h = ((xe * lax.rsqrt(ms + EPS)) * g_ref[...]).astype(BF16)

    def both(ref, rows_, c0):
        return jnp.concatenate([ref[rows_, c0:c0 + FF_CHUNK], ref[rows_, D_FF + c0:D_FF + c0 + FF_CHUNK]], axis=1)

    for c in range(D_FF // FF_CHUNK):
        c0 = c * FF_CHUNK
        u = _dot(h, both(wu_ref, slice(None), c0))
        y = None
        for j in range(FFN_CONV):
            term = _shift_rows(u, FFN_CONV // 2 - j)[HALO:HALO + tm] * both(cw_ref, slice(j, j + 1), c0)
            y = term if y is None else y + term
        y = y + both(cb_ref, slice(None), c0)
        act_ref[:, c0:c0 + FF_CHUNK] = (_silu(y[:, :FF_CHUNK]) * y[:, FF_CHUNK:]).astype(BF16)
    o_ref[0] = xc + _dot(act_ref[...], wd_ref[...])


def _ffn(x1, g, wu, cw, cb, wd):
    B, S, _ = x1.shape
    tm = ROW_TILE
    nt = S // tm
    hb = tm // HALO
    return pl.pallas_call(
        _ffn_kernel,
        grid=(B, nt),
        in_specs=[pl.BlockSpec((1, HALO, D_MODEL), lambda b, i: (b, jnp.maximum(i * hb - 1, 0), 0)),
                  pl.BlockSpec((1, tm, D_MODEL), lambda b, i: (b, i, 0)),
                  pl.BlockSpec((1, HALO, D_MODEL), lambda b, i: (b, jnp.minimum((i + 1) * hb, S // HALO - 1), 0)),
                  _resident((1, D_MODEL)), _resident((D_MODEL, 2 * D_FF)), _resident((FFN_CONV, 2 * D_FF)),
                  _resident((1, 2 * D_FF)), _resident((D_FF, D_MODEL))],
        out_specs=pl.BlockSpec((1, tm, D_MODEL), lambda b, i: (b, i, 0)),
        out_shape=jax.ShapeDtypeStruct((B, S, D_MODEL), F32),
        scratch_shapes=[pltpu.VMEM((tm, D_FF), BF16)],
        compiler_params=_cparams(("parallel", "parallel")),
        name="ffn",
    )(x1, x1, x1, g, wu, cw, cb, wd)


def _permute_w_in(w):
    idx = np.cumsum((0,) + IN_SPLITS)
    wb = w.astype(BF16)
    return wb[:, :idx[7]], wb[:, idx[9]:idx[10]], wb[:, idx[7]:idx[9]], wb[:, idx[10]:]


def _layer(x, mem, rel_bias_table, p):
    B, S, D = x.shape
    n = B * S
    x2 = x.reshape(n, D)
    row = lambda a: a.reshape(1, -1).astype(F32)
    tile2 = lambda a: jnp.tile(a.astype(F32), 2)[None]
    w_main, w_mq, w_ba, w_gate = _permute_w_in(p["w_in"])
    aq, akv, dz, mq, qkvn, betax, gcx, gct = _inproj(
        x, row(p["norm_mix_g"]), w_main, w_mq, w_ba, tile2(p["attn_q_norm_g"]), tile2(p["attn_k_norm_g"]),
        p["dn_conv_w"], p["dn_a_log"], p["dn_dt_bias"])
    y_attn = _attn(aq, akv, rel_bias_table, p["attn_sink"])
    o_f, o_b = _dn_scan(*_dn_chunk(qkvn, betax, gcx, gct))
    mk, mv = _mem_kv(mem.reshape(B * N_MEM, D), row(p["mem_norm_g"]), p["mem_w_kv"].astype(BF16),
                     row(p["mem_k_norm_g"]))
    og2 = jnp.tile(p["dn_out_norm_g"].astype(F32), 2)[None]
    x1 = _merge(x2, y_attn.reshape(n, -1), o_f.reshape(n, -1), o_b.reshape(n, -1), dz.reshape(n, -1),
                mq.reshape(n, -1), mk.reshape(B, N_MEM, -1), mv.reshape(B, N_MEM, -1), row(p["mem_q_norm_g"]),
                row(p["norm_mix_g"]), w_gate, og2, p["w_br_attn"].astype(BF16), p["w_br_dn"].astype(BF16),
                p["w_br_mem"].astype(BF16), p["w_out"].astype(BF16))
    return _ffn(x1.reshape(B, S, D), row(p["norm_ffn_g"]), p["ffn_w_up"].astype(BF16), p["ffn_conv_w"].astype(F32),
                row(p["ffn_conv_b"]), p["ffn_w_down"].astype(BF16))


_LAYER_PARAMS = ("norm_mix_g", "w_in", "attn_q_norm_g", "attn_k_norm_g", "attn_sink", "dn_conv_w", "dn_a_log",
                 "dn_dt_bias", "dn_out_norm_g", "mem_norm_g", "mem_w_kv", "mem_q_norm_g", "mem_k_norm_g",
                 "w_br_attn", "w_br_dn", "w_br_mem", "w_out", "norm_ffn_g", "ffn_w_up", "ffn_conv_w", "ffn_conv_b",
                 "ffn_w_down")


def kernel(x, mem, rel_bias_table, norm_mix_g, w_in, attn_q_norm_g, attn_k_norm_g, attn_sink, dn_conv_w, dn_a_log,
           dn_dt_bias, dn_out_norm_g, mem_norm_g, mem_w_kv, mem_q_norm_g, mem_k_norm_g, w_br_attn, w_br_dn,
           w_br_mem, w_out, norm_ffn_g, ffn_w_up, ffn_conv_w, ffn_conv_b, ffn_w_down):
    stacked = dict(zip(_LAYER_PARAMS, (norm_mix_g, w_in, attn_q_norm_g, attn_k_norm_g, attn_sink, dn_conv_w,
                                       dn_a_log, dn_dt_bias, dn_out_norm_g, mem_norm_g, mem_w_kv, mem_q_norm_g,
                                       mem_k_norm_g, w_br_attn, w_br_dn, w_br_mem, w_out, norm_ffn_g, ffn_w_up,
                                       ffn_conv_w, ffn_conv_b, ffn_w_down)))
    depth = w_in.shape[0]
    for l in range(depth):
        x = _layer(x, mem, rel_bias_table, {k: v[l] for k, v in stacked.items()})
    return x
```

```python
import math

import numpy as np
import jax
import jax.numpy as jnp
from jax import lax
from jax.experimental import pallas as pl
from jax.experimental.pallas import tpu as pltpu

F32 = jnp.float32
BF16 = jnp.bfloat16

EPS = 1e-6
D_MODEL = 1024
N_MEM = 256
ATTN_HEADS = 8
ATTN_KV_HEADS = 2
ATTN_HEAD_DIM = 64
WINDOW = 128
ATTN_BLOCK = 128
REL_BUCKETS = 32
REL_MAX_DIST = 128
DN_HEADS = 8
DN_KEY_DIM = 64
DN_VALUE_DIM = 64
DN_CONV = 5
DN_CHUNK = 64
MEM_HEADS = 4
MEM_HEAD_DIM = 128
D_FF = 2816
FFN_CONV = 3
N_BRANCH = 3

ATTN_Q = ATTN_HEADS * ATTN_HEAD_DIM
ATTN_KV = ATTN_KV_HEADS * ATTN_HEAD_DIM
DN_QK = DN_HEADS * DN_KEY_DIM
DN_V = DN_HEADS * DN_VALUE_DIM
MEM_Q = MEM_HEADS * MEM_HEAD_DIM
IN_SPLITS = (ATTN_Q, ATTN_KV, ATTN_KV, DN_QK, DN_QK, DN_V, DN_V, 2 * DN_HEADS, 2 * DN_HEADS, MEM_Q,
             N_BRANCH * D_MODEL)

LANES = 128
SUBLANES = 8
HALF = 64
HALO = SUBLANES
NEG = -1e30
VMEM_LIMIT = 56 * 1024 * 1024
LOG2E = math.log2(math.e)

ROW_TILE = 1024
FF_CHUNK = 256


def _cparams(sem):
    return pltpu.CompilerParams(dimension_semantics=sem, vmem_limit_bytes=VMEM_LIMIT)


def _resident(shape):
    zeros = (0,) * len(shape)
    return pl.BlockSpec(shape, lambda *_: zeros, pipeline_mode=pl.Buffered(1))


def _dot(a, b):
    return jnp.dot(a, b, preferred_element_type=F32)


def _dot_nt(a, b):
    return lax.dot_general(a, b, (((1,), (1,)), ((), ())), preferred_element_type=F32)


def _lane_is_low(shape):
    lane = lax.broadcasted_iota(jnp.int32, shape, len(shape) - 1)
    return (lane % LANES) < HALF


def _half_sums(sq):
    low = _lane_is_low(sq.shape)
    s_lo = jnp.sum(jnp.where(low, sq, 0.0), axis=-1, keepdims=True)
    s_hi = jnp.sum(jnp.where(low, 0.0, sq), axis=-1, keepdims=True)
    return jnp.where(low, s_lo, s_hi)


def _shift_rows(x, s):
    if s == 0:
        return x
    rr, cc = x.shape
    x3 = x.reshape(rr // SUBLANES, SUBLANES, cc)
    rot = pltpu.roll(x3, s % SUBLANES, axis=1)
    sub = lax.broadcasted_iota(jnp.int32, x3.shape, 1)
    if s > 0:
        other = jnp.concatenate([rot[-1:], rot[:-1]], axis=0)
        y3 = jnp.where(sub < s, other, rot)
    else:
        other = jnp.concatenate([rot[1:], rot[:1]], axis=0)
        y3 = jnp.where(sub >= SUBLANES + s, other, rot)
    return y3.reshape(rr, cc)


def _silu(x):
    h = 0.5 * x
    return h * jnp.tanh(h) + h


def _sigmoid(x):
    return 0.5 * jnp.tanh(0.5 * x) + 0.5


_C_AQ = (0, 512)
_C_AKV = (512, 768)
_C_DQKV = (768, 2304)
_C_DZ = (2304, 2816)
_N_IN = 2816


def _head_rmsnorm(t, gain2):
    return t * lax.rsqrt(_half_sums(t * t) * (1.0 / ATTN_HEAD_DIM) + EPS) * gain2


def _split_hi_lo(x):
    hi = x.astype(BF16)
    lo = (x - hi.astype(F32)).astype(BF16)
    return hi, lo


def _inproj_kernel(xp_ref, xc_ref, xn_ref, g_ref, w_ref, wmq_ref, wba_ref, qg_ref, kg_ref, cw_ref, alog_ref,
                   dtb_ref, tri_ref, expand_ref, aq_ref, akv_ref, dz_ref, mq_ref, qkv_ref, betax_ref, gcx_ref, gct_ref):
    i = pl.program_id(1)
    nt = pl.num_programs(1)
    tm = xc_ref.shape[1]
    prev = jnp.where(i == 0, 0.0, xp_ref[0])
    nxt = jnp.where(i == nt - 1, 0.0, xn_ref[0])
    xe = jnp.concatenate([prev, xc_ref[0], nxt], axis=0)
    ms = jnp.mean(xe * xe, axis=-1, keepdims=True)
    he = ((xe * lax.rsqrt(ms + EPS)) * g_ref[...]).astype(BF16)
    h = he[HALO:HALO + tm]

    def proj(c):
        return _dot(h, w_ref[:, c[0]:c[1]])

    cw_chunk = 2 * LANES

    def dn_qkv_chunk(c0):
        ue = _dot(he, w_ref[:, _C_DQKV[0] + c0:_C_DQKV[0] + c0 + cw_chunk])
        acc = None
        for j in range(DN_CONV):
            term = _shift_rows(ue, DN_CONV // 2 - j)[HALO:HALO + tm] * cw_ref[j:j + 1, c0:c0 + cw_chunk]
            acc = term if acc is None else acc + term
        y = _silu(acc)
        for l0 in range(0, cw_chunk, LANES):
            ys = y[:, l0:l0 + LANES]
            if c0 < 2 * DN_QK:
                ys = ys * lax.rsqrt(_half_sums(ys * ys) + EPS)
            if c0 < DN_QK:
                ys = ys * (DN_KEY_DIM ** -0.5)
            qkv_ref[0, :, c0 + l0:c0 + l0 + LANES] = ys.astype(BF16)

    def dn_beta_decay():
        ba = _dot(h, wba_ref[...])
        nh2 = 2 * DN_HEADS
        beta = _sigmoid(ba[:, :nh2])
        z = ba[:, nh2:2 * nh2] + dtb_ref[...]
        sp = jnp.maximum(z, 0.0) + jnp.log1p(jnp.exp(-jnp.abs(z)))
        g = -jnp.exp(alog_ref[...]) * sp
        g_hi, g_lo = _split_hi_lo(g)
        g_lo2 = (g - g_hi.astype(F32) - g_lo.astype(F32)).astype(BF16)
        gg = jnp.concatenate([g_hi, g_lo, g_lo2], axis=1)
        tb = tri_ref.shape[1]
        pre = jnp.concatenate([_dot(tri_ref[0], gg[r0:r0 + tb]) for r0 in range(0, tm, tb)], axis=0)
        suf = jnp.concatenate([_dot(tri_ref[1], gg[r0:r0 + tb]) for r0 in range(0, tm, tb)], axis=0)
        lane16 = lax.broadcasted_iota(jnp.int32, (tm, nh2), 1)
        gc = jnp.where(lane16 < DN_HEADS, pre[:, :nh2] + pre[:, nh2:2 * nh2] + pre[:, 2 * nh2:],
                       suf[:, :nh2] + suf[:, nh2:2 * nh2] + suf[:, 2 * nh2:])
        b_hi, b_lo = _split_hi_lo(beta)
        c_hi, c_lo = _split_hi_lo(gc)
        c_lo2 = (gc - c_hi.astype(F32) - c_lo.astype(F32)).astype(BF16)
        bx = _dot(jnp.concatenate([b_hi, b_lo, c_hi, c_lo, c_lo2], axis=1), expand_ref[...])
        betax_ref[0] = bx[:, :nh2 * HALF]
        gcx_ref[0] = bx[:, nh2 * HALF:]
        gct = jnp.concatenate([gc, jnp.zeros((tm, LANES - nh2), F32)], axis=1).T
        for c in range(tm // DN_CHUNK):
            gct_ref[0, c] = gct[:nh2, c * DN_CHUNK:(c + 1) * DN_CHUNK]

    def attn_q():
        aq = proj(_C_AQ)
        q_scale = ATTN_HEAD_DIM ** -0.5 * LOG2E
        for pc in range(ATTN_Q // LANES):
            sl = slice(pc * LANES, (pc + 1) * LANES)
            aq_ref[0, :, sl] = (_head_rmsnorm(aq[:, sl], qg_ref[...]) * q_scale).astype(BF16)

    def attn_kv():
        akv = proj(_C_AKV)
        kn = _head_rmsnorm(akv[:, :LANES], kg_ref[...])
        av = akv[:, LANES:]
        akv_ref[0, :, 0 * LANES:1 * LANES] = kn.astype(BF16)
        akv_ref[0, :, 1 * LANES:2 * LANES] = av.astype(BF16)
        akv_ref[0, :, 2 * LANES:3 * LANES] = pltpu.roll(kn, HALF, axis=1).astype(BF16)
        akv_ref[0, :, 3 * LANES:4 * LANES] = pltpu.roll(av, HALF, axis=1).astype(BF16)

    def dn_z():
        dz_ref[0] = proj(_C_DZ).astype(BF16)

    def mem_q():
        mq_ref[0] = _dot(h, wmq_ref[...]).astype(BF16)

    for c0 in range(0, 2 * DN_QK + DN_V, cw_chunk):
        dn_qkv_chunk(c0)
    for part in (dn_beta_decay, attn_q, attn_kv, dn_z, mem_q):
        part()


def _inproj(x, g, w, w_mq, w_ba, qg2, kg2, conv_w, a_log, dt_bias):
    B, S, D = x.shape
    tm = ROW_TILE
    nt = S // tm
    hb = tm // HALO
    nh2 = 2 * DN_HEADS
    tb = 2 * LANES
    r = np.arange(tb)
    same = (r[:, None] // DN_CHUNK) == (r[None, :] // DN_CHUNK)
    tri = np.stack([same & (r[:, None] >= r[None, :]), same & (r[:, None] <= r[None, :])]).astype(np.float32)
    rep = np.repeat(np.eye(nh2, dtype=np.float32), HALF, axis=1)
    zero = np.zeros_like(rep)
    expand = np.block([[rep, zero]] * 2 + [[zero, rep]] * 3)
    blk = lambda w_: pl.BlockSpec((1, tm, w_), lambda b, i: (b, i, 0))
    outs = [(ATTN_Q, BF16), (4 * LANES, BF16), (DN_V, BF16), (MEM_Q, BF16), (2 * DN_QK + DN_V, BF16),
            (nh2 * HALF, F32), (nh2 * HALF, F32)]
    return pl.pallas_call(
        _inproj_kernel,
        grid=(B, nt),
        in_specs=[pl.BlockSpec((1, HALO, D), lambda b, i: (b, jnp.maximum(i * hb - 1, 0), 0)),
                  blk(D),
                  pl.BlockSpec((1, HALO, D), lambda b, i: (b, jnp.minimum((i + 1) * hb, S // HALO - 1), 0)),
                  _resident((1, D)), _resident((D, _N_IN)), _resident((D, MEM_Q)), _resident((D, 2 * nh2)),
                  _resident((1, LANES)), _resident((1, LANES)),
                  _resident((DN_CONV, 2 * DN_QK + DN_V)), _resident((1, nh2)), _resident((1, nh2)),
                  _resident((2, tb, tb)), _resident((5 * nh2, 2 * nh2 * HALF))],
        out_specs=[blk(w_) for w_, _ in outs]
        + [pl.BlockSpec((1, tm // DN_CHUNK, nh2, DN_CHUNK), lambda b, i: (b, i, 0, 0))],
        out_shape=[jax.ShapeDtypeStruct((B, S, w_), dt) for w_, dt in outs]
        + [jax.ShapeDtypeStruct((B, S // DN_CHUNK, nh2, DN_CHUNK), F32)],
        compiler_params=_cparams(("parallel", "parallel")),
        name="inproj",
    )(x, x, x, g, w, w_mq, w_ba, qg2, kg2, conv_w.astype(F32), a_log.reshape(1, nh2).astype(F32),
      dt_bias.reshape(1, nh2).astype(F32), jnp.asarray(tri, BF16), jnp.asarray(expand, BF16))


def _t5_buckets(rel):
    nb = REL_BUCKETS // 2
    max_exact = nb // 2
    ret = (rel > 0).astype(np.int32) * nb
    n = np.abs(rel)
    large = max_exact + (np.log(np.maximum(n, 1) / max_exact) / np.log(REL_MAX_DIST / max_exact)
                         * (nb - max_exact)).astype(np.int32)
    large = np.minimum(large, nb - 1)
    return (ret + np.where(n < max_exact, n, large)).astype(np.int32)


_ATTN_GROUPS = ((0, 1, True, False), (0, 1, False, True), (2, 3, True, True), (2, 3, False, False))
_ATTN_GROUP_HEADS = ((0, 2), (1, 3), (4, 6), (5, 7))
ATTN_BLOCKS_PER_STEP = 8
ATTN_WAVE_BLOCKS = 4


def _attn_kernel(q_ref, kp_ref, kc_ref, kn_ref, bias_ref, sink_ref, o_ref):
    T = ATTN_BLOCK
    nq = ATTN_BLOCKS_PER_STEP
    first_blk = pl.program_id(1) * nq
    last_blk = pl.num_programs(1) * nq - 1
    kv_all = jnp.concatenate([kp_ref[0], kc_ref[0], kn_ref[0]], axis=0)
    ones = jnp.ones((3 * T, LANES), BF16)
    low_q = _lane_is_low((T, LANES))
    zero = jnp.zeros((T, LANES), BF16)
    groups = range(len(_ATTN_GROUPS))
    kvs, scores, maxes, res = {}, {}, {}, {}

    def score_stage(blocks):
        for qb in blocks:
            kv = kv_all[qb * T:(qb + 3) * T]
            kvs[qb, False] = (kv[:, 0 * LANES:1 * LANES],
                              jnp.concatenate([kv[:, 1 * LANES:2 * LANES], ones], axis=1))
            kvs[qb, True] = (kv[:, 2 * LANES:3 * LANES],
                             jnp.concatenate([kv[:, 3 * LANES:4 * LANES], ones], axis=1))
            rq = slice(qb * T, (qb + 1) * T)
            blk = first_blk + qb
            edge = jnp.where(blk == 0, 0, jnp.where(blk == last_blk, 2, 1))
            for gi in groups:
                pa, pb, low, swapped = _ATTN_GROUPS[gi]
                sel = low_q if low else jnp.logical_not(low_q)
                lhs = jnp.concatenate([jnp.where(sel, q_ref[0, rq, pa * LANES:(pa + 1) * LANES], zero),
                                       jnp.where(sel, q_ref[0, rq, pb * LANES:(pb + 1) * LANES], zero)], axis=0)
                scores[qb, gi] = _dot_nt(lhs, kvs[qb, swapped][0]) + bias_ref[edge, gi]

    def softmax_stages(blocks):
        probs = [(qb, gi) for qb in blocks for gi in groups]
        for qb, gi in probs:
            s = scores.pop((qb, gi))
            m = jnp.maximum(jnp.max(s, axis=-1, keepdims=True), sink_ref[gi])
            maxes[qb, gi] = m
            scores[qb, gi] = jnp.exp2(s - m).astype(BF16)
        for qb, gi in probs:
            res[qb, gi] = _dot(scores.pop((qb, gi)), kvs[qb, _ATTN_GROUPS[gi][3]][1])
        for qb, gi in probs:
            r = res[qb, gi]
            den = r[:, LANES:] + jnp.exp2(sink_ref[gi] - maxes[qb, gi])
            res[qb, gi] = r[:, :LANES] * (1.0 / den)
        for qb in blocks:
            rq = slice(qb * T, (qb + 1) * T)
            for pc, (ge, go) in enumerate(((0, 1), (0, 1), (2, 3), (2, 3))):
                r0 = (pc % 2) * T
                out = jnp.where(low_q, res[qb, ge][r0:r0 + T], res[qb, go][r0:r0 + T])
                o_ref[0, rq, pc * LANES:(pc + 1) * LANES] = out.astype(BF16)

    waves = [range(w0, min(w0 + ATTN_WAVE_BLOCKS, nq)) for w0 in range(0, nq, ATTN_WAVE_BLOCKS)]
    score_stage(waves[0])
    for w, blocks in enumerate(waves):
        if w + 1 < len(waves):
            score_stage(waves[w + 1])
        softmax_stages(blocks)


def _attn(aq, akv, rel_table, sink):
    B, S, _ = aq.shape
    T = ATTN_BLOCK
    nb = S // T
    assert nb >= 2
    t_idx = np.arange(T)[:, None]
    j_idx = np.arange(3 * T)[None, :]
    rel = j_idx - T - t_idx
    onehot = jnp.asarray(np.eye(REL_BUCKETS, dtype=np.float32)[_t5_buckets(rel)])
    bias = jnp.einsum("tjr,rh->htj", onehot, rel_table.astype(F32), precision=lax.Precision.HIGHEST) * LOG2E
    in_win = np.abs(rel) <= WINDOW
    edge_ok = np.stack([in_win & (j_idx >= T), in_win, in_win & (j_idx < 2 * T)])
    bias = jnp.where(jnp.asarray(edge_ok)[:, None], bias[None], NEG)
    bias_g = jnp.stack([jnp.concatenate([bias[:, a], bias[:, b]], axis=1) for a, b in _ATTN_GROUP_HEADS], axis=1)
    sk = sink.astype(F32) * LOG2E
    sink_g = jnp.stack([jnp.concatenate([jnp.full((T, 1), 1.0) * sk[a], jnp.full((T, 1), 1.0) * sk[b]], axis=0)
                        for a, b in _ATTN_GROUP_HEADS])
    kv_w = akv.shape[-1]
    nq = ATTN_BLOCKS_PER_STEP
    assert nb % nq == 0
    return pl.pallas_call(
        _attn_kernel,
        grid=(B, nb // nq),
        in_specs=[pl.BlockSpec((1, nq * T, ATTN_Q), lambda b, i: (b, i, 0)),
                  pl.BlockSpec((1, T, kv_w), lambda b, i: (b, jnp.maximum(i * nq - 1, 0), 0)),
                  pl.BlockSpec((1, nq * T, kv_w), lambda b, i: (b, i, 0)),
                  pl.BlockSpec((1, T, kv_w), lambda b, i: (b, jnp.minimum((i + 1) * nq, nb - 1), 0)),
                  _resident((3, 4, 2 * T, 3 * T)), _resident((4, 2 * T, 1))],
        out_specs=pl.BlockSpec((1, nq * T, ATTN_Q), lambda b, i: (b, i, 0)),
        out_shape=jax.ShapeDtypeStruct((B, S, ATTN_Q), BF16),
        compiler_params=_cparams(("parallel", "parallel")),
        name="attn",
    )(aq, akv, akv, akv, bias_g, sink_g)


DN_TILE_HEADS = 2
DN_TILE = DN_TILE_HEADS * HALF
DN_TILES = DN_HEADS // DN_TILE_HEADS
DN_PAIRS = DN_HEADS // 2
DN_CHUNKS_PER_STEP = 8
DN_WAVE_CHUNKS = 2
DN_SCAN_CHUNKS = 16


def _head_of_lane(shape):
    return lax.broadcasted_iota(jnp.int32, shape, len(shape) - 1) // HALF


def _block_diag(x, head):
    zero = jnp.zeros_like(x)
    return jnp.concatenate([jnp.where(head == h, x, zero) for h in range(DN_TILE_HEADS)], axis=0)


_PK_KINDS = 5
_PK_U, _PK_W, _PK_QKD, _PK_QIN, _PK_KUP = range(_PK_KINDS)


def _packed_cols(kind, tl):
    c0 = kind * DN_QK + tl * DN_TILE
    return slice(c0, c0 + DN_TILE)


def _dn_chunk_kernel(qkv_ref, bx_ref, gx_ref, gt_ref, pk_ref, egt_ref):
    C = DN_CHUNK
    TW = DN_TILE
    head = _head_of_lane((C, TW))
    r = lax.broadcasted_iota(jnp.int32, (C, TW), 0)
    m = lax.broadcasted_iota(jnp.int32, (C, TW), 1) % HALF
    eye = jnp.where(r == m, 1.0, 0.0)
    dir_w = _PK_KINDS * DN_QK

    def put(kind, d, rows, tl, val):
        c = _packed_cols(kind, tl)
        pk_ref[0, rows, d * dir_w + c.start:d * dir_w + c.stop] = val.astype(BF16)

    def prepare(ch):
        probs = []
        rows = slice(ch * C, (ch + 1) * C)
        for tl in range(DN_TILES):
            q4 = qkv_ref[0, rows, tl * TW:(tl + 1) * TW]
            k4 = qkv_ref[0, rows, DN_QK + tl * TW:DN_QK + (tl + 1) * TW]
            v4 = qkv_ref[0, rows, 2 * DN_QK + tl * TW:2 * DN_QK + (tl + 1) * TW]
            q4f, k4f, v4f = q4.astype(F32), k4.astype(F32), v4.astype(F32)
            qkk = _dot_nt(jnp.concatenate([q4, k4], axis=0), _block_diag(k4, head))
            qk, kk = qkk[:C], qkk[C:]
            for d in range(2):
                col = slice(d * DN_QK + tl * TW, d * DN_QK + (tl + 1) * TW)
                beta4 = bx_ref[0, rows, col]
                gcol4 = gx_ref[0, rows, col]
                h0 = d * DN_HEADS + tl * DN_TILE_HEADS
                grow4 = jnp.concatenate([gt_ref[0, ch, h0 + h:h0 + h + 1, :] for h in range(DN_TILE_HEADS)],
                                        axis=1)
                incl = (r >= m) if d == 0 else (r <= m)
                strict = (r > m) if d == 0 else (r < m)
                decay = jnp.where(incl, jnp.exp(jnp.where(incl, gcol4 - grow4, 0.0)), 0.0)
                gtot4 = gcol4[C - 1:C] if d == 0 else gcol4[0:1]
                e_col = jnp.exp(gcol4)
                p = jnp.where(strict, -(kk * beta4 * decay), 0.0)
                vb = (v4f * beta4).astype(BF16)
                kbe = (k4f * beta4 * e_col).astype(BF16)
                rhs = jnp.concatenate([_block_diag(vb, head), _block_diag(kbe, head)], axis=1)
                put(_PK_QKD, d, rows, tl, qk * decay)
                put(_PK_QIN, d, rows, tl, q4f * e_col)
                put(_PK_KUP, d, rows, tl, k4f * jnp.exp(gtot4 - gcol4))
                egt_ref[0, ch, d, tl:tl + 1, :] = jnp.exp(gtot4)
                probs.append([p, eye + p, rhs, rows, d, tl])
        return probs

    def solve(probs):
        for pb in probs:
            pb[0] = _dot(pb[0].astype(BF16), _block_diag(pb[0].astype(BF16), head))
        for _ in range(4):
            for pb in probs:
                both = _dot(jnp.concatenate([pb[1], pb[0]], axis=0).astype(BF16),
                            _block_diag(pb[0].astype(BF16), head))
                pb[1] = pb[1] + both[:C]
                pb[0] = both[C:]
        for p, t, rhs, rows, d, tl in probs:
            t = t + _dot(t.astype(BF16), _block_diag(p.astype(BF16), head))
            sol = _dot(t.astype(BF16), rhs)
            put(_PK_U, d, rows, tl, sol[:, :TW])
            put(_PK_W, d, rows, tl, sol[:, TW:])

    waves = [range(c0, min(c0 + DN_WAVE_CHUNKS, DN_CHUNKS_PER_STEP))
             for c0 in range(0, DN_CHUNKS_PER_STEP, DN_WAVE_CHUNKS)]
    ready = [pb for ch in waves[0] for pb in prepare(ch)]
    for w in range(len(waves)):
        upcoming = [pb for ch in waves[w + 1] for pb in prepare(ch)] if w + 1 < len(waves) else []
        solve(ready)
        ready = upcoming


def _dn_chunk(qkvn, betax, gcx, gct):
    B, S, W = qkvn.shape
    C = DN_CHUNK
    cps = DN_CHUNKS_PER_STEP
    nc = S // C
    wide = 2 * DN_QK
    blk = lambda w_: pl.BlockSpec((1, cps * C, w_), lambda b, i: (b, i, 0))
    return pl.pallas_call(
        _dn_chunk_kernel,
        grid=(B, nc // cps),
        in_specs=[blk(W), blk(wide), blk(wide),
                  pl.BlockSpec((1, cps, 2 * DN_HEADS, C), lambda b, i: (b, i, 0, 0))],
        out_specs=[blk(_PK_KINDS * wide),
                   pl.BlockSpec((1, cps, 2, DN_TILES, DN_TILE), lambda b, i: (b, i, 0, 0, 0))],
        out_shape=[jax.ShapeDtypeStruct((B, S, _PK_KINDS * wide), BF16),
                   jax.ShapeDtypeStruct((B, nc, 2, DN_TILES, DN_TILE), F32)],
        compiler_params=_cparams(("parallel", "parallel")),
        name="dn_chunk",
    )(qkvn, betax, gcx, gct)


def _dn_scan_kernel(pkf, egtf, pkb, egtb, of_ref, ob_ref, st_ref):
    @pl.when(pl.program_id(1) == 0)
    def _():
        st_ref[...] = jnp.zeros_like(st_ref)

    C = DN_CHUNK
    TW = DN_TILE
    head = _head_of_lane((C, TW))
    rr = lax.broadcasted_iota(jnp.int32, (LANES, LANES), 0) < HALF
    cc = lax.broadcasted_iota(jnp.int32, (LANES, LANES), 1) < HALF
    on_diag = rr == cc
    zpad = jnp.zeros((LANES, LANES), BF16)
    ppt = TW // LANES
    dirs = ((pkf, egtf, of_ref), (pkb, egtb, ob_ref))
    tile_ids = [(d, tl) for d in range(2) for tl in range(DN_TILES)]
    state = {(d, pr): st_ref[d, pr] for d in range(2) for pr in range(DN_PAIRS)}
    for step in range(DN_SCAN_CHUNKS):
        ws, vb = {}, {}
        for d, tl in tile_ids:
            pk_ref, egt_ref, o_ref = dirs[d]
            ch = step if d == 0 else DN_SCAN_CHUNKS - 1 - step
            rows = slice(ch * C, (ch + 1) * C)
            lhs = jnp.concatenate([pk_ref[0, rows, _packed_cols(_PK_W, tl)],
                                   pk_ref[0, rows, _packed_cols(_PK_QIN, tl)]], axis=0)
            blocks = [state[d, ppt * tl + i].astype(BF16) for i in range(ppt)]
            s_tile = jnp.concatenate(
                [jnp.concatenate([blocks[i] if j == i else zpad for j in range(ppt)], axis=1) for i in range(ppt)],
                axis=0) if ppt > 1 else blocks[0]
            ws[d, tl] = _dot(lhs, s_tile)
        for d, tl in tile_ids:
            pk_ref = dirs[d][0]
            ch = step if d == 0 else DN_SCAN_CHUNKS - 1 - step
            rows = slice(ch * C, (ch + 1) * C)
            vb[d, tl] = (pk_ref[0, rows, _packed_cols(_PK_U, tl)].astype(F32) - ws[d, tl][:C]).astype(BF16)
        for d, tl in tile_ids:
            pk_ref, egt_ref, o_ref = dirs[d]
            ch = step if d == 0 else DN_SCAN_CHUNKS - 1 - step
            rows = slice(ch * C, (ch + 1) * C)
            sl = slice(tl * TW, (tl + 1) * TW)
            k_up_t = pk_ref[0, rows, _packed_cols(_PK_KUP, tl)].astype(F32).T.astype(BF16)
            for part in range(ppt):
                ls = slice(part * LANES, (part + 1) * LANES)
                upd = _dot(k_up_t[ls], vb[d, tl][:, ls])
                pr = ppt * tl + part
                state[d, pr] = state[d, pr] * egt_ref[0, ch, 0, tl:tl + 1, ls] + jnp.where(on_diag, upd, 0.0)
            o4 = ws[d, tl][C:] + _dot(pk_ref[0, rows, _packed_cols(_PK_QKD, tl)], _block_diag(vb[d, tl], head))
            o_ref[0, rows, sl] = o4.astype(BF16)
    for dp, s in state.items():
        st_ref[dp[0], dp[1]] = s


def _dn_scan(packed, egt):
    B, S, pw = packed.shape
    C = DN_CHUNK
    sc = DN_SCAN_CHUNKS
    nb = S // (C * sc)
    fwd3 = lambda b, i: (b, i, 0)
    bwd3 = lambda b, i: (b, nb - 1 - i, 1)
    row_f = pl.BlockSpec((1, sc * C, pw // 2), fwd3)
    row_b = pl.BlockSpec((1, sc * C, pw // 2), bwd3)
    egt_f = pl.BlockSpec((1, sc, 1, DN_TILES, DN_TILE), lambda b, i: (b, i, 0, 0, 0))
    egt_b = pl.BlockSpec((1, sc, 1, DN_TILES, DN_TILE), lambda b, i: (b, nb - 1 - i, 1, 0, 0))
    return pl.pallas_call(
        _dn_scan_kernel,
        grid=(B, nb),
        in_specs=[row_f, egt_f, row_b, egt_b],
        out_specs=[pl.BlockSpec((1, sc * C, DN_V), fwd3),
                   pl.BlockSpec((1, sc * C, DN_V), lambda b, i: (b, nb - 1 - i, 0))],
        out_shape=[jax.ShapeDtypeStruct((B, S, DN_V), BF16), jax.ShapeDtypeStruct((B, S, DN_V), BF16)],
        scratch_shapes=[pltpu.VMEM((2, DN_PAIRS, LANES, LANES), F32)],
        compiler_params=_cparams(("parallel", "arbitrary")),
        name="dn_scan",
    )(packed, egt, packed, egt)


def _mem_kv_kernel(m_ref, g_ref, w_ref, kg_ref, k_ref, v_ref):
    x = m_ref[...]
    ms = jnp.mean(x * x, axis=-1, keepdims=True)
    h = ((x * lax.rsqrt(ms + EPS)) * g_ref[...]).astype(BF16)
    kv = _dot(h, w_ref[...])
    for hd in range(MEM_HEADS):
        kh = kv[:, hd * LANES:(hd + 1) * LANES]
        kms = jnp.mean(kh * kh, axis=-1, keepdims=True)
        k_ref[:, hd * LANES:(hd + 1) * LANES] = ((kh * lax.rsqrt(kms + EPS)) * kg_ref[...]).astype(BF16)
    v_ref[...] = kv[:, MEM_Q:].astype(BF16)


def _mem_kv(mem2, g, w_kv, kg):
    n = mem2.shape[0]
    return pl.pallas_call(
        _mem_kv_kernel,
        grid=(n // N_MEM,),
        in_specs=[pl.BlockSpec((N_MEM, D_MODEL), lambda i: (i, 0)),
                  pl.BlockSpec((1, D_MODEL), lambda i: (0, 0)),
                  pl.BlockSpec((D_MODEL, 2 * MEM_Q), lambda i: (0, 0)),
                  pl.BlockSpec((1, MEM_HEAD_DIM), lambda i: (0, 0))],
        out_specs=[pl.BlockSpec((N_MEM, MEM_Q), lambda i: (i, 0)), pl.BlockSpec((N_MEM, MEM_Q), lambda i: (i, 0))],
        out_shape=[jax.ShapeDtypeStruct((n, MEM_Q), BF16), jax.ShapeDtypeStruct((n, MEM_Q), BF16)],
        compiler_params=_cparams(("parallel",)),
        name="mem_kv",
    )(mem2, g, w_kv, kg)


def _mem_attend(q_ref, k_ref, v_ref, qg_ref):
    scale = MEM_HEAD_DIM ** -0.5 * LOG2E
    ones = jnp.ones((N_MEM, LANES), BF16)
    heads = []
    for hd in range(MEM_HEADS):
        sl = slice(hd * LANES, (hd + 1) * LANES)
        q = q_ref[:, sl].astype(F32)
        qms = jnp.mean(q * q, axis=-1, keepdims=True)
        qn = ((q * lax.rsqrt(qms + EPS)) * qg_ref[...] * scale).astype(BF16)
        s = _dot_nt(qn, k_ref[0, :, sl])
        p = jnp.exp2(s - jnp.max(s, axis=-1, keepdims=True))
        r = _dot(p.astype(BF16), jnp.concatenate([v_ref[0, :, sl], ones], axis=1))
        heads.append((r[:, :LANES] * (1.0 / r[:, LANES:])).astype(BF16))
    return jnp.concatenate(heads, axis=1)


def _merge_kernel(x_ref, ya_ref, of_ref, ob_ref, z_ref, mq_ref, mk_ref, mv_ref, mqg_ref, ng_ref, wg_ref, og_ref,
                  wa_ref, wd_ref, wm_ref, wo_ref, o_ref):
    y_mem = _mem_attend(mq_ref, mk_ref, mv_ref, mqg_ref)
    x = x_ref[...]
    ms = jnp.mean(x * x, axis=-1, keepdims=True)
    h = ((x * lax.rsqrt(ms + EPS)) * ng_ref[...]).astype(BF16)
    o = of_ref[...].astype(F32) + ob_ref[...].astype(F32)
    z = z_ref[...].astype(F32)
    og = og_ref[...]
    parts = []
    for pc in range(DN_V // LANES):
        os_ = o[:, pc * LANES:(pc + 1) * LANES]
        on = os_ * lax.rsqrt(_half_sums(os_ * os_) * (1.0 / DN_VALUE_DIM) + EPS) * og
        parts.append((on * _silu(z[:, pc * LANES:(pc + 1) * LANES])).astype(BF16))
    y_dn = jnp.concatenate(parts, axis=1)
    def gate(b):
        return _sigmoid(_dot(h, wg_ref[:, b * D_MODEL:(b + 1) * D_MODEL]))

    merged = (gate(0) * _dot(ya_ref[...], wa_ref[...])
              + gate(1) * _dot(y_dn, wd_ref[...])
              + gate(2) * _dot(y_mem, wm_ref[...]))
    o_ref[...] = x + _dot(merged.astype(BF16), wo_ref[...])


def _merge(x2, ya, of, ob, z, mq, mk, mv, mqg, ng, wg, og, wa, wd, wm, wo):
    n = x2.shape[0]
    tm = ROW_TILE
    tiles_per_seq = n // mk.shape[0] // tm
    row = lambda w_: pl.BlockSpec((tm, w_), lambda i: (i, 0))
    mem = pl.BlockSpec((1, N_MEM, MEM_Q), lambda i: (i // tiles_per_seq, 0, 0))
    full = lambda a, b: _resident((a, b))
    return pl.pallas_call(
        _merge_kernel,
        grid=(n // tm,),
        in_specs=[row(D_MODEL), row(ATTN_Q), row(DN_V), row(DN_V), row(DN_V), row(MEM_Q), mem, mem,
                  full(1, MEM_HEAD_DIM), full(1, D_MODEL), full(D_MODEL, N_BRANCH * D_MODEL),
                  full(1, LANES), full(ATTN_Q, D_MODEL), full(DN_V, D_MODEL), full(MEM_Q, D_MODEL),
                  full(D_MODEL, D_MODEL)],
        out_specs=row(D_MODEL),
        out_shape=jax.ShapeDtypeStruct((n, D_MODEL), F32),
        compiler_params=_cparams(("parallel",)),
        name="merge",
    )(x2, ya, of, ob, z, mq, mk, mv, mqg, ng, wg, og, wa, wd, wm, wo)


def _ffn_kernel(xp_ref, xc_ref, xn_ref, g_ref, wu_ref, cw_ref, cb_ref, wd_ref, o_ref, act_ref):
    i = pl.program_id(1)
    nt = pl.num_programs(1)
    tm = xc_ref.shape[1]
    xc = xc_ref[0]
    prev = jnp.where(i == 0, 0.0, xp_ref[0])
    nxt = jnp.where(i == nt - 1, 0.0, xn_ref[0])
    xe = jnp.concatenate([prev, xc, nxt], axis=0)
    ms = jnp.mean(xe * xe, axis=-1, keepdims=True)
    h = ((xe * lax.rsqrt(ms + EPS)) * g_ref[...]).astype(BF16)

    def both(ref, rows_, c0):
        return jnp.concatenate([ref[rows_, c0:c0 + FF_CHUNK], ref[rows_, D_FF + c0:D_FF + c0 + FF_CHUNK]], axis=1)

    for c in range(D_FF // FF_CHUNK):
        c0 = c * FF_CHUNK
        u = _dot(h, both(wu_ref, slice(None), c0))
        y = None
        for j in range(FFN_CONV):
            term = _shift_rows(u, FFN_CONV // 2 - j)[HALO:HALO + tm] * both(cw_ref, slice(j, j + 1), c0)
            y = term if y is None else y + term
        y = y + both(cb_ref, slice(None), c0)
        act_ref[:, c0:c0 + FF_CHUNK] = (_silu(y[:, :FF_CHUNK]) * y[:, FF_CHUNK:]).astype(BF16)
    o_ref[0] = xc + _dot(act_ref[...], wd_ref[...])


def _ffn(x1, g, wu, cw, cb, wd):
    B, S, _ = x1.shape
    tm = ROW_TILE
    nt = S // tm
    hb = tm // HALO
    return pl.pallas_call(
        _ffn_kernel,
        grid=(B, nt),
        in_specs=[pl.BlockSpec((1, HALO, D_MODEL), lambda b, i: (b, jnp.maximum(i * hb - 1, 0), 0)),
                  pl.BlockSpec((1, tm, D_MODEL), lambda b, i: (b, i, 0)),
                  pl.BlockSpec((1, HALO, D_MODEL), lambda b, i: (b, jnp.minimum((i + 1) * hb, S // HALO - 1), 0)),
                  _resident((1, D_MODEL)), _resident((D_MODEL, 2 * D_FF)), _resident((FFN_CONV, 2 * D_FF)),
                  _resident((1, 2 * D_FF)), _resident((D_FF, D_MODEL))],
        out_specs=pl.BlockSpec((1, tm, D_MODEL), lambda b, i: (b, i, 0)),
        out_shape=jax.ShapeDtypeStruct((B, S, D_MODEL), F32),
        scratch_shapes=[pltpu.VMEM((tm, D_FF), BF16)],
        compiler_params=_cparams(("parallel", "parallel")),
        name="ffn",
    )(x1, x1, x1, g, wu, cw, cb, wd)


def _permute_w_in(w):
    idx = np.cumsum((0,) + IN_SPLITS)
    wb = w.astype(BF16)
    return wb[:, :idx[7]], wb[:, idx[9]:idx[10]], wb[:, idx[7]:idx[9]], wb[:, idx[10]:]


def _layer(x, mem, rel_bias_table, p):
    B, S, D = x.shape
    n = B * S
    x2 = x.reshape(n, D)
    row = lambda a: a.reshape(1, -1).astype(F32)
    tile2 = lambda a: jnp.tile(a.astype(F32), 2)[None]
    w_main, w_mq, w_ba, w_gate = _permute_w_in(p["w_in"])
    aq, akv, dz, mq, qkvn, betax, gcx, gct = _inproj(
        x, row(p["norm_mix_g"]), w_main, w_mq, w_ba, tile2(p["attn_q_norm_g"]), tile2(p["attn_k_norm_g"]),
        p["dn_conv_w"], p["dn_a_log"], p["dn_dt_bias"])
    y_attn = _attn(aq, akv, rel_bias_table, p["attn_sink"])
    o_f, o_b = _dn_scan(*_dn_chunk(qkvn, betax, gcx, gct))
    mk, mv = _mem_kv(mem.reshape(B * N_MEM, D), row(p["mem_norm_g"]), p["mem_w_kv"].astype(BF16),
                     row(p["mem_k_norm_g"]))
    og2 = jnp.tile(p["dn_out_norm_g"].astype(F32), 2)[None]
    x1 = _merge(x2, y_attn.reshape(n, -1), o_f.reshape(n, -1), o_b.reshape(n, -1), dz.reshape(n, -1),
                mq.reshape(n, -1), mk.reshape(B, N_MEM, -1), mv.reshape(B, N_MEM, -1), row(p["mem_q_norm_g"]),
                row(p["norm_mix_g"]), w_gate, og2, p["w_br_attn"].astype(BF16), p["w_br_dn"].astype(BF16),
                p["w_br_mem"].astype(BF16), p["w_out"].astype(BF16))
    return _ffn(x1.reshape(B, S, D), row(p["norm_ffn_g"]), p["ffn_w_up"].astype(BF16), p["ffn_conv_w"].astype(F32),
                row(p["ffn_conv_b"]), p["ffn_w_down"].astype(BF16))


_LAYER_PARAMS = ("norm_mix_g", "w_in", "attn_q_norm_g", "attn_k_norm_g", "attn_sink", "dn_conv_w", "dn_a_log",
                 "dn_dt_bias", "dn_out_norm_g", "mem_norm_g", "mem_w_kv", "mem_q_norm_g", "mem_k_norm_g",
                 "w_br_attn", "w_br_dn", "w_br_mem", "w_out", "norm_ffn_g", "ffn_w_up", "ffn_conv_w", "ffn_conv_b",
                 "ffn_w_down")


def kernel(x, mem, rel_bias_table, norm_mix_g, w_in, attn_q_norm_g, attn_k_norm_g, attn_sink, dn_conv_w, dn_a_log,
           dn_dt_bias, dn_out_norm_g, mem_norm_g, mem_w_kv, mem_q_norm_g, mem_k_norm_g, w_br_attn, w_br_dn,
           w_br_mem, w_out, norm_ffn_g, ffn_w_up, ffn_conv_w, ffn_conv_b, ffn_w_down):
    stacked = dict(zip(_LAYER_PARAMS, (norm_mix_g, w_in, attn_q_norm_g, attn_k_norm_g, attn_sink, dn_conv_w,
                                       dn_a_log, dn_dt_bias, dn_out_norm_g, mem_norm_g, mem_w_kv, mem_q_norm_g,
                                       mem_k_norm_g, w_br_attn, w_br_dn, w_br_mem, w_out, norm_ffn_g, ffn_w_up,
                                       ffn_conv_w, ffn_conv_b, ffn_w_down)))
    depth = w_in.shape[0]
    for l in range(depth):
        x = _layer(x, mem, rel_bias_table, {k: v[l] for k, v in stacked.items()})
    return x
```

```python
import math

import numpy as np
import jax
import jax.numpy as jnp
from jax import lax
from jax.experimental import pallas as pl
from jax.experimental.pallas import tpu as pltpu

F32 = jnp.float32
BF16 = jnp.bfloat16

EPS = 1e-6
D_MODEL = 1024
N_MEM = 256
ATTN_HEADS = 8
ATTN_KV_HEADS = 2
ATTN_HEAD_DIM = 64
WINDOW = 128
ATTN_BLOCK = 128
REL_BUCKETS = 32
REL_MAX_DIST = 128
DN_HEADS = 8
DN_KEY_DIM = 64
DN_VALUE_DIM = 64
DN_CONV = 5
DN_CHUNK = 64
MEM_HEADS = 4
MEM_HEAD_DIM = 128
D_FF = 2816
FFN_CONV = 3
N_BRANCH = 3

ATTN_Q = ATTN_HEADS * ATTN_HEAD_DIM
ATTN_KV = ATTN_KV_HEADS * ATTN_HEAD_DIM
DN_QK = DN_HEADS * DN_KEY_DIM
DN_V = DN_HEADS * DN_VALUE_DIM
MEM_Q = MEM_HEADS * MEM_HEAD_DIM
IN_SPLITS = (ATTN_Q, ATTN_KV, ATTN_KV, DN_QK, DN_QK, DN_V, DN_V, 2 * DN_HEADS, 2 * DN_HEADS, MEM_Q,
             N_BRANCH * D_MODEL)

LANES = 128
SUBLANES = 8
HALF = 64
HALO = SUBLANES
NEG = -1e30
VMEM_LIMIT = 56 * 1024 * 1024
LOG2E = math.log2(math.e)

ROW_TILE = 1024
FF_CHUNK = 256


def _cparams(sem):
    return pltpu.CompilerParams(dimension_semantics=sem, vmem_limit_bytes=VMEM_LIMIT)


def _resident(shape):
    zeros = (0,) * len(shape)
    return pl.BlockSpec(shape, lambda *_: zeros, pipeline_mode=pl.Buffered(1))


def _dot(a, b):
    return jnp.dot(a, b, preferred_element_type=F32)


def _dot_nt(a, b):
    return lax.dot_general(a, b, (((1,), (1,)), ((), ())), preferred_element_type=F32)


def _lane_is_low(shape):
    lane = lax.broadcasted_iota(jnp.int32, shape, len(shape) - 1)
    return (lane % LANES) < HALF


def _half_sums(sq):
    low = _lane_is_low(sq.shape)
    s_lo = jnp.sum(jnp.where(low, sq, 0.0), axis=-1, keepdims=True)
    s_hi = jnp.sum(jnp.where(low, 0.0, sq), axis=-1, keepdims=True)
    return jnp.where(low, s_lo, s_hi)


def _shift_rows(x, s):
    if s == 0:
        return x
    rr, cc = x.shape
    x3 = x.reshape(rr // SUBLANES, SUBLANES, cc)
    rot = pltpu.roll(x3, s % SUBLANES, axis=1)
    sub = lax.broadcasted_iota(jnp.int32, x3.shape, 1)
    if s > 0:
        other = jnp.concatenate([rot[-1:], rot[:-1]], axis=0)
        y3 = jnp.where(sub < s, other, rot)
    else:
        other = jnp.concatenate([rot[1:], rot[:1]], axis=0)
        y3 = jnp.where(sub >= SUBLANES + s, other, rot)
    return y3.reshape(rr, cc)


def _silu(x):
    h = 0.5 * x
    return h * jnp.tanh(h) + h


def _sigmoid(x):
    return 0.5 * jnp.tanh(0.5 * x) + 0.5


_C_AQ = (0, 512)
_C_AKV = (512, 768)
_C_DQKV = (768, 2304)
_C_DZ = (2304, 2816)
_N_IN = 2816


def _head_rmsnorm(t, gain2):
    return t * lax.rsqrt(_half_sums(t * t) * (1.0 / ATTN_HEAD_DIM) + EPS) * gain2


def _split_hi_lo(x):
    hi = x.astype(BF16)
    lo = (x - hi.astype(F32)).astype(BF16)
    return hi, lo


def _inproj_kernel(xp_ref, xc_ref, xn_ref, g_ref, w_ref, wmq_ref, wba_ref, qg_ref, kg_ref, cw_ref, alog_ref,
                   dtb_ref, tri_ref, expand_ref, aq_ref, akv_ref, dz_ref, mq_ref, qkv_ref, betax_ref, gcx_ref, gct_ref):
    i = pl.program_id(1)
    nt = pl.num_programs(1)
    tm = xc_ref.shape[1]
    prev = jnp.where(i == 0, 0.0, xp_ref[0])
    nxt = jnp.where(i == nt - 1, 0.0, xn_ref[0])
    xe = jnp.concatenate([prev, xc_ref[0], nxt], axis=0)
    ms = jnp.mean(xe * xe, axis=-1, keepdims=True)
    he = ((xe * lax.rsqrt(ms + EPS)) * g_ref[...]).astype(BF16)
    h = he[HALO:HALO + tm]

    def proj(c):
        return _dot(h, w_ref[:, c[0]:c[1]])

    cw_chunk = 2 * LANES

    def dn_qkv_chunk(c0):
        ue = _dot(he, w_ref[:, _C_DQKV[0] + c0:_C_DQKV[0] + c0 + cw_chunk])
        acc = None
        for j in range(DN_CONV):
            term = _shift_rows(ue, DN_CONV // 2 - j)[HALO:HALO + tm] * cw_ref[j:j + 1, c0:c0 + cw_chunk]
            acc = term if acc is None else acc + term
        y = _silu(acc)
        for l0 in range(0, cw_chunk, LANES):
            ys = y[:, l0:l0 + LANES]
            if c0 < 2 * DN_QK:
                ys = ys * lax.rsqrt(_half_sums(ys * ys) + EPS)
            if c0 < DN_QK:
                ys = ys * (DN_KEY_DIM ** -0.5)
            qkv_ref[0, :, c0 + l0:c0 + l0 + LANES] = ys.astype(BF16)

    def dn_beta_decay():
        ba = _dot(h, wba_ref[...])
        nh2 = 2 * DN_HEADS
        beta = _sigmoid(ba[:, :nh2])
        z = ba[:, nh2:2 * nh2] + dtb_ref[...]
        sp = jnp.maximum(z, 0.0) + jnp.log1p(jnp.exp(-jnp.abs(z)))
        g = -jnp.exp(alog_ref[...]) * sp
        g_hi, g_lo = _split_hi_lo(g)
        g_lo2 = (g - g_hi.astype(F32) - g_lo.astype(F32)).astype(BF16)
        gg = jnp.concatenate([g_hi, g_lo, g_lo2], axis=1)
        tb = tri_ref.shape[1]
        pre = jnp.concatenate([_dot(tri_ref[0], gg[r0:r0 + tb]) for r0 in range(0, tm, tb)], axis=0)
        suf = jnp.concatenate([_dot(tri_ref[1], gg[r0:r0 + tb]) for r0 in range(0, tm, tb)], axis=0)
        lane16 = lax.broadcasted_iota(jnp.int32, (tm, nh2), 1)
        gc = jnp.where(lane16 < DN_HEADS, pre[:, :nh2] + pre[:, nh2:2 * nh2] + pre[:, 2 * nh2:],
                       suf[:, :nh2] + suf[:, nh2:2 * nh2] + suf[:, 2 * nh2:])
        b_hi, b_lo = _split_hi_lo(beta)
        c_hi, c_lo = _split_hi_lo(gc)
        c_lo2 = (gc - c_hi.astype(F32) - c_lo.astype(F32)).astype(BF16)
        bx = _dot(jnp.concatenate([b_hi, b_lo, c_hi, c_lo, c_lo2], axis=1), expand_ref[...])
        betax_ref[0] = bx[:, :nh2 * HALF]
        gcx_ref[0] = bx[:, nh2 * HALF:]
        gct = jnp.concatenate([gc, jnp.zeros((tm, LANES - nh2), F32)], axis=1).T
        for c in range(tm // DN_CHUNK):
            gct_ref[0, c] = gct[:nh2, c * DN_CHUNK:(c + 1) * DN_CHUNK]

    def attn_q():
        aq = proj(_C_AQ)
        q_scale = ATTN_HEAD_DIM ** -0.5 * LOG2E
        for pc in range(ATTN_Q // LANES):
            sl = slice(pc * LANES, (pc + 1) * LANES)
            aq_ref[0, :, sl] = (_head_rmsnorm(aq[:, sl], qg_ref[...]) * q_scale).astype(BF16)

    def attn_kv():
        akv = proj(_C_AKV)
        kn = _head_rmsnorm(akv[:, :LANES], kg_ref[...])
        av = akv[:, LANES:]
        akv_ref[0, :, 0 * LANES:1 * LANES] = kn.astype(BF16)
        akv_ref[0, :, 1 * LANES:2 * LANES] = av.astype(BF16)
        akv_ref[0, :, 2 * LANES:3 * LANES] = pltpu.roll(kn, HALF, axis=1).astype(BF16)
        akv_ref[0, :, 3 * LANES:4 * LANES] = pltpu.roll(av, HALF, axis=1).astype(BF16)

    def dn_z():
        dz_ref[0] = proj(_C_DZ).astype(BF16)

    def mem_q():
        mq_ref[0] = _dot(h, wmq_ref[...]).astype(BF16)

    for c0 in range(0, 2 * DN_QK + DN_V, cw_chunk):
        dn_qkv_chunk(c0)
    for part in (dn_beta_decay, attn_q, attn_kv, dn_z, mem_q):
        part()


def _inproj(x, g, w, w_mq, w_ba, qg2, kg2, conv_w, a_log, dt_bias):
    B, S, D = x.shape
    tm = ROW_TILE
    nt = S // tm
    hb = tm // HALO
    nh2 = 2 * DN_HEADS
    tb = 2 * LANES
    r = np.arange(tb)
    same = (r[:, None] // DN_CHUNK) == (r[None, :] // DN_CHUNK)
    tri = np.stack([same & (r[:, None] >= r[None, :]), same & (r[:, None] <= r[None, :])]).astype(np.float32)
    rep = np.repeat(np.eye(nh2, dtype=np.float32), HALF, axis=1)
    zero = np.zeros_like(rep)
    expand = np.block([[rep, zero]] * 2 + [[zero, rep]] * 3)
    blk = lambda w_: pl.BlockSpec((1, tm, w_), lambda b, i: (b, i, 0))
    outs = [(ATTN_Q, BF16), (4 * LANES, BF16), (DN_V, BF16), (MEM_Q, BF16), (2 * DN_QK + DN_V, BF16),
            (nh2 * HALF, F32), (nh2 * HALF, F32)]
    return pl.pallas_call(
        _inproj_kernel,
        grid=(B, nt),
        in_specs=[pl.BlockSpec((1, HALO, D), lambda b, i: (b, jnp.maximum(i * hb - 1, 0), 0)),
                  blk(D),
                  pl.BlockSpec((1, HALO, D), lambda b, i: (b, jnp.minimum((i + 1) * hb, S // HALO - 1), 0)),
                  _resident((1, D)), _resident((D, _N_IN)), _resident((D, MEM_Q)), _resident((D, 2 * nh2)),
                  _resident((1, LANES)), _resident((1, LANES)),
                  _resident((DN_CONV, 2 * DN_QK + DN_V)), _resident((1, nh2)), _resident((1, nh2)),
                  _resident((2, tb, tb)), _resident((5 * nh2, 2 * nh2 * HALF))],
        out_specs=[blk(w_) for w_, _ in outs]
        + [pl.BlockSpec((1, tm // DN_CHUNK, nh2, DN_CHUNK), lambda b, i: (b, i, 0, 0))],
        out_shape=[jax.ShapeDtypeStruct((B, S, w_), dt) for w_, dt in outs]
        + [jax.ShapeDtypeStruct((B, S // DN_CHUNK, nh2, DN_CHUNK), F32)],
        compiler_params=_cparams(("parallel", "parallel")),
        name="inproj",
    )(x, x, x, g, w, w_mq, w_ba, qg2, kg2, conv_w.astype(F32), a_log.reshape(1, nh2).astype(F32),
      dt_bias.reshape(1, nh2).astype(F32), jnp.asarray(tri, BF16), jnp.asarray(expand, BF16))


def _t5_buckets(rel):
    nb = REL_BUCKETS // 2
    max_exact = nb // 2
    ret = (rel > 0).astype(np.int32) * nb
    n = np.abs(rel)
    large = max_exact + (np.log(np.maximum(n, 1) / max_exact) / np.log(REL_MAX_DIST / max_exact)
                         * (nb - max_exact)).astype(np.int32)
    large = np.minimum(large, nb - 1)
    return (ret + np.where(n < max_exact, n, large)).astype(np.int32)


_ATTN_GROUPS = ((0, 1, True, False), (0, 1, False, True), (2, 3, True, True), (2, 3, False, False))
_ATTN_GROUP_HEADS = ((0, 2), (1, 3), (4, 6), (5, 7))
ATTN_BLOCKS_PER_STEP = 8
ATTN_WAVE_BLOCKS = 4


def _attn_kernel(q_ref, kp_ref, kc_ref, kn_ref, bias_ref, sink_ref, o_ref):
    T = ATTN_BLOCK
    nq = ATTN_BLOCKS_PER_STEP
    first_blk = pl.program_id(1) * nq
    last_blk = pl.num_programs(1) * nq - 1
    kv_all = jnp.concatenate([kp_ref[0], kc_ref[0], kn_ref[0]], axis=0)
    ones = jnp.ones((3 * T, LANES), BF16)
    low_q = _lane_is_low((T, LANES))
    zero = jnp.zeros((T, LANES), BF16)
    groups = range(len(_ATTN_GROUPS))
    kvs, scores, maxes, res = {}, {}, {}, {}

    def score_stage(blocks):
        for qb in blocks:
            kv = kv_all[qb * T:(qb + 3) * T]
            kvs[qb, False] = (kv[:, 0 * LANES:1 * LANES],
                              jnp.concatenate([kv[:, 1 * LANES:2 * LANES], ones], axis=1))
            kvs[qb, True] = (kv[:, 2 * LANES:3 * LANES],
                             jnp.concatenate([kv[:, 3 * LANES:4 * LANES], ones], axis=1))
            rq = slice(qb * T, (qb + 1) * T)
            blk = first_blk + qb
            edge = jnp.where(blk == 0, 0, jnp.where(blk == last_blk, 2, 1))
            for gi in groups:
                pa, pb, low, swapped = _ATTN_GROUPS[gi]
                sel = low_q if low else jnp.logical_not(low_q)
                lhs = jnp.concatenate([jnp.where(sel, q_ref[0, rq, pa * LANES:(pa + 1) * LANES], zero),
                                       jnp.where(sel, q_ref[0, rq, pb * LANES:(pb + 1) * LANES], zero)], axis=0)
                scores[qb, gi] = _dot_nt(lhs, kvs[qb, swapped][0]) + bias_ref[edge, gi]

    def softmax_stages(blocks):
        probs = [(qb, gi) for qb in blocks for gi in groups]
        for qb, gi in probs:
            s = scores.pop((qb, gi))
            m = jnp.maximum(jnp.max(s, axis=-1, keepdims=True), sink_ref[gi])
            maxes[qb, gi] = m
            scores[qb, gi] = jnp.exp2(s - m).astype(BF16)
        for qb, gi in probs:
            res[qb, gi] = _dot(scores.pop((qb, gi)), kvs[qb, _ATTN_GROUPS[gi][3]][1])
        for qb, gi in probs:
            r = res[qb, gi]
            den = r[:, LANES:] + jnp.exp2(sink_ref[gi] - maxes[qb, gi])
            res[qb, gi] = r[:, :LANES] * (1.0 / den)
        for qb in blocks:
            rq = slice(qb * T, (qb + 1) * T)
            for pc, (ge, go) in enumerate(((0, 1), (0, 1), (2, 3), (2, 3))):
                r0 = (pc % 2) * T
                out = jnp.where(low_q, res[qb, ge][r0:r0 + T], res[qb, go][r0:r0 + T])
                o_ref[0, rq, pc * LANES:(pc + 1) * LANES] = out.astype(BF16)

    waves = [range(w0, min(w0 + ATTN_WAVE_BLOCKS, nq)) for w0 in range(0, nq, ATTN_WAVE_BLOCKS)]
    score_stage(waves[0])
    for w, blocks in enumerate(waves):
        if w + 1 < len(waves):
            score_stage(waves[w + 1])
        softmax_stages(blocks)


def _attn(aq, akv, rel_table, sink):
    B, S, _ = aq.shape
    T = ATTN_BLOCK
    nb = S // T
    assert nb >= 2
    t_idx = np.arange(T)[:, None]
    j_idx = np.arange(3 * T)[None, :]
    rel = j_idx - T - t_idx
    onehot = jnp.asarray(np.eye(REL_BUCKETS, dtype=np.float32)[_t5_buckets(rel)])
    bias = jnp.einsum("tjr,rh->htj", onehot, rel_table.astype(F32), precision=lax.Precision.HIGHEST) * LOG2E
    in_win = np.abs(rel) <= WINDOW
    edge_ok = np.stack([in_win & (j_idx >= T), in_win, in_win & (j_idx < 2 * T)])
    bias = jnp.where(jnp.asarray(edge_ok)[:, None], bias[None], NEG)
    bias_g = jnp.stack([jnp.concatenate([bias[:, a], bias[:, b]], axis=1) for a, b in _ATTN_GROUP_HEADS], axis=1)
    sk = sink.astype(F32) * LOG2E
    sink_g = jnp.stack([jnp.concatenate([jnp.full((T, 1), 1.0) * sk[a], jnp.full((T, 1), 1.0) * sk[b]], axis=0)
                        for a, b in _ATTN_GROUP_HEADS])
    kv_w = akv.shape[-1]
    nq = ATTN_BLOCKS_PER_STEP
    assert nb % nq == 0
    return pl.pallas_call(
        _attn_kernel,
        grid=(B, nb // nq),
        in_specs=[pl.BlockSpec((1, nq * T, ATTN_Q), lambda b, i: (b, i, 0)),
                  pl.BlockSpec((1, T, kv_w), lambda b, i: (b, jnp.maximum(i * nq - 1, 0), 0)),
                  pl.BlockSpec((1, nq * T, kv_w), lambda b, i: (b, i, 0)),
                  pl.BlockSpec((1, T, kv_w), lambda b, i: (b, jnp.minimum((i + 1) * nq, nb - 1), 0)),
                  _resident((3, 4, 2 * T, 3 * T)), _resident((4, 2 * T, 1))],
        out_specs=pl.BlockSpec((1, nq * T, ATTN_Q), lambda b, i: (b, i, 0)),
        out_shape=jax.ShapeDtypeStruct((B, S, ATTN_Q), BF16),
        compiler_params=_cparams(("parallel", "parallel")),
        name="attn",
    )(aq, akv, akv, akv, bias_g, sink_g)


DN_TILE_HEADS = 2
DN_TILE = DN_TILE_HEADS * HALF
DN_TILES = DN_HEADS // DN_TILE_HEADS
DN_PAIRS = DN_HEADS // 2
DN_CHUNKS_PER_STEP = 8
DN_WAVE_CHUNKS = 2
DN_SCAN_CHUNKS = 16


def _head_of_lane(shape):
    return lax.broadcasted_iota(jnp.int32, shape, len(shape) - 1) // HALF


def _block_diag(x, head):
    zero = jnp.zeros_like(x)
    return jnp.concatenate([jnp.where(head == h, x, zero) for h in range(DN_TILE_HEADS)], axis=0)


_PK_U, _PK_W, _PK_QKD, _PK_QIN = range(4)


def _packed_cols(kind, tl):
    c0 = kind * DN_QK + tl * DN_TILE
    return slice(c0, c0 + DN_TILE)


def _dn_chunk_kernel(qkv_ref, bx_ref, gx_ref, gt_ref, pk_ref, kupt_ref, egt_ref):
    C = DN_CHUNK
    TW = DN_TILE
    head = _head_of_lane((C, TW))
    r = lax.broadcasted_iota(jnp.int32, (C, TW), 0)
    m = lax.broadcasted_iota(jnp.int32, (C, TW), 1) % HALF
    eye = jnp.where(r == m, 1.0, 0.0)
    dir_w = len((_PK_U, _PK_W, _PK_QKD, _PK_QIN)) * DN_QK

    def put(kind, d, rows, tl, val):
        c = _packed_cols(kind, tl)
        pk_ref[0, rows, d * dir_w + c.start:d * dir_w + c.stop] = val.astype(BF16)

    def prepare(ch):
        probs = []
        rows = slice(ch * C, (ch + 1) * C)
        for tl in range(DN_TILES):
            q4 = qkv_ref[0, rows, tl * TW:(tl + 1) * TW]
            k4 = qkv_ref[0, rows, DN_QK + tl * TW:DN_QK + (tl + 1) * TW]
            v4 = qkv_ref[0, rows, 2 * DN_QK + tl * TW:2 * DN_QK + (tl + 1) * TW]
            q4f, k4f, v4f = q4.astype(F32), k4.astype(F32), v4.astype(F32)
            qkk = _dot_nt(jnp.concatenate([q4, k4], axis=0), _block_diag(k4, head))
            qk, kk = qkk[:C], qkk[C:]
            for d in range(2):
                col = slice(d * DN_QK + tl * TW, d * DN_QK + (tl + 1) * TW)
                beta4 = bx_ref[0, rows, col]
                gcol4 = gx_ref[0, rows, col]
                h0 = d * DN_HEADS + tl * DN_TILE_HEADS
                grow4 = jnp.concatenate([gt_ref[0, ch, h0 + h:h0 + h + 1, :] for h in range(DN_TILE_HEADS)],
                                        axis=1)
                incl = (r >= m) if d == 0 else (r <= m)
                strict = (r > m) if d == 0 else (r < m)
                decay = jnp.where(incl, jnp.exp(jnp.where(incl, gcol4 - grow4, 0.0)), 0.0)
                gtot4 = gcol4[C - 1:C] if d == 0 else gcol4[0:1]
                e_col = jnp.exp(gcol4)
                p = jnp.where(strict, -(kk * beta4 * decay), 0.0)
                vb = (v4f * beta4).astype(BF16)
                kbe = (k4f * beta4 * e_col).astype(BF16)
                rhs = jnp.concatenate([_block_diag(vb, head), _block_diag(kbe, head)], axis=1)
                put(_PK_QKD, d, rows, tl, qk * decay)
                put(_PK_QIN, d, rows, tl, q4f * e_col)
                kupt_ref[0, ch, d, tl] = (k4f * jnp.exp(gtot4 - gcol4)).T.astype(BF16)
                egt_ref[0, ch, d, tl:tl + 1, :] = jnp.exp(gtot4)
                probs.append([p, eye + p, rhs, rows, d, tl])
        return probs

    def solve(probs):
        for pb in probs:
            pb[0] = _dot(pb[0].astype(BF16), _block_diag(pb[0].astype(BF16), head))
        for _ in range(4):
            for pb in probs:
                both = _dot(jnp.concatenate([pb[1], pb[0]], axis=0).astype(BF16),
                            _block_diag(pb[0].astype(BF16), head))
                pb[1] = pb[1] + both[:C]
                pb[0] = both[C:]
        for p, t, rhs, rows, d, tl in probs:
            t = t + _dot(t.astype(BF16), _block_diag(p.astype(BF16), head))
            sol = _dot(t.astype(BF16), rhs)
            put(_PK_U, d, rows, tl, sol[:, :TW])
            put(_PK_W, d, rows, tl, sol[:, TW:])

    waves = [range(c0, min(c0 + DN_WAVE_CHUNKS, DN_CHUNKS_PER_STEP))
             for c0 in range(0, DN_CHUNKS_PER_STEP, DN_WAVE_CHUNKS)]
    ready = [pb for ch in waves[0] for pb in prepare(ch)]
    for w in range(len(waves)):
        upcoming = [pb for ch in waves[w + 1] for pb in prepare(ch)] if w + 1 < len(waves) else []
        solve(ready)
        ready = upcoming


def _dn_chunk(qkvn, betax, gcx, gct):
    B, S, W = qkvn.shape
    C = DN_CHUNK
    cps = DN_CHUNKS_PER_STEP
    nc = S // C
    wide = 2 * DN_QK
    blk = lambda w_: pl.BlockSpec((1, cps * C, w_), lambda b, i: (b, i, 0))
    return pl.pallas_call(
        _dn_chunk_kernel,
        grid=(B, nc // cps),
        in_specs=[blk(W), blk(wide), blk(wide),
                  pl.BlockSpec((1, cps, 2 * DN_HEADS, C), lambda b, i: (b, i, 0, 0))],
        out_specs=[blk(4 * wide),
                   pl.BlockSpec((1, cps, 2, DN_TILES, DN_TILE, C), lambda b, i: (b, i, 0, 0, 0, 0)),
                   pl.BlockSpec((1, cps, 2, DN_TILES, DN_TILE), lambda b, i: (b, i, 0, 0, 0))],
        out_shape=[jax.ShapeDtypeStruct((B, S, 4 * wide), BF16),
                   jax.ShapeDtypeStruct((B, nc, 2, DN_TILES, DN_TILE, C), BF16),
                   jax.ShapeDtypeStruct((B, nc, 2, DN_TILES, DN_TILE), F32)],
        compiler_params=_cparams(("parallel", "parallel")),
        name="dn_chunk",
    )(qkvn, betax, gcx, gct)


def _dn_scan_kernel(pkf, kuptf, egtf, pkb, kuptb, egtb, of_ref, ob_ref, st_ref):
    @pl.when(pl.program_id(1) == 0)
    def _():
        st_ref[...] = jnp.zeros_like(st_ref)

    C = DN_CHUNK
    TW = DN_TILE
    head = _head_of_lane((C, TW))
    rr = lax.broadcasted_iota(jnp.int32, (LANES, LANES), 0) < HALF
    cc = lax.broadcasted_iota(jnp.int32, (LANES, LANES), 1) < HALF
    on_diag = rr == cc
    zpad = jnp.zeros((LANES, LANES), BF16)
    ppt = TW // LANES
    dirs = ((pkf, kuptf, egtf, of_ref), (pkb, kuptb, egtb, ob_ref))
    tile_ids = [(d, tl) for d in range(2) for tl in range(DN_TILES)]
    state = {(d, pr): st_ref[d, pr] for d in range(2) for pr in range(DN_PAIRS)}
    for step in range(DN_SCAN_CHUNKS):
        ws, vb = {}, {}
        for d, tl in tile_ids:
            pk_ref, kupt_ref, egt_ref, o_ref = dirs[d]
            ch = step if d == 0 else DN_SCAN_CHUNKS - 1 - step
            rows = slice(ch * C, (ch + 1) * C)
            lhs = jnp.concatenate([pk_ref[0, rows, _packed_cols(_PK_W, tl)],
                                   pk_ref[0, rows, _packed_cols(_PK_QIN, tl)]], axis=0)
            blocks = [state[d, ppt * tl + i].astype(BF16) for i in range(ppt)]
            s_tile = jnp.concatenate(
                [jnp.concatenate([blocks[i] if j == i else zpad for j in range(ppt)], axis=1) for i in range(ppt)],
                axis=0) if ppt > 1 else blocks[0]
            ws[d, tl] = _dot(lhs, s_tile)
        for d, tl in tile_ids:
            pk_ref = dirs[d][0]
            ch = step if d == 0 else DN_SCAN_CHUNKS - 1 - step
            rows = slice(ch * C, (ch + 1) * C)
            vb[d, tl] = (pk_ref[0, rows, _packed_cols(_PK_U, tl)].astype(F32) - ws[d, tl][:C]).astype(BF16)
        for d, tl in tile_ids:
            pk_ref, kupt_ref, egt_ref, o_ref = dirs[d]
            ch = step if d == 0 else DN_SCAN_CHUNKS - 1 - step
            rows = slice(ch * C, (ch + 1) * C)
            sl = slice(tl * TW, (tl + 1) * TW)
            for part in range(ppt):
                ls = slice(part * LANES, (part + 1) * LANES)
                upd = _dot(kupt_ref[0, ch, 0, tl, ls, :], vb[d, tl][:, ls])
                pr = ppt * tl + part
                state[d, pr] = state[d, pr] * egt_ref[0, ch, 0, tl:tl + 1, ls] + jnp.where(on_diag, upd, 0.0)
            o4 = ws[d, tl][C:] + _dot(pk_ref[0, rows, _packed_cols(_PK_QKD, tl)], _block_diag(vb[d, tl], head))
            o_ref[0, rows, sl] = o4.astype(BF16)
    for dp, s in state.items():
        st_ref[dp[0], dp[1]] = s


def _dn_scan(packed, kupt, egt):
    B, S, pw = packed.shape
    C = DN_CHUNK
    sc = DN_SCAN_CHUNKS
    nb = S // (C * sc)
    fwd3 = lambda b, i: (b, i, 0)
    bwd3 = lambda b, i: (b, nb - 1 - i, 1)
    row_f = pl.BlockSpec((1, sc * C, pw // 2), fwd3)
    row_b = pl.BlockSpec((1, sc * C, pw // 2), bwd3)
    kup_f = pl.BlockSpec((1, sc, 1, DN_TILES, DN_TILE, C), lambda b, i: (b, i, 0, 0, 0, 0))
    kup_b = pl.BlockSpec((1, sc, 1, DN_TILES, DN_TILE, C), lambda b, i: (b, nb - 1 - i, 1, 0, 0, 0))
    egt_f = pl.BlockSpec((1, sc, 1, DN_TILES, DN_TILE), lambda b, i: (b, i, 0, 0, 0))
    egt_b = pl.BlockSpec((1, sc, 1, DN_TILES, DN_TILE), lambda b, i: (b, nb - 1 - i, 1, 0, 0))
    return pl.pallas_call(
        _dn_scan_kernel,
        grid=(B, nb),
        in_specs=[row_f, kup_f, egt_f, row_b, kup_b, egt_b],
        out_specs=[pl.BlockSpec((1, sc * C, DN_V), fwd3),
                   pl.BlockSpec((1, sc * C, DN_V), lambda b, i: (b, nb - 1 - i, 0))],
        out_shape=[jax.ShapeDtypeStruct((B, S, DN_V), BF16), jax.ShapeDtypeStruct((B, S, DN_V), BF16)],
        scratch_shapes=[pltpu.VMEM((2, DN_PAIRS, LANES, LANES), F32)],
        compiler_params=_cparams(("parallel", "arbitrary")),
        name="dn_scan",
    )(packed, kupt, egt, packed, kupt, egt)


def _mem_kv_kernel(m_ref, g_ref, w_ref, kg_ref, k_ref, v_ref):
    x = m_ref[...]
    ms = jnp.mean(x * x, axis=-1, keepdims=True)
    h = ((x * lax.rsqrt(ms + EPS)) * g_ref[...]).astype(BF16)
    kv = _dot(h, w_ref[...])
    for hd in range(MEM_HEADS):
        kh = kv[:, hd * LANES:(hd + 1) * LANES]
        kms = jnp.mean(kh * kh, axis=-1, keepdims=True)
        k_ref[:, hd * LANES:(hd + 1) * LANES] = ((kh * lax.rsqrt(kms + EPS)) * kg_ref[...]).astype(BF16)
    v_ref[...] = kv[:, MEM_Q:].astype(BF16)


def _mem_kv(mem2, g, w_kv, kg):
    n = mem2.shape[0]
    return pl.pallas_call(
        _mem_kv_kernel,
        grid=(n // N_MEM,),
        in_specs=[pl.BlockSpec((N_MEM, D_MODEL), lambda i: (i, 0)),
                  pl.BlockSpec((1, D_MODEL), lambda i: (0, 0)),
                  pl.BlockSpec((D_MODEL, 2 * MEM_Q), lambda i: (0, 0)),
                  pl.BlockSpec((1, MEM_HEAD_DIM), lambda i: (0, 0))],
        out_specs=[pl.BlockSpec((N_MEM, MEM_Q), lambda i: (i, 0)), pl.BlockSpec((N_MEM, MEM_Q), lambda i: (i, 0))],
        out_shape=[jax.ShapeDtypeStruct((n, MEM_Q), BF16), jax.ShapeDtypeStruct((n, MEM_Q), BF16)],
        compiler_params=_cparams(("parallel",)),
        name="mem_kv",
    )(mem2, g, w_kv, kg)


def _mem_attend(q_ref, k_ref, v_ref, qg_ref):
    scale = MEM_HEAD_DIM ** -0.5 * LOG2E
    ones = jnp.ones((N_MEM, LANES), BF16)
    heads = []
    for hd in range(MEM_HEADS):
        sl = slice(hd * LANES, (hd + 1) * LANES)
        q = q_ref[:, sl].astype(F32)
        qms = jnp.mean(q * q, axis=-1, keepdims=True)
        qn = ((q * lax.rsqrt(qms + EPS)) * qg_ref[...] * scale).astype(BF16)
        s = _dot_nt(qn, k_ref[0, :, sl])
        p = jnp.exp2(s - jnp.max(s, axis=-1, keepdims=True))
        r = _dot(p.astype(BF16), jnp.concatenate([v_ref[0, :, sl], ones], axis=1))
        heads.append((r[:, :LANES] * (1.0 / r[:, LANES:])).astype(BF16))
    return jnp.concatenate(heads, axis=1)


def _merge_kernel(x_ref, ya_ref, of_ref, ob_ref, z_ref, mq_ref, mk_ref, mv_ref, mqg_ref, ng_ref, wg_ref, og_ref,
                  wa_ref, wd_ref, wm_ref, wo_ref, o_ref):
    y_mem = _mem_attend(mq_ref, mk_ref, mv_ref, mqg_ref)
    x = x_ref[...]
    ms = jnp.mean(x * x, axis=-1, keepdims=True)
    h = ((x * lax.rsqrt(ms + EPS)) * ng_ref[...]).astype(BF16)
    o = of_ref[...].astype(F32) + ob_ref[...].astype(F32)
    z = z_ref[...].astype(F32)
    og = og_ref[...]
    parts = []
    for pc in range(DN_V // LANES):
        os_ = o[:, pc * LANES:(pc + 1) * LANES]
        on = os_ * lax.rsqrt(_half_sums(os_ * os_) * (1.0 / DN_VALUE_DIM) + EPS) * og
        parts.append((on * _silu(z[:, pc * LANES:(pc + 1) * LANES])).astype(BF16))
    y_dn = jnp.concatenate(parts, axis=1)
    def gate(b):
        return _sigmoid(_dot(h, wg_ref[:, b * D_MODEL:(b + 1) * D_MODEL]))

    merged = (gate(0) * _dot(ya_ref[...], wa_ref[...])
              + gate(1) * _dot(y_dn, wd_ref[...])
              + gate(2) * _dot(y_mem, wm_ref[...]))
    o_ref[...] = x + _dot(merged.astype(BF16), wo_ref[...])


def _merge(x2, ya, of, ob, z, mq, mk, mv, mqg, ng, wg, og, wa, wd, wm, wo):
    n = x2.shape[0]
    tm = ROW_TILE
    tiles_per_seq = n // mk.shape[0] // tm
    row = lambda w_: pl.BlockSpec((tm, w_), lambda i: (i, 0))
    mem = pl.BlockSpec((1, N_MEM, MEM_Q), lambda i: (i // tiles_per_seq, 0, 0))
    full = lambda a, b: _resident((a, b))
    return pl.pallas_call(
        _merge_kernel,
        grid=(n // tm,),
        in_specs=[row(D_MODEL), row(ATTN_Q), row(DN_V), row(DN_V), row(DN_V), row(MEM_Q), mem, mem,
                  full(1, MEM_HEAD_DIM), full(1, D_MODEL), full(D_MODEL, N_BRANCH * D_MODEL),
                  full(1, LANES), full(ATTN_Q, D_MODEL), full(DN_V, D_MODEL), full(MEM_Q, D_MODEL),
                  full(D_MODEL, D_MODEL)],
        out_specs=row(D_MODEL),
        out_shape=jax.ShapeDtypeStruct((n, D_MODEL), F32),
        compiler_params=_cparams(("parallel",)),
        name="merge",
    )(x2, ya, of, ob, z, mq, mk, mv, mqg, ng, wg, og, wa, wd, wm, wo)


def _ffn_kernel(xp_ref, xc_ref, xn_ref, g_ref, wu_ref, cw_ref, cb_ref, wd_ref, o_ref, act_ref):
    i = pl.program_id(1)
    nt = pl.num_programs(1)
    tm = xc_ref.shape[1]
    xc = xc_ref[0]
    prev = jnp.where(i == 0, 0.0, xp_ref[0])
    nxt = jnp.where(i == nt - 1, 0.0, xn_ref[0])
    xe = jnp.concatenate([prev, xc, nxt], axis=0)
    ms = jnp.mean(xe * xe, axis=-1, keepdims=True)
    h = ((xe * lax.rsqrt(ms + EPS)) * g_ref[...]).astype(BF16)

    def both(ref, rows_, c0):
        return jnp.concatenate([ref[rows_, c0:c0 + FF_CHUNK], ref[rows_, D_FF + c0:D_FF + c0 + FF_CHUNK]], axis=1)

    for c in range(D_FF // FF_CHUNK):
        c0 = c * FF_CHUNK
        u = _dot(h, both(wu_ref, slice(None), c0))
        y = None
        for j in range(FFN_CONV):
            term = _shift_rows(u, FFN_CONV // 2 - j)[HALO:HALO + tm] * both(cw_ref, slice(j, j + 1), c0)
            y = term if y is None else y + term
        y = y + both(cb_ref, slice(None), c0)
        act_ref[:, c0:c0 + FF_CHUNK] = (_silu(y[:, :FF_CHUNK]) * y[:, FF_CHUNK:]).astype(BF16)
    for n0 in range(0, D_MODEL, D_MODEL // 2):
        ns = slice(n0, n0 + D_MODEL // 2)
        o_ref[0, :, ns] = xc[:, ns] + _dot(act_ref[...], wd_ref[:, ns])


def _ffn(x1, g, wu, cw, cb, wd):
    B, S, _ = x1.shape
    tm = ROW_TILE
    nt = S // tm
    hb = tm // HALO
    return pl.pallas_call(
        _ffn_kernel,
        grid=(B, nt),
        in_specs=[pl.BlockSpec((1, HALO, D_MODEL), lambda b, i: (b, jnp.maximum(i * hb - 1, 0), 0)),
                  pl.BlockSpec((1, tm, D_MODEL), lambda b, i: (b, i, 0)),
                  pl.BlockSpec((1, HALO, D_MODEL), lambda b, i: (b, jnp.minimum((i + 1) * hb, S // HALO - 1), 0)),
                  _resident((1, D_MODEL)), _resident((D_MODEL, 2 * D_FF)), _resident((FFN_CONV, 2 * D_FF)),
                  _resident((1, 2 * D_FF)), _resident((D_FF, D_MODEL))],
        out_specs=pl.BlockSpec((1, tm, D_MODEL), lambda b, i: (b, i, 0)),
        out_shape=jax.ShapeDtypeStruct((B, S, D_MODEL), F32),
        scratch_shapes=[pltpu.VMEM((tm, D_FF), BF16)],
        compiler_params=_cparams(("parallel", "parallel")),
        name="ffn",
    )(x1, x1, x1, g, wu, cw, cb, wd)


def _permute_w_in(w):
    idx = np.cumsum((0,) + IN_SPLITS)
    wb = w.astype(BF16)
    return wb[:, :idx[7]], wb[:, idx[9]:idx[10]], wb[:, idx[7]:idx[9]], wb[:, idx[10]:]


def _layer(x, mem, rel_bias_table, p):
    B, S, D = x.shape
    n = B * S
    x2 = x.reshape(n, D)
    row = lambda a: a.reshape(1, -1).astype(F32)
    tile2 = lambda a: jnp.tile(a.astype(F32), 2)[None]
    w_main, w_mq, w_ba, w_gate = _permute_w_in(p["w_in"])
    aq, akv, dz, mq, qkvn, betax, gcx, gct = _inproj(
        x, row(p["norm_mix_g"]), w_main, w_mq, w_ba, tile2(p["attn_q_norm_g"]), tile2(p["attn_k_norm_g"]),
        p["dn_conv_w"], p["dn_a_log"], p["dn_dt_bias"])
    y_attn = _attn(aq, akv, rel_bias_table, p["attn_sink"])
    o_f, o_b = _dn_scan(*_dn_chunk(qkvn, betax, gcx, gct))
    mk, mv = _mem_kv(mem.reshape(B * N_MEM, D), row(p["mem_norm_g"]), p["mem_w_kv"].astype(BF16),
                     row(p["mem_k_norm_g"]))
    og2 = jnp.tile(p["dn_out_norm_g"].astype(F32), 2)[None]
    x1 = _merge(x2, y_attn.reshape(n, -1), o_f.reshape(n, -1), o_b.reshape(n, -1), dz.reshape(n, -1),
                mq.reshape(n, -1), mk.reshape(B, N_MEM, -1), mv.reshape(B, N_MEM, -1), row(p["mem_q_norm_g"]),
                row(p["norm_mix_g"]), w_gate, og2, p["w_br_attn"].astype(BF16), p["w_br_dn"].astype(BF16),
                p["w_br_mem"].astype(BF16), p["w_out"].astype(BF16))
    return _ffn(x1.reshape(B, S, D), row(p["norm_ffn_g"]), p["ffn_w_up"].astype(BF16), p["ffn_conv_w"].astype(F32),
                row(p["ffn_conv_b"]), p["ffn_w_down"].astype(BF16))


_LAYER_PARAMS = ("norm_mix_g", "w_in", "attn_q_norm_g", "attn_k_norm_g", "attn_sink", "dn_conv_w", "dn_a_log",
                 "dn_dt_bias", "dn_out_norm_g", "mem_norm_g", "mem_w_kv", "mem_q_norm_g", "mem_k_norm_g",
                 "w_br_attn", "w_br_dn", "w_br_mem", "w_out", "norm_ffn_g", "ffn_w_up", "ffn_conv_w", "ffn_conv_b",
                 "ffn_w_down")


def kernel(x, mem, rel_bias_table, norm_mix_g, w_in, attn_q_norm_g, attn_k_norm_g, attn_sink, dn_conv_w, dn_a_log,
           dn_dt_bias, dn_out_norm_g, mem_norm_g, mem_w_kv, mem_q_norm_g, mem_k_norm_g, w_br_attn, w_br_dn,
           w_br_mem, w_out, norm_ffn_g, ffn_w_up, ffn_conv_w, ffn_conv_b, ffn_w_down):
    stacked = dict(zip(_LAYER_PARAMS, (norm_mix_g, w_in, attn_q_norm_g, attn_k_norm_g, attn_sink, dn_conv_w,
                                       dn_a_log, dn_dt_bias, dn_out_norm_g, mem_norm_g, mem_w_kv, mem_q_norm_g,
                                       mem_k_norm_g, w_br_attn, w_br_dn, w_br_mem, w_out, norm_ffn_g, ffn_w_up,
                                       ffn_conv_w, ffn_conv_b, ffn_w_down)))
    depth = w_in.shape[0]
    for l in range(depth):
        x = _layer(x, mem, rel_bias_table, {k: v[l] for k, v in stacked.items()})
    return x
```
